```python
import math
import jax, jax.numpy as jnp
from jax import lax
import numpy as np

D_MODEL = 1024
BATCH = 16
SEQ = 2048
DEPTH = 1

CTX_LEN = 256
GRID_W = 64

MIX_WIDTH = D_MODEL
DN_HEAD_DIM = 128
DN_WIDTH = MIX_WIDTH // 2
DN_HEADS = DN_WIDTH // DN_HEAD_DIM
NA_HEAD_DIM = 64
NA_WIDTH = MIX_WIDTH - DN_WIDTH
NA_HEADS = NA_WIDTH // NA_HEAD_DIM

DN_IN = 4 * DN_WIDTH + 4 * DN_HEADS
NA_IN = 3 * NA_WIDTH
IN_COLS = DN_IN + NA_IN

CONV_K = 5
CHUNK = 64

WIN_ROWS_MAX = 8
WIN_COLS = 16
QB_W = 16
KB_W = 32
N_COL_BLK = GRID_W // QB_W

D_FF = -(-8 * D_MODEL // (3 * 256)) * 256
EPS = 1e-6

kernel_name = "hymba_gdn_natten_dit_prefix"


def rms_norm(x, w):
    xf = x.astype(jnp.float32)
    y = xf * lax.rsqrt(jnp.mean(xf * xf, axis=-1, keepdims=True) + EPS)
    return (y * w.astype(jnp.float32)).astype(x.dtype)


def l2_normalize(x):
    xf = x.astype(jnp.float32)
    return xf * lax.rsqrt(jnp.sum(xf * xf, axis=-1, keepdims=True) + EPS)


def modulate(h, shift, scale):
    return h * (1 + scale) + shift


def short_conv(u, w):
    out = lax.conv_general_dilated(
        u, w[:, None, :].astype(u.dtype), window_strides=(1,),
        padding=[(CONV_K // 2, CONV_K // 2)],
        dimension_numbers=("NWC", "WIO", "NWC"), feature_group_count=u.shape[-1])
    return jax.nn.silu(out)


def chunk_gated_delta_rule(q, k, v, g, beta, S0):
    B, H, L, dk = q.shape
    dv = v.shape[-1]
    n = L // CHUNK
    f32 = jnp.float32
    q = q.astype(f32).reshape(B, H, n, CHUNK, dk)
    k = k.astype(f32).reshape(B, H, n, CHUNK, dk)
    v = v.astype(f32).reshape(B, H, n, CHUNK, dv)
    beta = beta.astype(f32).reshape(B, H, n, CHUNK)
    gam = jnp.cumsum(g.astype(f32).reshape(B, H, n, CHUNK), axis=-1)
    tril = jnp.tril(jnp.ones((CHUNK, CHUNK), dtype=bool))
    strict = jnp.tril(jnp.ones((CHUNK, CHUNK), dtype=bool), -1)
    decay = jnp.exp(jnp.where(tril, gam[..., :, None] - gam[..., None, :], -jnp.inf))
    kb = k * beta[..., None]
    m = jnp.where(strict, jnp.einsum("bhncd,bhnmd->bhncm", kb, k) * decay, 0.0)
    a_mat = jnp.eye(CHUNK, dtype=f32) + m
    u = lax.linalg.triangular_solve(a_mat, v * beta[..., None], left_side=True, lower=True, unit_diagonal=True)
    w = lax.linalg.triangular_solve(a_mat, kb * jnp.exp(gam)[..., None], left_side=True, lower=True, unit_diagonal=True)
    qk = jnp.where(tril, jnp.einsum("bhncd,bhnmd->bhncm", q, k) * decay, 0.0)

    def step(S, xs):
        q_i, k_i, u_i, w_i, qk_i, gam_i = xs
        v_new = u_i - jnp.einsum("bhck,bhkv->bhcv", w_i, S)
        o_i = (jnp.einsum("bhck,bhkv->bhcv", q_i * jnp.exp(gam_i)[..., None], S)
               + jnp.einsum("bhcm,bhmv->bhcv", qk_i, v_new))
        g_last = gam_i[..., -1:]
        k_dec = k_i * jnp.exp(g_last - gam_i)[..., None]
        S = S * jnp.exp(g_last)[..., None] + jnp.einsum("bhck,bhcv->bhkv", k_dec, v_new)
        return S, o_i

    xs = tuple(jnp.moveaxis(t, 2, 0) for t in (q, k, u, w, qk, gam))
    S, o = lax.scan(step, S0.astype(f32), xs)
    o = jnp.moveaxis(o, 0, 2).reshape(B, H, L, dv)
    return o, S


def gated_deltanet(p, conv_w, A_log, dt_bias, out_norm_w, S0_f, S0_b):
    B, L, _ = p.shape
    qkv = short_conv(p[..., :3 * DN_WIDTH], conv_w)
    z = p[..., 3 * DN_WIDTH:4 * DN_WIDTH].reshape(B, L, DN_HEADS, DN_HEAD_DIM)
    a = p[..., 4 * DN_WIDTH:4 * DN_WIDTH + 2 * DN_HEADS].reshape(B, L, 2, DN_HEADS)
    b = p[..., 4 * DN_WIDTH + 2 * DN_HEADS:].reshape(B, L, 2, DN_HEADS)
    q, k, v = [t.reshape(B, L, DN_HEADS, DN_HEAD_DIM).transpose(0, 2, 1, 3) for t in jnp.split(qkv, 3, axis=-1)]
    q = l2_normalize(q) * DN_HEAD_DIM ** -0.5
    k = l2_normalize(k)
    g = (-jnp.exp(A_log.astype(jnp.float32))
         * jax.nn.softplus(a.astype(jnp.float32) + dt_bias.astype(jnp.float32))).transpose(2, 0, 3, 1)
    beta = jax.nn.sigmoid(b.astype(jnp.float32)).transpose(2, 0, 3, 1)
    o_f, S_f = chunk_gated_delta_rule(q, k, v, g[0], beta[0], S0_f)
    rev = lambda t: jnp.flip(t, axis=2)
    o_b, S_b = chunk_gated_delta_rule(rev(q), rev(k), rev(v), rev(g[1]), rev(beta[1]), S0_b)
    o = (o_f + rev(o_b)).transpose(0, 2, 1, 3)
    o = rms_norm(o, out_norm_w) * jax.nn.silu(z.astype(jnp.float32))
    return o.reshape(B, L, DN_WIDTH).astype(p.dtype), S_f, S_b


def na_qkv(p, q_norm_w, k_norm_w):
    B, L, _ = p.shape
    q, k, v = [t.reshape(B, L, NA_HEADS, NA_HEAD_DIM) for t in jnp.split(p, 3, axis=-1)]
    q = rms_norm(q, q_norm_w) * NA_HEAD_DIM ** -0.5
    k = rms_norm(k, k_norm_w)
    return q, k, v


def na_column_tables():
    cols = np.arange(GRID_W)
    win_start = np.clip(cols - WIN_COLS // 2, 0, GRID_W - WIN_COLS).reshape(N_COL_BLK, QB_W)
    blk_start = np.clip(np.arange(N_COL_BLK) * QB_W - WIN_COLS // 2, 0, GRID_W - KB_W)
    key_cols = blk_start[:, None] + np.arange(KB_W)
    q_cols = cols.reshape(N_COL_BLK, QB_W)
    kc = key_cols[:, None, :]
    valid = (kc >= win_start[:, :, None]) & (kc < win_start[:, :, None] + WIN_COLS)
    rel_idx = np.clip(kc - q_cols[:, :, None] + WIN_COLS - 1, 0, 2 * WIN_COLS - 2)
    return key_cols, valid, rel_idx


def neighborhood_attention(q, k, v, k_ctx, v_ctx, rpb, rows):
    B, L, H, Dh = q.shape
    win_rows = min(WIN_ROWS_MAX, rows)
    n_win = win_rows * KB_W
    key_cols, col_valid, rel_col_idx = na_column_tables()
    qg = q.reshape(B, rows, N_COL_BLK, QB_W, H, Dh)
    kg = k.reshape(B, rows, GRID_W, H, Dh)
    vg = v.reshape(B, rows, GRID_W, H, Dh)
    bias_cols = rpb[:, :, rel_col_idx]

    def row_block(r):
        r0 = jnp.clip(r - win_rows // 2, 0, rows - win_rows)
        q_r = lax.dynamic_index_in_dim(qg, r, axis=1, keepdims=False)
        k_r = lax.dynamic_slice_in_dim(kg, r0, win_rows, axis=1)[:, :, key_cols]
        v_r = lax.dynamic_slice_in_dim(vg, r0, win_rows, axis=1)[:, :, key_cols]
        dr_idx = r0 + jnp.arange(win_rows) - r + WIN_ROWS_MAX - 1
        bias = jnp.take(bias_cols, dr_idx, axis=1).transpose(0, 2, 3, 1, 4)
        s_win = jnp.einsum("bnqhd,binchd->bhnqic", q_r, k_r).astype(jnp.float32) + bias.astype(jnp.float32)
        s_win = jnp.where(col_valid[:, :, None, :], s_win, -jnp.inf)
        s_ctx = jnp.einsum("bnqhd,bkhd->bhnqk", q_r, k_ctx).astype(jnp.float32)
        s = jnp.concatenate([s_win.reshape(B, H, N_COL_BLK, QB_W, n_win), s_ctx], axis=-1)
        p = jax.nn.softmax(s, axis=-1).astype(v.dtype)
        p_win = p[..., :n_win].reshape(B, H, N_COL_BLK, QB_W, win_rows, KB_W)
        o = (jnp.einsum("bhnqic,binchd->bnqhd", p_win, v_r)
             + jnp.einsum("bhnqk,bkhd->bnqhd", p[..., n_win:], v_ctx))
        return o.reshape(B, GRID_W, H, Dh)

    out = lax.map(row_block, jnp.arange(rows))
    return jnp.moveaxis(out, 0, 1).reshape(B, L, H * Dh)


def context_attention(q, k, v):
    B, N, H, Dh = q.shape
    s = jnp.einsum("bqhd,bkhd->bhqk", q, k).astype(jnp.float32)
    p = jax.nn.softmax(s, axis=-1).astype(v.dtype)
    return jnp.einsum("bhqk,bkhd->bqhd", p, v).reshape(B, N, H * Dh)


def swiglu(h, w_in, w_out):
    gate, up = jnp.split(h @ w_in, 2, axis=-1)
    return (jax.nn.silu(gate) * up) @ w_out


def setup_inputs(seed: int = 0) -> dict:
    key = jax.random.key(seed)
    ks = jax.random.split(key, 20)
    nrm = lambda k, shape, s: jax.random.normal(k, shape, jnp.float32) * s
    A = jax.random.uniform(ks[10], (DEPTH, 2, DN_HEADS), jnp.float32, minval=1.0, maxval=16.0)
    dt = jnp.exp(jax.random.uniform(ks[11], (DEPTH, 2, DN_HEADS), jnp.float32,
                                    minval=math.log(1e-3), maxval=math.log(1e-1)))
    return {
        "x": nrm(ks[0], (BATCH, SEQ, D_MODEL), 1.0),
        "c": nrm(ks[1], (BATCH, D_MODEL), 1.0),
        "ctx": nrm(ks[2], (BATCH, CTX_LEN, D_MODEL), 1.0),
        "c_ctx": nrm(ks[3], (D_MODEL,), 1.0),
        "norm1_w": 1.0 + nrm(ks[4], (DEPTH, D_MODEL), 0.02),
        "norm2_w": 1.0 + nrm(ks[5], (DEPTH, D_MODEL), 0.02),
        "w_ada": nrm(ks[6], (DEPTH, D_MODEL, 6 * D_MODEL), 0.5 * D_MODEL ** -0.5),
        "b_ada": nrm(ks[7], (DEPTH, 6 * D_MODEL), 0.02),
        "w_in": nrm(ks[8], (DEPTH, D_MODEL, IN_COLS), D_MODEL ** -0.5),
        "dn_conv_w": nrm(ks[9], (DEPTH, CONV_K, 3 * DN_WIDTH), CONV_K ** -0.5),
        "dn_A_log": jnp.log(A),
        "dn_dt_bias": dt + jnp.log(-jnp.expm1(-dt)),
        "dn_out_norm_w": 1.0 + nrm(ks[12], (DEPTH, DN_HEAD_DIM), 0.02),
        "na_q_norm_w": 1.0 + nrm(ks[13], (DEPTH, NA_HEAD_DIM), 0.02),
        "na_k_norm_w": 1.0 + nrm(ks[14], (DEPTH, NA_HEAD_DIM), 0.02),
        "na_rpb": nrm(ks[15], (DEPTH, NA_HEADS, 2 * WIN_ROWS_MAX - 1, 2 * WIN_COLS - 1), 0.1),
        "w_out": nrm(ks[16], (DEPTH, MIX_WIDTH, D_MODEL), MIX_WIDTH ** -0.5),
        "w_ffn_in": nrm(ks[17], (DEPTH, D_MODEL, 2 * D_FF), D_MODEL ** -0.5),
        "w_ffn_out": nrm(ks[18], (DEPTH, D_FF, D_MODEL), D_FF ** -0.5),
    }


def reference(x, c, ctx, c_ctx, norm1_w, norm2_w, w_ada, b_ada, w_in, dn_conv_w, dn_A_log, dn_dt_bias,
              dn_out_norm_w, na_q_norm_w, na_k_norm_w, na_rpb, w_out, w_ffn_in, w_ffn_out):
    B, L, _ = x.shape
    rows = L // GRID_W
    for i in range(DEPTH):
        last = i == DEPTH - 1
        mod_x = (jax.nn.silu(c) @ w_ada[i] + b_ada[i])[:, None, :]
        mod_c = jax.nn.silu(c_ctx) @ w_ada[i] + b_ada[i]
        sh1, sc1, g1, sh2, sc2, g2 = jnp.split(mod_x, 6, axis=-1)
        csh1, csc1, cg1, csh2, csc2, cg2 = jnp.split(mod_c, 6, axis=-1)

        px = modulate(rms_norm(x, norm1_w[i]), sh1, sc1) @ w_in[i]
        pc = modulate(rms_norm(ctx, norm1_w[i]), csh1, csc1) @ w_in[i]

        s_zero = jnp.zeros((B, DN_HEADS, DN_HEAD_DIM, DN_HEAD_DIM), jnp.float32)
        dn_c, s_ctx_f, s_ctx_b = gated_deltanet(pc[..., :DN_IN], dn_conv_w[i], dn_A_log[i], dn_dt_bias[i],
                                                dn_out_norm_w[i], s_zero, s_zero)
        dn_x, _, _ = gated_deltanet(px[..., :DN_IN], dn_conv_w[i], dn_A_log[i], dn_dt_bias[i],
                                    dn_out_norm_w[i], s_ctx_f, s_ctx_b)

        qx, kx, vx = na_qkv(px[..., DN_IN:], na_q_norm_w[i], na_k_norm_w[i])
        qc, kc, vc = na_qkv(pc[..., DN_IN:], na_q_norm_w[i], na_k_norm_w[i])
        na_x = neighborhood_attention(qx, kx, vx, kc, vc, na_rpb[i], rows)

        x = x + g1 * (jnp.concatenate([dn_x, na_x], axis=-1) @ w_out[i])
        x = x + g2 * swiglu(modulate(rms_norm(x, norm2_w[i]), sh2, sc2), w_ffn_in[i], w_ffn_out[i])

        if not last:
            na_c = context_attention(qc, kc, vc)
            ctx = ctx + cg1 * (jnp.concatenate([dn_c, na_c], axis=-1) @ w_out[i])
            ctx = ctx + cg2 * swiglu(modulate(rms_norm(ctx, norm2_w[i]), csh2, csc2), w_ffn_in[i], w_ffn_out[i])
    return x
```

```python
import functools

import numpy as np
import jax
import jax.numpy as jnp
from jax import lax
from jax.experimental import pallas as pl
from jax.experimental.pallas import tpu as pltpu

F32 = jnp.float32
BF16 = jnp.bfloat16
HIGHEST = lax.Precision.HIGHEST

EPS = 1e-6
CHUNK = 64
CONV_K = 5
DN_HEAD_DIM = 128
DN_HEADS = 4
NA_HEAD_DIM = 64
NA_HEADS = 8
GRID_W = 64
WIN_ROWS = 8
WIN_COLS = 16
LANES = 128
VMEM_LIMIT = 56 * 1024 * 1024

DN_HB = 4
ROW_TILE = 512
FFN_SPLITS = (1024, 1024, 768)


def _silu(x):
    return x * (1.0 / (1.0 + jnp.exp(-x)))


def _softplus(x):
    return jnp.maximum(x, 0.0) + jnp.log(1.0 + jnp.exp(-jnp.abs(x)))


def _mm(a, b):
    return jnp.dot(a.astype(BF16), b.astype(BF16), preferred_element_type=F32)


def _mm_nt(a, b):
    return lax.dot_general(a.astype(BF16), b.astype(BF16), (((1,), (1,)), ((), ())),
                           preferred_element_type=F32)


def _mm_tn(a, b):
    return lax.dot_general(a.astype(BF16), b.astype(BF16), (((0,), (0,)), ((), ())),
                           preferred_element_type=F32)


def _mm_f32(a, b):
    return lax.dot_general(a, b, (((1,), (0,)), ((), ())), precision=HIGHEST,
                           preferred_element_type=F32)


def _adaln_kernel(c_ref, w_ref, b_ref, o_ref):
    o_ref[...] = _mm_f32(_silu(c_ref[...]), w_ref[...]) + b_ref[...]


def _adaln(c_all, w_ada, b_ada):
    rows, d = c_all.shape
    n = w_ada.shape[1]
    tn = 1024
    return pl.pallas_call(
        _adaln_kernel,
        grid=(n // tn,),
        in_specs=[
            pl.BlockSpec((rows, d), lambda j: (0, 0)),
            pl.BlockSpec((d, tn), lambda j: (0, j)),
            pl.BlockSpec((1, tn), lambda j: (0, j)),
        ],
        out_specs=pl.BlockSpec((rows, tn), lambda j: (0, j)),
        out_shape=jax.ShapeDtypeStruct((rows, n), F32),
        compiler_params=pltpu.CompilerParams(
            dimension_semantics=("arbitrary",), vmem_limit_bytes=VMEM_LIMIT),
        name="adaln",
    )(c_all, w_ada, b_ada)


def _inproj_kernel(x_ref, mod_ref, nw_ref, w_ref, oq_ref, oz_ref, on_ref, oab_ref, *, n_dn, n_z, n_na):
    x = x_ref[0]
    ms = jnp.mean(x * x, axis=-1, keepdims=True)
    h = (x * lax.rsqrt(ms + EPS)) * nw_ref[...]
    h = h * (1.0 + mod_ref[0, 1:2, :]) + mod_ref[0, 0:1, :]
    hb = h.astype(BF16)
    c0, c1, c2 = n_dn, n_dn + n_z, n_dn + n_z + n_na
    oq_ref[0] = jnp.dot(hb, w_ref[:, 0:c0], preferred_element_type=F32).astype(BF16)
    oz_ref[0] = jnp.dot(hb, w_ref[:, c0:c1], preferred_element_type=F32).astype(BF16)
    on_ref[0] = jnp.dot(hb, w_ref[:, c1:c2], preferred_element_type=F32).astype(BF16)
    oab_ref[0] = jnp.dot(hb, w_ref[:, c2:], preferred_element_type=F32)


def _in_proj(x, mods, norm_w, w_perm, *, n_dn, n_z, n_na, per_batch_mod, tm):
    b, l, d = x.shape
    n_all = w_perm.shape[1]
    n_ab = n_all - (n_dn + n_z + n_na)
    tm = min(tm, l)
    mod_map = (lambda bi, i: (bi, 0, 0)) if per_batch_mod else (lambda bi, i: (0, 0, 0))
    row_map = lambda bi, i: (bi, i, 0)
    return pl.pallas_call(
        functools.partial(_inproj_kernel, n_dn=n_dn, n_z=n_z, n_na=n_na),
        grid=(b, l // tm),
        in_specs=[
            pl.BlockSpec((1, tm, d), row_map),
            pl.BlockSpec((1, 6, d), mod_map),
            pl.BlockSpec((1, d), lambda bi, i: (0, 0)),
            pl.BlockSpec((d, n_all), lambda bi, i: (0, 0)),
        ],
        out_specs=[
            pl.BlockSpec((1, tm, n_dn), row_map),
            pl.BlockSpec((1, tm, n_z), row_map),
            pl.BlockSpec((1, tm, n_na), row_map),
            pl.BlockSpec((1, tm, n_ab), row_map),
        ],
        out_shape=[
            jax.ShapeDtypeStruct((b, l, n_dn), BF16),
            jax.ShapeDtypeStruct((b, l, n_z), BF16),
            jax.ShapeDtypeStruct((b, l, n_na), BF16),
            jax.ShapeDtypeStruct((b, l, n_ab), F32),
        ],
        compiler_params=pltpu.CompilerParams(
            dimension_semantics=("arbitrary", "arbitrary"), vmem_limit_bytes=VMEM_LIMIT),
        name="in_proj",
    )(x, mods, norm_w, w_perm)


def _dn_kernel(qx_ref, kx_ref, vx_ref, qc_ref, kc_ref, vc_ref, z_ref,
               gcx_ref, grx_ref, gcc_ref, grc_ref,
               cwq_ref, cwk_ref, cwv_ref, pc_ref, pr_ref, onw_ref,
               out_ref,
               upad, q_s, k_s, v_s, gc_s, gr_s, s_s, o_s, *, hb, seq, ctx_len):
    n_ctx = ctx_len // CHUNK
    n_lat = seq // CHUNK
    w = hb * DN_HEAD_DIM
    tile = 128

    def conv_into(src_ref, n_rows, cw_ref, dst_ref, dst_off, mode):
        upad[0:16, :] = jnp.zeros((16, w), BF16)
        upad[16:16 + n_rows, :] = src_ref[0]
        upad[16 + n_rows:32 + n_rows, :] = jnp.zeros((16, w), BF16)
        cw = cw_ref[...]

        def body(t, carry):
            r0 = pl.multiple_of(t * tile, tile)
            xe = upad[pl.ds(r0, tile + 32), :].astype(F32)
            acc = cw[0:1, :] * xe[14:14 + tile, :]
            for j in range(1, CONV_K):
                acc = acc + cw[j:j + 1, :] * xe[14 + j:14 + j + tile, :]
            y = _silu(acc)
            if mode != "v":
                parts = []
                for hh in range(hb):
                    seg = y[:, hh * DN_HEAD_DIM:(hh + 1) * DN_HEAD_DIM]
                    inv = lax.rsqrt(jnp.sum(seg * seg, axis=-1, keepdims=True) + EPS)
                    if mode == "q":
                        inv = inv * (DN_HEAD_DIM ** -0.5)
                    parts.append(seg * inv)
                y = parts[0] if hb == 1 else jnp.concatenate(parts, axis=1)
            dst_ref[pl.ds(dst_off + r0, tile), :] = y
            return carry

        lax.fori_loop(0, n_rows // tile, body, 0)

    conv_into(qc_ref, ctx_len, cwq_ref, q_s, 0, "q")
    conv_into(kc_ref, ctx_len, cwk_ref, k_s, 0, "k")
    conv_into(vc_ref, ctx_len, cwv_ref, v_s, 0, "v")
    conv_into(qx_ref, seq, cwq_ref, q_s, ctx_len, "q")
    conv_into(kx_ref, seq, cwk_ref, k_s, ctx_len, "k")
    conv_into(vx_ref, seq, cwv_ref, v_s, ctx_len, "v")

    ri = lax.broadcasted_iota(jnp.int32, (CHUNK, CHUNK), 0)
    ci = lax.broadcasted_iota(jnp.int32, (CHUNK, CHUNK), 1)
    lower = (ci <= ri).astype(F32)
    upper = (ci >= ri).astype(F32)
    a_log_c, dt_c = pc_ref[0, 0:1, :], pc_ref[0, 1:2, :]
    a_log_r, dt_r = pr_ref[0, :, 0:1], pr_ref[0, :, 1:2]
    lane = lax.broadcasted_iota(jnp.int32, (CHUNK, LANES), 1)
    sub = lax.broadcasted_iota(jnp.int32, (4 * hb, CHUNK), 0)

    def gates_into(gc_ref, gr_ref, n_chunks, off):
        def body(t, carry):
            xc = gc_ref[0, t]
            g = -jnp.exp(a_log_c) * _softplus(xc + dt_c)
            g = jnp.where(lane < 2 * hb, g, 0.0)
            gam = jnp.where(lane < hb, _mm_f32(lower, g), _mm_f32(upper, g))
            gc_s[off + t] = jnp.where(lane < 2 * hb, gam, 1.0 / (1.0 + jnp.exp(-xc)))
            xr = gr_ref[0, 0, t]
            g = -jnp.exp(a_log_r) * _softplus(xr + dt_r)
            g = jnp.where(sub < 2 * hb, g, 0.0)
            gam = jnp.where(sub < hb, _mm_f32(g, upper), _mm_f32(g, lower))
            gr_s[off + t] = jnp.where(sub < 2 * hb, gam, 1.0 / (1.0 + jnp.exp(-xr)))
            return carry

        lax.fori_loop(0, n_chunks, body, 0)

    gates_into(gcc_ref, grc_ref, n_ctx, 0)
    gates_into(gcx_ref, grx_ref, n_lat, n_ctx)

    eye64 = (ri == ci).astype(F32)
    r128 = lax.broadcasted_iota(jnp.int32, (DN_HEAD_DIM, DN_HEAD_DIM), 0)
    c128 = lax.broadcasted_iota(jnp.int32, (DN_HEAD_DIM, DN_HEAD_DIM), 1)
    eye128 = (r128 == c128).astype(F32)
    onw = onw_ref[...]

    def chain_step(d, hh, chunk, out_mode):
        c = d * hb + hh
        r0 = pl.multiple_of(chunk * CHUNK, CHUNK)
        hs = slice(hh * DN_HEAD_DIM, (hh + 1) * DN_HEAD_DIM)
        q = q_s[pl.ds(r0, CHUNK), hs]
        k = k_s[pl.ds(r0, CHUNK), hs]
        v = v_s[pl.ds(r0, CHUNK), hs]
        gcol = gc_s[chunk]
        grow = gr_s[chunk]
        gam_c = gcol[:, c:c + 1]
        beta_c = gcol[:, 2 * hb + c:2 * hb + c + 1]
        gam_r = grow[c:c + 1, :]
        g_tot = gam_c[CHUNK - 1:CHUNK, :] if d == 0 else gam_c[0:1, :]
        before_eq = (ci <= ri) if d == 0 else (ci >= ri)
        strict = (ci < ri) if d == 0 else (ci > ri)
        decay = jnp.exp(jnp.where(before_eq, gam_c - gam_r, -jnp.inf))

        qk_kk = _mm_nt(jnp.concatenate([q, k], axis=0), k)
        qkd = qk_kk[:CHUNK] * decay
        m = jnp.where(strict, qk_kk[CHUNK:] * beta_c * decay, 0.0)
        t_inv = eye64 - m
        pw = m
        for _ in range(5):
            pw = _mm(pw, pw)
            t_inv = t_inv + _mm(t_inv, pw)

        e_gam = jnp.exp(gam_c)
        uw = _mm(t_inv, jnp.concatenate([v * beta_c, k * (beta_c * e_gam)], axis=1))
        kdec = k * jnp.exp(g_tot - gam_c)
        qkw = _mm_tn(kdec, uw)
        p_mat = jnp.exp(g_tot) * eye128 - qkw[:, DN_HEAD_DIM:]
        s_old = s_s[c]
        s_s[c] = _mm(p_mat, s_old) + qkw[:, :DN_HEAD_DIM]
        if out_mode is None:
            return
        ow = _mm(qkd, uw)
        r_mat = q * e_gam - ow[:, DN_HEAD_DIM:]
        o = _mm(r_mat, s_old) + ow[:, :DN_HEAD_DIM]
        rl = pl.multiple_of(r0 - ctx_len, CHUNK)
        if out_mode == "store":
            o_s[pl.ds(rl, CHUNK), hs] = o
        else:
            o = o + o_s[pl.ds(rl, CHUNK), hs]
            y = o * lax.rsqrt(jnp.mean(o * o, axis=-1, keepdims=True) + EPS) * onw
            zz = z_ref[0, pl.ds(rl, CHUNK), hs].astype(F32)
            out_ref[0, pl.ds(rl, CHUNK), hs] = (y * _silu(zz)).astype(out_ref.dtype)

    s_s[...] = jnp.zeros(s_s.shape, F32)

    def ctx_body(i, carry):
        for hh in range(hb):
            chain_step(0, hh, i, None)
            chain_step(1, hh, n_ctx - 1 - i, None)
        return carry

    lax.fori_loop(0, n_ctx, ctx_body, 0)

    def lat_body(mode, i, carry):
        for hh in range(hb):
            chain_step(0, hh, n_ctx + i, mode)
            chain_step(1, hh, n_ctx + n_lat - 1 - i, mode)
        return carry

    lax.fori_loop(0, n_lat // 2, functools.partial(lat_body, "store"), 0)
    lax.fori_loop(n_lat // 2, n_lat, functools.partial(lat_body, "final"), 0)


def _deltanet(qkv_x, qkv_c, z_x, gcol_x, grow_x, gcol_c, grow_c, conv_w, pcol, prow, onw, *, hb):
    b, seq, _ = qkv_x.shape
    ctx_len = qkv_c.shape[1]
    ng = DN_HEADS // hb
    w = hb * DN_HEAD_DIM
    n_lat, n_ctx = seq // CHUNK, ctx_len // CHUNK
    assert n_lat % 2 == 0 and seq % 128 == 0 and ctx_len % 128 == 0

    def col(part):
        return lambda bi, g: (bi, 0, part * ng + g)

    def cwcol(part):
        return lambda bi, g: (0, part * ng + g)

    return pl.pallas_call(
        functools.partial(_dn_kernel, hb=hb, seq=seq, ctx_len=ctx_len),
        grid=(b, ng),
        in_specs=[
            pl.BlockSpec((1, seq, w), col(0)),
            pl.BlockSpec((1, seq, w), col(1)),
            pl.BlockSpec((1, seq, w), col(2)),
            pl.BlockSpec((1, ctx_len, w), col(0)),
            pl.BlockSpec((1, ctx_len, w), col(1)),
            pl.BlockSpec((1, ctx_len, w), col(2)),
            pl.BlockSpec((1, seq, w), lambda bi, g: (bi, 0, g)),
            pl.BlockSpec((1, n_lat, CHUNK, LANES), lambda bi, g: (bi, 0, 0, g)),
            pl.BlockSpec((1, 1, n_lat, 4 * hb, CHUNK), lambda bi, g: (bi, g, 0, 0, 0)),
            pl.BlockSpec((1, n_ctx, CHUNK, LANES), lambda bi, g: (bi, 0, 0, g)),
            pl.BlockSpec((1, 1, n_ctx, 4 * hb, CHUNK), lambda bi, g: (bi, g, 0, 0, 0)),
            pl.BlockSpec((CONV_K, w), cwcol(0)),
            pl.BlockSpec((CONV_K, w), cwcol(1)),
            pl.BlockSpec((CONV_K, w), cwcol(2)),
            pl.BlockSpec((1, 2, LANES), lambda bi, g: (g, 0, 0)),
            pl.BlockSpec((1, 4 * hb, 2), lambda bi, g: (g, 0, 0)),
            pl.BlockSpec((1, DN_HEAD_DIM), lambda bi, g: (0, 0)),
        ],
        out_specs=pl.BlockSpec((1, seq, w), lambda bi, g: (bi, 0, g)),
        out_shape=jax.ShapeDtypeStruct((b, seq, DN_HEADS * DN_HEAD_DIM), BF16),
        scratch_shapes=[
            pltpu.VMEM((seq + 32, w), BF16),
            pltpu.VMEM((ctx_len + seq, w), F32),
            pltpu.VMEM((ctx_len + seq, w), F32),
            pltpu.VMEM((ctx_len + seq, w), F32),
            pltpu.VMEM((n_ctx + n_lat, CHUNK, LANES), F32),
            pltpu.VMEM((n_ctx + n_lat, 4 * hb, CHUNK), F32),
            pltpu.VMEM((2 * hb, DN_HEAD_DIM, DN_HEAD_DIM), F32),
            pltpu.VMEM((seq, w), F32),
        ],
        compiler_params=pltpu.CompilerParams(
            dimension_semantics=("arbitrary", "arbitrary"), vmem_limit_bytes=VMEM_LIMIT),
        name="deltanet",
    )(qkv_x, qkv_x, qkv_x, qkv_c, qkv_c, qkv_c, z_x, gcol_x, grow_x, gcol_c, grow_c,
      conv_w, conv_w, conv_w, pcol, prow, onw)


def _na_kernel(q_ref, k_ref, v_ref, kc_ref, vc_ref, qw_ref, kw_ref, bias_ref, out_ref, kn_s, knc_s,
               *, seq, ctx_len):
    rows = seq // GRID_W
    win_rows = min(WIN_ROWS, rows)
    n_pairs = NA_HEADS // 2
    wq = NA_HEADS * NA_HEAD_DIM
    lane = lax.broadcasted_iota(jnp.int32, (1, wq), 1)
    lo_half = (lane % LANES) < NA_HEAD_DIM

    def head_rms(x, wgt):
        parts = []
        for j in range(n_pairs):
            seg = x[:, j * LANES:(j + 1) * LANES]
            sq = seg * seg
            lo = lo_half[:, j * LANES:(j + 1) * LANES]
            s_all = jnp.sum(sq, axis=-1, keepdims=True)
            s_lo = jnp.sum(jnp.where(lo, sq, 0.0), axis=-1, keepdims=True)
            ms = jnp.where(lo, s_lo, s_all - s_lo) * (1.0 / NA_HEAD_DIM)
            parts.append(seg * lax.rsqrt(ms + EPS))
        return jnp.concatenate(parts, axis=1) * wgt

    kw = kw_ref[...]
    qw = qw_ref[...]
    tile = 256

    def knorm_body(t, carry):
        r0 = pl.multiple_of(t * tile, tile)
        kn_s[pl.ds(r0, tile), :] = head_rms(k_ref[0, pl.ds(r0, tile), :].astype(F32), kw).astype(BF16)
        return carry

    lax.fori_loop(0, seq // tile, knorm_body, 0)
    knc_s[...] = head_rms(kc_ref[0].astype(F32), kw).astype(BF16)

    lane2 = lax.broadcasted_iota(jnp.int32, (GRID_W, LANES), 1)
    lo2 = lane2 < NA_HEAD_DIM

    def row_body(r, carry):
        r_start = jnp.clip(r - win_rows // 2, 0, rows - win_rows)
        case = r_start - r + (WIN_ROWS - 1)
        q0 = pl.multiple_of(r * GRID_W, GRID_W)
        k0 = pl.multiple_of(r_start * GRID_W, GRID_W)
        qn = head_rms(q_ref[0, pl.ds(q0, GRID_W), :].astype(F32), qw) * (NA_HEAD_DIM ** -0.5)
        for j in range(n_pairs):
            ls = slice(j * LANES, (j + 1) * LANES)
            q2 = qn[:, ls]
            qs = jnp.concatenate([jnp.where(lo2, q2, 0.0), jnp.where(lo2, 0.0, q2)], axis=0)
            s_w = _mm_nt(qs, kn_s[pl.ds(k0, win_rows * GRID_W), ls]) + bias_ref[case, j]
            s_c = _mm_nt(qs, knc_s[:, ls])
            mx = jnp.maximum(jnp.max(s_w, axis=-1, keepdims=True), jnp.max(s_c, axis=-1, keepdims=True))
            p_w = jnp.exp(s_w - mx)
            p_c = jnp.exp(s_c - mx)
            den = jnp.sum(p_w, axis=-1, keepdims=True) + jnp.sum(p_c, axis=-1, keepdims=True)
            o = _mm(p_w, v_ref[0, pl.ds(k0, win_rows * GRID_W), ls]) + _mm(p_c, vc_ref[0, :, ls])
            o = o * (1.0 / den)
            out_ref[0, pl.ds(q0, GRID_W), ls] = jnp.where(lo2, o[:GRID_W], o[GRID_W:]).astype(out_ref.dtype)
        return carry

    lax.fori_loop(0, rows, row_body, 0)


def _natten(na_x, na_c, qw, kw, bias):
    b, seq, _ = na_x.shape
    ctx_len = na_c.shape[1]
    wq = NA_HEADS * NA_HEAD_DIM
    rows = seq // GRID_W
    win_rows = min(WIN_ROWS, rows)
    return pl.pallas_call(
        functools.partial(_na_kernel, seq=seq, ctx_len=ctx_len),
        grid=(b,),
        in_specs=[
            pl.BlockSpec((1, seq, wq), lambda bi: (bi, 0, 0)),
            pl.BlockSpec((1, seq, wq), lambda bi: (bi, 0, 1)),
            pl.BlockSpec((1, seq, wq), lambda bi: (bi, 0, 2)),
            pl.BlockSpec((1, ctx_len, wq), lambda bi: (bi, 0, 1)),
            pl.BlockSpec((1, ctx_len, wq), lambda bi: (bi, 0, 2)),
            pl.BlockSpec((1, wq), lambda bi: (0, 0)),
            pl.BlockSpec((1, wq), lambda bi: (0, 0)),
            pl.BlockSpec(bias.shape, lambda bi: (0, 0, 0, 0)),
        ],
        out_specs=pl.BlockSpec((1, seq, wq), lambda bi: (bi, 0, 0)),
        out_shape=jax.ShapeDtypeStruct((b, seq, wq), BF16),
        scratch_shapes=[
            pltpu.VMEM((seq, wq), BF16),
            pltpu.VMEM((ctx_len, wq), BF16),
        ],
        compiler_params=pltpu.CompilerParams(
            dimension_semantics=("arbitrary",), vmem_limit_bytes=VMEM_LIMIT),
        name="natten",
    )(na_x, na_x, na_x, na_c, na_c, qw, kw, bias)


def _na_bias_table(rpb, rows):
    win_rows = min(WIN_ROWS, rows)
    cols = np.arange(GRID_W)
    win_start = np.clip(cols - WIN_COLS // 2, 0, GRID_W - WIN_COLS)
    kc = cols[None, :]
    valid = (kc >= win_start[:, None]) & (kc < win_start[:, None] + WIN_COLS)
    rel = np.clip(kc - cols[:, None] + WIN_COLS - 1, 0, 2 * WIN_COLS - 2)
    g = rpb[:, :, rel]
    n_case = WIN_ROWS
    tabs = []
    for case in range(n_case):
        dr = np.clip(case + np.arange(win_rows), 0, 2 * WIN_ROWS - 2)
        t = jnp.transpose(g[:, dr], (0, 2, 1, 3))
        t = jnp.where(valid[None, :, None, :], t, -1e30)
        tabs.append(t.reshape(NA_HEADS // 2, 2 * GRID_W, win_rows * GRID_W))
    return jnp.stack(tabs, axis=0).astype(F32)


def _outffn_kernel(x_ref, dn_ref, na_ref, mod_ref, nw_ref, wo_ref, wi_ref, wf_ref, out_ref, *, d_ff, splits):
    half = dn_ref.shape[-1]
    attn = (jnp.dot(dn_ref[0], wo_ref[0:half, :], preferred_element_type=F32)
            + jnp.dot(na_ref[0], wo_ref[half:, :], preferred_element_type=F32))
    x1 = x_ref[0] + mod_ref[0, 2:3, :] * attn
    ms = jnp.mean(x1 * x1, axis=-1, keepdims=True)
    h = (x1 * lax.rsqrt(ms + EPS)) * nw_ref[...]
    hb = (h * (1.0 + mod_ref[0, 4:5, :]) + mod_ref[0, 3:4, :]).astype(BF16)
    acc = None
    off = 0
    for n in splits:
        gate = jnp.dot(hb, wi_ref[:, off:off + n], preferred_element_type=F32)
        up = jnp.dot(hb, wi_ref[:, d_ff + off:d_ff + off + n], preferred_element_type=F32)
        part = jnp.dot((_silu(gate) * up).astype(BF16), wf_ref[off:off + n, :], preferred_element_type=F32)
        acc = part if acc is None else acc + part
        off += n
    out_ref[0] = x1 + mod_ref[0, 5:6, :] * acc


def _out_ffn(x, dn, na, mods, norm_w, w_out, w_ffn_in, w_ffn_out, *, tm):
    b, l, d = x.shape
    half = dn.shape[-1]
    d_ff = w_ffn_out.shape[0]
    assert sum(FFN_SPLITS) == d_ff
    tm = min(tm, l)
    row_map = lambda bi, i: (bi, i, 0)
    const = lambda bi, i: (0, 0)
    single = pl.Buffered(1)
    return pl.pallas_call(
        functools.partial(_outffn_kernel, d_ff=d_ff, splits=FFN_SPLITS),
        grid=(b, l // tm),
        in_specs=[
            pl.BlockSpec((1, tm, d), row_map),
            pl.BlockSpec((1, tm, half), row_map),
            pl.BlockSpec((1, tm, half), row_map),
            pl.BlockSpec((1, 6, d), lambda bi, i: (bi, 0, 0)),
            pl.BlockSpec((1, d), const),
            pl.BlockSpec(w_out.shape, const, pipeline_mode=single),
            pl.BlockSpec(w_ffn_in.shape, const, pipeline_mode=single),
            pl.BlockSpec(w_ffn_out.shape, const, pipeline_mode=single),
        ],
        out_specs=pl.BlockSpec((1, tm, d), row_map),
        out_shape=jax.ShapeDtypeStruct((b, l, d), F32),
        compiler_params=pltpu.CompilerParams(
            dimension_semantics=("arbitrary", "arbitrary"), vmem_limit_bytes=VMEM_LIMIT),
        name="out_ffn",
    )(x, dn, na, mods, norm_w, w_out, w_ffn_in, w_ffn_out)


def _gate_layouts(ab, hb):
    b, l, n = ab.shape
    ng = n // LANES
    nc = l // CHUNK
    col = ab.reshape(b, nc, CHUNK, n)
    row = ab.reshape(b, nc, CHUNK, ng, LANES)[..., :4 * hb]
    row = jnp.transpose(row, (0, 3, 1, 4, 2))
    return col, row


def kernel(x, c, ctx, c_ctx, norm1_w, norm2_w, w_ada, b_ada, w_in, dn_conv_w, dn_A_log, dn_dt_bias,
           dn_out_norm_w, na_q_norm_w, na_k_norm_w, na_rpb, w_out, w_ffn_in, w_ffn_out):
    assert w_in.shape[0] == 1, "single-layer problem"
    b, seq, d = x.shape
    hb = DN_HB
    ng = DN_HEADS // hb
    dn_w = DN_HEADS * DN_HEAD_DIM
    na_w = NA_HEADS * NA_HEAD_DIM
    rows = seq // GRID_W

    n_mod_rows = -(-(b + 1) // 8) * 8
    c_all = jnp.zeros((n_mod_rows, d), F32).at[:b].set(c).at[b].set(c_ctx)
    mod = _adaln(c_all, w_ada[0], b_ada[0][None, :])
    mod_x = mod[:b].reshape(b, 6, d)
    mod_c = mod[b:b + 1].reshape(1, 6, d)

    w0 = w_in[0]
    gate0 = 4 * dn_w
    na0 = gate0 + 4 * DN_HEADS
    gate_cols = w0[:, gate0:na0].reshape(d, 4, ng, hb)
    gate_cols = jnp.transpose(gate_cols, (0, 2, 1, 3)).reshape(d, ng, 4 * hb)
    gate_cols = jnp.pad(gate_cols, ((0, 0), (0, 0), (0, LANES - 4 * hb))).reshape(d, ng * LANES)
    w_perm = jnp.concatenate([w0[:, :gate0], w0[:, na0:], gate_cols], axis=1).astype(BF16)

    proj = functools.partial(_in_proj, norm_w=norm1_w[0][None, :], w_perm=w_perm,
                             n_dn=3 * dn_w, n_z=dn_w, n_na=3 * na_w, tm=ROW_TILE)
    qkv_x, z_x, na_x, ab_x = proj(x, mod_x, per_batch_mod=True)
    qkv_c, _, na_c, ab_c = proj(ctx, mod_c, per_batch_mod=False)

    def group_params(p):
        return jnp.transpose(p[0].reshape(2, ng, hb), (1, 0, 2)).reshape(ng, 2 * hb)

    pg = jnp.stack([group_params(dn_A_log), group_params(dn_dt_bias)], axis=1)
    pcol = jnp.pad(pg, ((0, 0), (0, 0), (0, LANES - 2 * hb)))
    prow = jnp.pad(jnp.transpose(pg, (0, 2, 1)), ((0, 0), (0, 2 * hb), (0, 0)))
    gcol_x, grow_x = _gate_layouts(ab_x, hb)
    gcol_c, grow_c = _gate_layouts(ab_c, hb)
    dn_x = _deltanet(qkv_x, qkv_c, z_x, gcol_x, grow_x, gcol_c, grow_c, dn_conv_w[0], pcol, prow,
                     dn_out_norm_w[0][None, :], hb=hb)

    bias = _na_bias_table(na_rpb[0], rows)
    qw = jnp.tile(na_q_norm_w[0], NA_HEADS)[None, :]
    kw = jnp.tile(na_k_norm_w[0], NA_HEADS)[None, :]
    na_o = _natten(na_x, na_c, qw, kw, bias)

    return _out_ffn(x, dn_x, na_o, mod_x, norm2_w[0][None, :], w_out[0].astype(BF16),
                    w_ffn_in[0].astype(BF16), w_ffn_out[0].astype(BF16), tm=ROW_TILE)
```

```python
import functools

import numpy as np
import jax
import jax.numpy as jnp
from jax import lax
from jax.experimental import pallas as pl
from jax.experimental.pallas import tpu as pltpu

F32 = jnp.float32
BF16 = jnp.bfloat16
HIGHEST = lax.Precision.HIGHEST

EPS = 1e-6
CHUNK = 64
CONV_K = 5
DN_HEAD_DIM = 128
DN_HEADS = 4
NA_HEAD_DIM = 64
NA_HEADS = 8
GRID_W = 64
WIN_ROWS = 8
WIN_COLS = 16
LANES = 128
VMEM_LIMIT = 56 * 1024 * 1024

DN_HB = 4
ROW_TILE = 512
FFN_SPLITS = (1024, 1024, 768)


def _silu(x):
    return x * (1.0 / (1.0 + jnp.exp(-x)))


def _softplus(x):
    return jnp.maximum(x, 0.0) + jnp.log(1.0 + jnp.exp(-jnp.abs(x)))


def _mm(a, b):
    return jnp.dot(a.astype(BF16), b.astype(BF16), preferred_element_type=F32)


def _mm_nt(a, b):
    return lax.dot_general(a.astype(BF16), b.astype(BF16), (((1,), (1,)), ((), ())),
                           preferred_element_type=F32)


def _mm_tn(a, b):
    return lax.dot_general(a.astype(BF16), b.astype(BF16), (((0,), (0,)), ((), ())),
                           preferred_element_type=F32)


def _mm_f32(a, b):
    return lax.dot_general(a, b, (((1,), (0,)), ((), ())), precision=HIGHEST,
                           preferred_element_type=F32)


def _adaln_kernel(c_ref, w_ref, b_ref, o_ref):
    o_ref[...] = _mm_f32(_silu(c_ref[...]), w_ref[...]) + b_ref[...]


def _adaln(c_all, w_ada, b_ada):
    rows, d = c_all.shape
    n = w_ada.shape[1]
    tn = 1024
    return pl.pallas_call(
        _adaln_kernel,
        grid=(n // tn,),
        in_specs=[
            pl.BlockSpec((rows, d), lambda j: (0, 0)),
            pl.BlockSpec((d, tn), lambda j: (0, j)),
            pl.BlockSpec((1, tn), lambda j: (0, j)),
        ],
        out_specs=pl.BlockSpec((rows, tn), lambda j: (0, j)),
        out_shape=jax.ShapeDtypeStruct((rows, n), F32),
        compiler_params=pltpu.CompilerParams(
            dimension_semantics=("arbitrary",), vmem_limit_bytes=VMEM_LIMIT),
        name="adaln",
    )(c_all, w_ada, b_ada)


def _inproj_kernel(x_ref, mod_ref, nw_ref, w_ref, oq_ref, oz_ref, on_ref, oab_ref, *, n_dn, n_z, n_na):
    x = x_ref[0]
    ms = jnp.mean(x * x, axis=-1, keepdims=True)
    h = (x * lax.rsqrt(ms + EPS)) * nw_ref[...]
    h = h * (1.0 + mod_ref[0, 1:2, :]) + mod_ref[0, 0:1, :]
    hb = h.astype(BF16)
    c0, c1, c2 = n_dn, n_dn + n_z, n_dn + n_z + n_na
    oq_ref[0] = jnp.dot(hb, w_ref[:, 0:c0], preferred_element_type=F32).astype(BF16)
    oz_ref[0] = jnp.dot(hb, w_ref[:, c0:c1], preferred_element_type=F32).astype(BF16)
    on_ref[0] = jnp.dot(hb, w_ref[:, c1:c2], preferred_element_type=F32).astype(BF16)
    oab_ref[0] = jnp.dot(hb, w_ref[:, c2:], preferred_element_type=F32)


def _in_proj(x, mods, norm_w, w_perm, *, n_dn, n_z, n_na, per_batch_mod, tm):
    b, l, d = x.shape
    n_all = w_perm.shape[1]
    n_ab = n_all - (n_dn + n_z + n_na)
    tm = min(tm, l)
    mod_map = (lambda bi, i: (bi, 0, 0)) if per_batch_mod else (lambda bi, i: (0, 0, 0))
    row_map = lambda bi, i: (bi, i, 0)
    return pl.pallas_call(
        functools.partial(_inproj_kernel, n_dn=n_dn, n_z=n_z, n_na=n_na),
        grid=(b, l // tm),
        in_specs=[
            pl.BlockSpec((1, tm, d), row_map),
            pl.BlockSpec((1, 6, d), mod_map),
            pl.BlockSpec((1, d), lambda bi, i: (0, 0)),
            pl.BlockSpec((d, n_all), lambda bi, i: (0, 0)),
        ],
        out_specs=[
            pl.BlockSpec((1, tm, n_dn), row_map),
            pl.BlockSpec((1, tm, n_z), row_map),
            pl.BlockSpec((1, tm, n_na), row_map),
            pl.BlockSpec((1, tm, n_ab), row_map),
        ],
        out_shape=[
            jax.ShapeDtypeStruct((b, l, n_dn), BF16),
            jax.ShapeDtypeStruct((b, l, n_z), BF16),
            jax.ShapeDtypeStruct((b, l, n_na), BF16),
            jax.ShapeDtypeStruct((b, l, n_ab), F32),
        ],
        compiler_params=pltpu.CompilerParams(
            dimension_semantics=("arbitrary", "arbitrary"), vmem_limit_bytes=VMEM_LIMIT),
        name="in_proj",
    )(x, mods, norm_w, w_perm)


def _dn_kernel(qx_ref, kx_ref, vx_ref, qc_ref, kc_ref, vc_ref, z_ref,
               gcx_ref, grx_ref, gcc_ref, grc_ref,
               cwq_ref, cwk_ref, cwv_ref, pc_ref, pr_ref, onw_ref,
               out_ref,
               q_s, k_s, v_s, gc_s, gr_s, s_s, o_s, *, hb, seq, ctx_len):
    n_ctx = ctx_len // CHUNK
    n_lat = seq // CHUNK
    tile = 128
    halo = 16
    win = tile + 2 * halo
    side_taps = [j for j in range(CONV_K) if j != CONV_K // 2]

    sr = lax.broadcasted_iota(jnp.int32, (len(side_taps) * tile, win), 0)
    sc = lax.broadcasted_iota(jnp.int32, (len(side_taps) * tile, win), 1)
    tap_off = jnp.zeros_like(sr)
    for n, j in enumerate(side_taps):
        tap_off = jnp.where(sr // tile == n, j - CONV_K // 2, tap_off)

    def shift_matrix(delta):
        return jnp.where(sc == (sr % tile) + tap_off + delta, 1.0, 0.0).astype(BF16)

    def conv_into(src_ref, n_rows, cw_ref, dst_ref, dst_off, mode):
        cw = cw_ref[...]
        n_tiles = n_rows // tile

        def one_tile(r0, a0, sh):
            z = jnp.dot(sh, src_ref[0, pl.ds(a0, win), :], preferred_element_type=F32)
            acc = cw[CONV_K // 2:CONV_K // 2 + 1, :] * src_ref[0, pl.ds(r0, tile), :].astype(F32)
            for n, j in enumerate(side_taps):
                acc = acc + cw[j:j + 1, :] * z[n * tile:(n + 1) * tile, :]
            y = _silu(acc)
            if mode != "v":
                parts = []
                for hh in range(hb):
                    seg = y[:, hh * DN_HEAD_DIM:(hh + 1) * DN_HEAD_DIM]
                    inv = lax.rsqrt(jnp.sum(seg * seg, axis=-1, keepdims=True) + EPS)
                    if mode == "q":
                        inv = inv * (DN_HEAD_DIM ** -0.5)
                    parts.append(seg * inv)
                y = parts[0] if hb == 1 else jnp.concatenate(parts, axis=1)
            dst_ref[pl.ds(dst_off + r0, tile), :] = y

        one_tile(0, 0, shift_matrix(0))
        if n_tiles > 2:
            sh_mid = shift_matrix(halo)

            def body(t, carry):
                r0 = pl.multiple_of(t * tile, tile)
                one_tile(r0, pl.multiple_of(r0 - halo, halo), sh_mid)
                return carry

            lax.fori_loop(1, n_tiles - 1, body, 0)
        one_tile(n_rows - tile, n_rows - win, shift_matrix(2 * halo))

    conv_into(qc_ref, ctx_len, cwq_ref, q_s, 0, "q")
    conv_into(kc_ref, ctx_len, cwk_ref, k_s, 0, "k")
    conv_into(vc_ref, ctx_len, cwv_ref, v_s, 0, "v")
    conv_into(qx_ref, seq, cwq_ref, q_s, ctx_len, "q")
    conv_into(kx_ref, seq, cwk_ref, k_s, ctx_len, "k")
    conv_into(vx_ref, seq, cwv_ref, v_s, ctx_len, "v")

    ri = lax.broadcasted_iota(jnp.int32, (CHUNK, CHUNK), 0)
    ci = lax.broadcasted_iota(jnp.int32, (CHUNK, CHUNK), 1)
    lower = jnp.where(ci <= ri, 1.0, 0.0).astype(BF16)
    upper = jnp.where(ci >= ri, 1.0, 0.0).astype(BF16)
    tri_c = jnp.concatenate([lower, upper], axis=1)
    tri_r = jnp.concatenate([upper, lower], axis=0)
    a_log_c, dt_c = pc_ref[0, 0:1, :], pc_ref[0, 1:2, :]
    a_log_r, dt_r = pr_ref[0, :, 0:1], pr_ref[0, :, 1:2]
    lane = lax.broadcasted_iota(jnp.int32, (CHUNK, LANES), 1)
    sub = lax.broadcasted_iota(jnp.int32, (4 * hb, CHUNK), 0)

    def split3(x):
        x1 = x.astype(BF16)
        r1 = x - x1.astype(F32)
        x2 = r1.astype(BF16)
        x3 = (r1 - x2.astype(F32)).astype(BF16)
        return x1, x2, x3

    def gates_into(gc_ref, gr_ref, n_chunks, off):
        def body(t, carry):
            xc = gc_ref[0, t]
            g = -jnp.exp(a_log_c) * _softplus(xc + dt_c)
            gs = jnp.concatenate([jnp.where(lane < hb, g, 0.0),
                                  jnp.where((lane >= hb) & (lane < 2 * hb), g, 0.0)], axis=0)
            gam = sum(jnp.dot(tri_c, p, preferred_element_type=F32) for p in split3(gs))
            gc_s[off + t] = jnp.where(lane < 2 * hb, gam, 1.0 / (1.0 + jnp.exp(-xc)))
            xr = gr_ref[0, 0, t]
            g = -jnp.exp(a_log_r) * _softplus(xr + dt_r)
            gs = jnp.concatenate([jnp.where(sub < hb, g, 0.0),
                                  jnp.where((sub >= hb) & (sub < 2 * hb), g, 0.0)], axis=1)
            gam = sum(jnp.dot(p, tri_r, preferred_element_type=F32) for p in split3(gs))
            gr_s[off + t] = jnp.where(sub < 2 * hb, gam, 1.0 / (1.0 + jnp.exp(-xr)))
            return carry

        lax.fori_loop(0, n_chunks, body, 0)

    gates_into(gcc_ref, grc_ref, n_ctx, 0)
    gates_into(gcx_ref, grx_ref, n_lat, n_ctx)

    eye64 = (ri == ci).astype(F32)
    r128 = lax.broadcasted_iota(jnp.int32, (DN_HEAD_DIM, DN_HEAD_DIM), 0)
    c128 = lax.broadcasted_iota(jnp.int32, (DN_HEAD_DIM, DN_HEAD_DIM), 1)
    eye128 = (r128 == c128).astype(F32)
    onw = onw_ref[...]

    def chunk_steps(groups, out_mode):
        chains = [c for grp in groups for c in grp]
        rng = range(len(chains))
        cs = [d * hb + hh for d, hh, _ in chains]
        r0s = [pl.multiple_of(chunk * CHUNK, CHUNK) for _, _, chunk in chains]
        hss = [slice(hh * DN_HEAD_DIM, (hh + 1) * DN_HEAD_DIM) for _, hh, _ in chains]
        q = [q_s[pl.ds(r0s[i], CHUNK), hss[i]] for i in rng]
        k = [k_s[pl.ds(r0s[i], CHUNK), hss[i]] for i in rng]
        v = [v_s[pl.ds(r0s[i], CHUNK), hss[i]] for i in rng]
        gcol = [gc_s[chains[i][2]] for i in rng]
        grow = [gr_s[chains[i][2]] for i in rng]
        gam_c = [gcol[i][:, cs[i]:cs[i] + 1] for i in rng]
        beta_c = [gcol[i][:, 2 * hb + cs[i]:2 * hb + cs[i] + 1] for i in rng]
        gam_r = [grow[i][cs[i]:cs[i] + 1, :] for i in rng]
        fwd = [chains[i][0] == 0 for i in rng]
        g_tot = [gam_c[i][CHUNK - 1:CHUNK, :] if fwd[i] else gam_c[i][0:1, :] for i in rng]
        decay = [jnp.exp(jnp.where((ci <= ri) if fwd[i] else (ci >= ri), gam_c[i] - gam_r[i], -jnp.inf))
                 for i in rng]
        qk_kk = [_mm_nt(jnp.concatenate([q[i], k[i]], axis=0), k[i]) for i in rng]
        qkd = [qk_kk[i][:CHUNK] * decay[i] for i in rng]
        m = [jnp.where((ci < ri) if fwd[i] else (ci > ri), qk_kk[i][CHUNK:] * beta_c[i] * decay[i], 0.0)
             for i in rng]
        t_inv = [eye64 - m[i] for i in rng]
        pw = m
        for _ in range(5):
            pw = [_mm(pw[i], pw[i]) for i in rng]
            t_inv = [t_inv[i] + _mm(t_inv[i], pw[i]) for i in rng]
        e_gam = [jnp.exp(gam_c[i]) for i in rng]
        uw = [_mm(t_inv[i], jnp.concatenate([v[i] * beta_c[i], k[i] * (beta_c[i] * e_gam[i])], axis=1))
              for i in rng]
        qkw = [_mm_tn(k[i] * jnp.exp(g_tot[i] - gam_c[i]), uw[i]) for i in rng]
        p_mat = [jnp.exp(g_tot[i]) * eye128 - qkw[i][:, DN_HEAD_DIM:] for i in rng]
        if out_mode is not None:
            ow = [_mm(qkd[i], uw[i]) for i in rng]
            r_mat = [q[i] * e_gam[i] - ow[i][:, DN_HEAD_DIM:] for i in rng]

        state = {}
        pos = 0
        for grp in groups:
            idx = range(pos, pos + len(grp))
            pos += len(grp)
            s_old = [state[cs[i]] if cs[i] in state else s_s[cs[i]] for i in idx]
            s_new = [_mm(p_mat[i], s_old[n]) + qkw[i][:, :DN_HEAD_DIM] for n, i in enumerate(idx)]
            for n, i in enumerate(idx):
                state[cs[i]] = s_new[n]
            if out_mode is None:
                continue
            o = [_mm(r_mat[i], s_old[n]) + ow[i][:, :DN_HEAD_DIM] for n, i in enumerate(idx)]
            for n, i in enumerate(idx):
                rl = pl.multiple_of(r0s[i] - ctx_len, CHUNK)
                if out_mode == "store":
                    o_s[pl.ds(rl, CHUNK), hss[i]] = o[n]
                else:
                    tot = o[n] + o_s[pl.ds(rl, CHUNK), hss[i]]
                    y = tot * lax.rsqrt(jnp.mean(tot * tot, axis=-1, keepdims=True) + EPS) * onw
                    zz = z_ref[0, pl.ds(rl, CHUNK), hss[i]].astype(F32)
                    out_ref[0, pl.ds(rl, CHUNK), hss[i]] = (y * _silu(zz)).astype(out_ref.dtype)
        for c, val in state.items():
            s_s[c] = val

    s_s[...] = jnp.zeros(s_s.shape, F32)
    heads = range(hb)

    def groups_at(first, last, i):
        return [(0, hh, first + i) for hh in heads] + [(1, hh, last - i) for hh in heads]

    def ctx_body(j, carry):
        chunk_steps([groups_at(0, n_ctx - 1, 2 * j), groups_at(0, n_ctx - 1, 2 * j + 1)], None)
        return carry

    lax.fori_loop(0, n_ctx // 2, ctx_body, 0)

    def lat_body(mode, j, carry):
        lo, hi = n_ctx, n_ctx + n_lat - 1
        chunk_steps([groups_at(lo, hi, 2 * j), groups_at(lo, hi, 2 * j + 1)], mode)
        return carry

    lax.fori_loop(0, n_lat // 4, functools.partial(lat_body, "store"), 0)
    lax.fori_loop(n_lat // 4, n_lat // 2, functools.partial(lat_body, "final"), 0)


def _deltanet(qkv_x, qkv_c, z_x, gcol_x, grow_x, gcol_c, grow_c, conv_w, pcol, prow, onw, *, hb):
    b, seq, _ = qkv_x.shape
    ctx_len = qkv_c.shape[1]
    ng = DN_HEADS // hb
    w = hb * DN_HEAD_DIM
    n_lat, n_ctx = seq // CHUNK, ctx_len // CHUNK
    assert n_lat % 4 == 0 and n_ctx % 2 == 0 and seq >= 256 and ctx_len >= 256

    def col(part):
        return lambda bi, g: (bi, 0, part * ng + g)

    def cwcol(part):
        return lambda bi, g: (0, part * ng + g)

    return pl.pallas_call(
        functools.partial(_dn_kernel, hb=hb, seq=seq, ctx_len=ctx_len),
        grid=(b, ng),
        in_specs=[
            pl.BlockSpec((1, seq, w), col(0)),
            pl.BlockSpec((1, seq, w), col(1)),
            pl.BlockSpec((1, seq, w), col(2)),
            pl.BlockSpec((1, ctx_len, w), col(0)),
            pl.BlockSpec((1, ctx_len, w), col(1)),
            pl.BlockSpec((1, ctx_len, w), col(2)),
            pl.BlockSpec((1, seq, w), lambda bi, g: (bi, 0, g)),
            pl.BlockSpec((1, n_lat, CHUNK, LANES), lambda bi, g: (bi, 0, 0, g)),
            pl.BlockSpec((1, 1, n_lat, 4 * hb, CHUNK), lambda bi, g: (bi, g, 0, 0, 0)),
            pl.BlockSpec((1, n_ctx, CHUNK, LANES), lambda bi, g: (bi, 0, 0, g)),
            pl.BlockSpec((1, 1, n_ctx, 4 * hb, CHUNK), lambda bi, g: (bi, g, 0, 0, 0)),
            pl.BlockSpec((CONV_K, w), cwcol(0)),
            pl.BlockSpec((CONV_K, w), cwcol(1)),
            pl.BlockSpec((CONV_K, w), cwcol(2)),
            pl.BlockSpec((1, 2, LANES), lambda bi, g: (g, 0, 0)),
            pl.BlockSpec((1, 4 * hb, 2), lambda bi, g: (g, 0, 0)),
            pl.BlockSpec((1, DN_HEAD_DIM), lambda bi, g: (0, 0)),
        ],
        out_specs=pl.BlockSpec((1, seq, w), lambda bi, g: (bi, 0, g)),
        out_shape=jax.ShapeDtypeStruct((b, seq, DN_HEADS * DN_HEAD_DIM), BF16),
        scratch_shapes=[
            pltpu.VMEM((ctx_len + seq, w), F32),
            pltpu.VMEM((ctx_len + seq, w), F32),
            pltpu.VMEM((ctx_len + seq, w), F32),
            pltpu.VMEM((n_ctx + n_lat, CHUNK, LANES), F32),
            pltpu.VMEM((n_ctx + n_lat, 4 * hb, CHUNK), F32),
            pltpu.VMEM((2 * hb, DN_HEAD_DIM, DN_HEAD_DIM), F32),
            pltpu.VMEM((seq, w), F32),
        ],
        compiler_params=pltpu.CompilerParams(
            dimension_semantics=("arbitrary", "arbitrary"), vmem_limit_bytes=VMEM_LIMIT),
        name="deltanet",
    )(qkv_x, qkv_x, qkv_x, qkv_c, qkv_c, qkv_c, z_x, gcol_x, grow_x, gcol_c, grow_c,
      conv_w, conv_w, conv_w, pcol, prow, onw)


def _na_kernel(q_ref, k_ref, v_ref, kc_ref, vc_ref, qw_ref, kw_ref, bias_ref, out_ref, kn_s, knc_s,
               *, seq, ctx_len):
    rows = seq // GRID_W
    win_rows = min(WIN_ROWS, rows)
    n_pairs = NA_HEADS // 2
    wq = NA_HEADS * NA_HEAD_DIM
    lane = lax.broadcasted_iota(jnp.int32, (1, wq), 1)
    lo_half = (lane % LANES) < NA_HEAD_DIM

    def head_rms(x, wgt):
        parts = []
        for j in range(n_pairs):
            seg = x[:, j * LANES:(j + 1) * LANES]
            sq = seg * seg
            lo = lo_half[:, j * LANES:(j + 1) * LANES]
            s_all = jnp.sum(sq, axis=-1, keepdims=True)
            s_lo = jnp.sum(jnp.where(lo, sq, 0.0), axis=-1, keepdims=True)
            ms = jnp.where(lo, s_lo, s_all - s_lo) * (1.0 / NA_HEAD_DIM)
            parts.append(seg * lax.rsqrt(ms + EPS))
        return jnp.concatenate(parts, axis=1) * wgt

    kw = kw_ref[...]
    qw = qw_ref[...]
    tile = 256

    def knorm_body(t, carry):
        r0 = pl.multiple_of(t * tile, tile)
        kn_s[pl.ds(r0, tile), :] = head_rms(k_ref[0, pl.ds(r0, tile), :].astype(F32), kw).astype(BF16)
        return carry

    lax.fori_loop(0, seq // tile, knorm_body, 0)
    knc_s[...] = head_rms(kc_ref[0].astype(F32), kw).astype(BF16)

    lane2 = lax.broadcasted_iota(jnp.int32, (GRID_W, LANES), 1)
    lo2 = lane2 < NA_HEAD_DIM

    def row_body(r, carry):
        r_start = jnp.clip(r - win_rows // 2, 0, rows - win_rows)
        case = r_start - r + (WIN_ROWS - 1)
        q0 = pl.multiple_of(r * GRID_W, GRID_W)
        k0 = pl.multiple_of(r_start * GRID_W, GRID_W)
        n_win = win_rows * GRID_W
        qn = head_rms(q_ref[0, pl.ds(q0, GRID_W), :].astype(F32), qw) * (NA_HEAD_DIM ** -0.5)
        pairs = range(n_pairs)
        lss = [slice(j * LANES, (j + 1) * LANES) for j in pairs]
        qs = [jnp.concatenate([jnp.where(lo2, qn[:, ls], 0.0), jnp.where(lo2, 0.0, qn[:, ls])], axis=0)
              for ls in lss]
        s_w = [_mm_nt(qs[j], kn_s[pl.ds(k0, n_win), lss[j]]) + bias_ref[case, j] for j in pairs]
        s_c = [_mm_nt(qs[j], knc_s[:, lss[j]]) for j in pairs]
        mx = [jnp.maximum(jnp.max(s_w[j], axis=-1, keepdims=True), jnp.max(s_c[j], axis=-1, keepdims=True))
              for j in pairs]
        p_w = [jnp.exp(s_w[j] - mx[j]) for j in pairs]
        p_c = [jnp.exp(s_c[j] - mx[j]) for j in pairs]
        den = [jnp.sum(p_w[j], axis=-1, keepdims=True) + jnp.sum(p_c[j], axis=-1, keepdims=True) for j in pairs]
        o = [_mm(p_w[j], v_ref[0, pl.ds(k0, n_win), lss[j]]) + _mm(p_c[j], vc_ref[0, :, lss[j]]) for j in pairs]
        for j in pairs:
            on = o[j] * (1.0 / den[j])
            out_ref[0, pl.ds(q0, GRID_W), lss[j]] = jnp.where(lo2, on[:GRID_W], on[GRID_W:]).astype(out_ref.dtype)
        return carry

    lax.fori_loop(0, rows, row_body, 0)


def _natten(na_x, na_c, qw, kw, bias):
    b, seq, _ = na_x.shape
    ctx_len = na_c.shape[1]
    wq = NA_HEADS * NA_HEAD_DIM
    rows = seq // GRID_W
    win_rows = min(WIN_ROWS, rows)
    return pl.pallas_call(
        functools.partial(_na_kernel, seq=seq, ctx_len=ctx_len),
        grid=(b,),
        in_specs=[
            pl.BlockSpec((1, seq, wq), lambda bi: (bi, 0, 0)),
            pl.BlockSpec((1, seq, wq), lambda bi: (bi, 0, 1)),
            pl.BlockSpec((1, seq, wq), lambda bi: (bi, 0, 2)),
            pl.BlockSpec((1, ctx_len, wq), lambda bi: (bi, 0, 1)),
            pl.BlockSpec((1, ctx_len, wq), lambda bi: (bi, 0, 2)),
            pl.BlockSpec((1, wq), lambda bi: (0, 0)),
            pl.BlockSpec((1, wq), lambda bi: (0, 0)),
            pl.BlockSpec(bias.shape, lambda bi: (0, 0, 0, 0)),
        ],
        out_specs=pl.BlockSpec((1, seq, wq), lambda bi: (bi, 0, 0)),
        out_shape=jax.ShapeDtypeStruct((b, seq, wq), BF16),
        scratch_shapes=[
            pltpu.VMEM((seq, wq), BF16),
            pltpu.VMEM((ctx_len, wq), BF16),
        ],
        compiler_params=pltpu.CompilerParams(
            dimension_semantics=("arbitrary",), vmem_limit_bytes=VMEM_LIMIT),
        name="natten",
    )(na_x, na_x, na_x, na_c, na_c, qw, kw, bias)


def _na_bias_table(rpb, rows):
    win_rows = min(WIN_ROWS, rows)
    cols = np.arange(GRID_W)
    win_start = np.clip(cols - WIN_COLS // 2, 0, GRID_W - WIN_COLS)
    kc = cols[None, :]
    valid = (kc >= win_start[:, None]) & (kc < win_start[:, None] + WIN_COLS)
    rel = np.clip(kc - cols[:, None] + WIN_COLS - 1, 0, 2 * WIN_COLS - 2)
    g = rpb[:, :, rel]
    n_case = WIN_ROWS
    tabs = []
    for case in range(n_case):
        dr = np.clip(case + np.arange(win_rows), 0, 2 * WIN_ROWS - 2)
        t = jnp.transpose(g[:, dr], (0, 2, 1, 3))
        t = jnp.where(valid[None, :, None, :], t, -1e30)
        tabs.append(t.reshape(NA_HEADS // 2, 2 * GRID_W, win_rows * GRID_W))
    return jnp.stack(tabs, axis=0).astype(F32)


def _outffn_kernel(x_ref, dn_ref, na_ref, mod_ref, nw_ref, wo_ref, wi_ref, wf_ref, out_ref, *, d_ff, splits):
    half = dn_ref.shape[-1]
    attn = (jnp.dot(dn_ref[0], wo_ref[0:half, :], preferred_element_type=F32)
            + jnp.dot(na_ref[0], wo_ref[half:, :], preferred_element_type=F32))
    x1 = x_ref[0] + mod_ref[0, 2:3, :] * attn
    ms = jnp.mean(x1 * x1, axis=-1, keepdims=True)
    h = (x1 * lax.rsqrt(ms + EPS)) * nw_ref[...]
    hb = (h * (1.0 + mod_ref[0, 4:5, :]) + mod_ref[0, 3:4, :]).astype(BF16)
    acc = None
    off = 0
    for n in splits:
        gate = jnp.dot(hb, wi_ref[:, off:off + n], preferred_element_type=F32)
        up = jnp.dot(hb, wi_ref[:, d_ff + off:d_ff + off + n], preferred_element_type=F32)
        part = jnp.dot((_silu(gate) * up).astype(BF16), wf_ref[off:off + n, :], preferred_element_type=F32)
        acc = part if acc is None else acc + part
        off += n
    out_ref[0] = x1 + mod_ref[0, 5:6, :] * acc


def _out_ffn(x, dn, na, mods, norm_w, w_out, w_ffn_in, w_ffn_out, *, tm):
    b, l, d = x.shape
    half = dn.shape[-1]
    d_ff = w_ffn_out.shape[0]
    assert sum(FFN_SPLITS) == d_ff
    tm = min(tm, l)
    row_map = lambda bi, i: (bi, i, 0)
    const = lambda bi, i: (0, 0)
    single = pl.Buffered(1)
    return pl.pallas_call(
        functools.partial(_outffn_kernel, d_ff=d_ff, splits=FFN_SPLITS),
        grid=(b, l // tm),
        in_specs=[
            pl.BlockSpec((1, tm, d), row_map),
            pl.BlockSpec((1, tm, half), row_map),
            pl.BlockSpec((1, tm, half), row_map),
            pl.BlockSpec((1, 6, d), lambda bi, i: (bi, 0, 0)),
            pl.BlockSpec((1, d), const),
            pl.BlockSpec(w_out.shape, const, pipeline_mode=single),
            pl.BlockSpec(w_ffn_in.shape, const, pipeline_mode=single),
            pl.BlockSpec(w_ffn_out.shape, const, pipeline_mode=single),
        ],
        out_specs=pl.BlockSpec((1, tm, d), row_map),
        out_shape=jax.ShapeDtypeStruct((b, l, d), F32),
        compiler_params=pltpu.CompilerParams(
            dimension_semantics=("arbitrary", "arbitrary"), vmem_limit_bytes=VMEM_LIMIT),
        name="out_ffn",
    )(x, dn, na, mods, norm_w, w_out, w_ffn_in, w_ffn_out)


def _gate_layouts(ab, hb):
    b, l, n = ab.shape
    ng = n // LANES
    nc = l // CHUNK
    col = ab.reshape(b, nc, CHUNK, n)
    row = ab.reshape(b, nc, CHUNK, ng, LANES)[..., :4 * hb]
    row = jnp.transpose(row, (0, 3, 1, 4, 2))
    return col, row


def kernel(x, c, ctx, c_ctx, norm1_w, norm2_w, w_ada, b_ada, w_in, dn_conv_w, dn_A_log, dn_dt_bias,
           dn_out_norm_w, na_q_norm_w, na_k_norm_w, na_rpb, w_out, w_ffn_in, w_ffn_out):
    assert w_in.shape[0] == 1, "single-layer problem"
    b, seq, d = x.shape
    hb = DN_HB
    ng = DN_HEADS // hb
    dn_w = DN_HEADS * DN_HEAD_DIM
    na_w = NA_HEADS * NA_HEAD_DIM
    rows = seq // GRID_W

    n_mod_rows = -(-(b + 1) // 8) * 8
    c_all = jnp.zeros((n_mod_rows, d), F32).at[:b].set(c).at[b].set(c_ctx)
    mod = _adaln(c_all, w_ada[0], b_ada[0][None, :])
    mod_x = mod[:b].reshape(b, 6, d)
    mod_c = mod[b:b + 1].reshape(1, 6, d)

    w0 = w_in[0]
    gate0 = 4 * dn_w
    na0 = gate0 + 4 * DN_HEADS
    gate_cols = w0[:, gate0:na0].reshape(d, 4, ng, hb)
    gate_cols = jnp.transpose(gate_cols, (0, 2, 1, 3)).reshape(d, ng, 4 * hb)
    gate_cols = jnp.pad(gate_cols, ((0, 0), (0, 0), (0, LANES - 4 * hb))).reshape(d, ng * LANES)
    w_perm = jnp.concatenate([w0[:, :gate0], w0[:, na0:], gate_cols], axis=1).astype(BF16)

    proj = functools.partial(_in_proj, norm_w=norm1_w[0][None, :], w_perm=w_perm,
                             n_dn=3 * dn_w, n_z=dn_w, n_na=3 * na_w, tm=ROW_TILE)
    qkv_x, z_x, na_x, ab_x = proj(x, mod_x, per_batch_mod=True)
    qkv_c, _, na_c, ab_c = proj(ctx, mod_c, per_batch_mod=False)

    def group_params(p):
        return jnp.transpose(p[0].reshape(2, ng, hb), (1, 0, 2)).reshape(ng, 2 * hb)

    pg = jnp.stack([group_params(dn_A_log), group_params(dn_dt_bias)], axis=1)
    pcol = jnp.pad(pg, ((0, 0), (0, 0), (0, LANES - 2 * hb)))
    prow = jnp.pad(jnp.transpose(pg, (0, 2, 1)), ((0, 0), (0, 2 * hb), (0, 0)))
    gcol_x, grow_x = _gate_layouts(ab_x, hb)
    gcol_c, grow_c = _gate_layouts(ab_c, hb)
    dn_x = _deltanet(qkv_x, qkv_c, z_x, gcol_x, grow_x, gcol_c, grow_c, dn_conv_w[0], pcol, prow,
                     dn_out_norm_w[0][None, :], hb=hb)

    bias = _na_bias_table(na_rpb[0], rows)
    qw = jnp.tile(na_q_norm_w[0], NA_HEADS)[None, :]
    kw = jnp.tile(na_k_norm_w[0], NA_HEADS)[None, :]
    na_o = _natten(na_x, na_c, qw, kw, bias)

    return _out_ffn(x, dn_x, na_o, mod_x, norm2_w[0][None, :], w_out[0].astype(BF16),
                    w_ffn_in[0].astype(BF16), w_ffn_out[0].astype(BF16), tm=ROW_TILE)
```

```python
import functools

import numpy as np
import jax
import jax.numpy as jnp
from jax import lax
from jax.experimental import pallas as pl
from jax.experimental.pallas import tpu as pltpu

F32 = jnp.float32
BF16 = jnp.bfloat16
HIGHEST = lax.Precision.HIGHEST

EPS = 1e-6
LOG2E = 1.4426950408889634
CHUNK = 64
CONV_K = 5
DN_HEAD_DIM = 128
DN_HEADS = 4
NA_HEAD_DIM = 64
NA_HEADS = 8
GRID_W = 64
WIN_ROWS = 8
WIN_COLS = 16
LANES = 128
VMEM_LIMIT = 56 * 1024 * 1024

DN_HB = 4
CONV_TILE = 128
CONV_HALO = 16
GATE_UNROLL = 4
NA_ROW_UNROLL = 2
ROW_TILE = 512
FFN_SPLITS = (1024, 1024, 768)


def _silu(x):
    return x * (1.0 / (1.0 + jnp.exp(-x)))


def _softplus(x):
    return jnp.maximum(x, 0.0) + jnp.log(1.0 + jnp.exp(-jnp.abs(x)))


def _mm(a, b):
    return jnp.dot(a.astype(BF16), b.astype(BF16), preferred_element_type=F32)


def _mm_nt(a, b):
    return lax.dot_general(a.astype(BF16), b.astype(BF16), (((1,), (1,)), ((), ())),
                           preferred_element_type=F32)


def _mm_tn(a, b):
    return lax.dot_general(a.astype(BF16), b.astype(BF16), (((0,), (0,)), ((), ())),
                           preferred_element_type=F32)


def _mm_f32(a, b):
    return lax.dot_general(a, b, (((1,), (0,)), ((), ())), precision=HIGHEST,
                           preferred_element_type=F32)


def _adaln_kernel(c_ref, w_ref, b_ref, o_ref):
    o_ref[...] = _mm_f32(_silu(c_ref[...]), w_ref[...]) + b_ref[...]


def _adaln(c_all, w_ada, b_ada):
    rows, d = c_all.shape
    n = w_ada.shape[1]
    tn = 1024
    return pl.pallas_call(
        _adaln_kernel,
        grid=(n // tn,),
        in_specs=[
            pl.BlockSpec((rows, d), lambda j: (0, 0)),
            pl.BlockSpec((d, tn), lambda j: (0, j)),
            pl.BlockSpec((1, tn), lambda j: (0, j)),
        ],
        out_specs=pl.BlockSpec((rows, tn), lambda j: (0, j)),
        out_shape=jax.ShapeDtypeStruct((rows, n), F32),
        compiler_params=pltpu.CompilerParams(
            dimension_semantics=("arbitrary",), vmem_limit_bytes=VMEM_LIMIT),
        name="adaln",
    )(c_all, w_ada, b_ada)


def _inproj_kernel(x_ref, mod_ref, nw_ref, w_ref, oq_ref, oz_ref, on_ref, oab_ref, *, n_dn, n_z, n_na):
    x = x_ref[0]
    ms = jnp.mean(x * x, axis=-1, keepdims=True)
    h = (x * lax.rsqrt(ms + EPS)) * nw_ref[...]
    h = h * (1.0 + mod_ref[0, 1:2, :]) + mod_ref[0, 0:1, :]
    hb = h.astype(BF16)
    c0, c1, c2 = n_dn, n_dn + n_z, n_dn + n_z + n_na
    oq_ref[0] = jnp.dot(hb, w_ref[:, 0:c0], preferred_element_type=F32).astype(BF16)
    oz_ref[0] = jnp.dot(hb, w_ref[:, c0:c1], preferred_element_type=F32).astype(BF16)
    on_ref[0] = jnp.dot(hb, w_ref[:, c1:c2], preferred_element_type=F32).astype(BF16)
    oab_ref[0] = jnp.dot(hb, w_ref[:, c2:], preferred_element_type=F32)


def _in_proj(x, mods, norm_w, w_perm, *, n_dn, n_z, n_na, per_batch_mod, tm):
    b, l, d = x.shape
    n_all = w_perm.shape[1]
    n_ab = n_all - (n_dn + n_z + n_na)
    tm = min(tm, l)
    mod_map = (lambda bi, i: (bi, 0, 0)) if per_batch_mod else (lambda bi, i: (0, 0, 0))
    row_map = lambda bi, i: (bi, i, 0)
    return pl.pallas_call(
        functools.partial(_inproj_kernel, n_dn=n_dn, n_z=n_z, n_na=n_na),
        grid=(b, l // tm),
        in_specs=[
            pl.BlockSpec((1, tm, d), row_map),
            pl.BlockSpec((1, 6, d), mod_map),
            pl.BlockSpec((1, d), lambda bi, i: (0, 0)),
            pl.BlockSpec((d, n_all), lambda bi, i: (0, 0)),
        ],
        out_specs=[
            pl.BlockSpec((1, tm, n_dn), row_map),
            pl.BlockSpec((1, tm, n_z), row_map),
            pl.BlockSpec((1, tm, n_na), row_map),
            pl.BlockSpec((1, tm, n_ab), row_map),
        ],
        out_shape=[
            jax.ShapeDtypeStruct((b, l, n_dn), BF16),
            jax.ShapeDtypeStruct((b, l, n_z), BF16),
            jax.ShapeDtypeStruct((b, l, n_na), BF16),
            jax.ShapeDtypeStruct((b, l, n_ab), F32),
        ],
        compiler_params=pltpu.CompilerParams(
            dimension_semantics=("arbitrary", "arbitrary"), vmem_limit_bytes=VMEM_LIMIT),
        name="in_proj",
    )(x, mods, norm_w, w_perm)


def _dn_kernel(qx_ref, kx_ref, vx_ref, qc_ref, kc_ref, vc_ref, z_ref,
               gcx_ref, grx_ref, gcc_ref, grc_ref,
               cwq_ref, cwk_ref, cwv_ref, pc_ref, pr_ref, onw_ref, sh_ref,
               out_ref,
               q_s, k_s, v_s, gc_s, gr_s, s_s, o_s, *, hb, seq, ctx_len):
    n_ctx = ctx_len // CHUNK
    n_lat = seq // CHUNK
    tile = CONV_TILE
    win = tile + 2 * CONV_HALO
    side_taps = [j for j in range(CONV_K) if j != CONV_K // 2]

    streams = ((cwq_ref, q_s, "q"), (cwk_ref, k_s, "k"), (cwv_ref, v_s, "v"))

    def conv_tiles(srcs, dst_off, r0, a0, sh):
        zs = [jnp.dot(sh, src[0, pl.ds(a0, win), :], preferred_element_type=F32) for src in srcs]
        ys = []
        for src, z, (cw_ref, _, _) in zip(srcs, zs, streams):
            cw = cw_ref[...]
            acc = cw[CONV_K // 2:CONV_K // 2 + 1, :] * src[0, pl.ds(r0, tile), :].astype(F32)
            for n, j in enumerate(side_taps):
                acc = acc + cw[j:j + 1, :] * z[n * tile:(n + 1) * tile, :]
            ys.append(_silu(acc))
        for y, (_, dst_ref, mode) in zip(ys, streams):
            if mode != "v":
                parts = []
                for hh in range(hb):
                    seg = y[:, hh * DN_HEAD_DIM:(hh + 1) * DN_HEAD_DIM]
                    inv = lax.rsqrt(jnp.sum(seg * seg, axis=-1, keepdims=True) + EPS)
                    if mode == "q":
                        inv = inv * (DN_HEAD_DIM ** -0.5)
                    parts.append(seg * inv)
                y = parts[0] if hb == 1 else jnp.concatenate(parts, axis=1)
            dst_ref[pl.ds(dst_off + r0, tile), :] = y

    def conv_into(srcs, n_rows, dst_off):
        n_tiles = n_rows // tile
        conv_tiles(srcs, dst_off, 0, 0, sh_ref[0])
        if n_tiles > 2:
            def body(t, carry):
                r0 = pl.multiple_of(t * tile, tile)
                conv_tiles(srcs, dst_off, r0, pl.multiple_of(r0 - CONV_HALO, CONV_HALO), sh_ref[1])
                return carry

            lax.fori_loop(1, n_tiles - 1, body, 0)
        conv_tiles(srcs, dst_off, n_rows - tile, n_rows - win, sh_ref[2])

    conv_into((qc_ref, kc_ref, vc_ref), ctx_len, 0)
    conv_into((qx_ref, kx_ref, vx_ref), seq, ctx_len)

    ri = lax.broadcasted_iota(jnp.int32, (CHUNK, CHUNK), 0)
    ci = lax.broadcasted_iota(jnp.int32, (CHUNK, CHUNK), 1)
    lower = jnp.where(ci <= ri, 1.0, 0.0).astype(BF16)
    upper = jnp.where(ci >= ri, 1.0, 0.0).astype(BF16)
    tri_c = jnp.concatenate([lower, upper], axis=1)
    tri_r = jnp.concatenate([upper, lower], axis=0)
    a_log_c, dt_c = pc_ref[0, 0:1, :], pc_ref[0, 1:2, :]
    a_log_r, dt_r = pr_ref[0, :, 0:1], pr_ref[0, :, 1:2]
    lane = lax.broadcasted_iota(jnp.int32, (CHUNK, LANES), 1)
    sub = lax.broadcasted_iota(jnp.int32, (4 * hb, CHUNK), 0)

    def split3(x):
        x1 = x.astype(BF16)
        r1 = x - x1.astype(F32)
        x2 = r1.astype(BF16)
        x3 = (r1 - x2.astype(F32)).astype(BF16)
        return x1, x2, x3

    def gates_into(gc_ref, gr_ref, n_chunks, off):
        def body(t4, carry):
            ts = [t4 * GATE_UNROLL + u for u in range(GATE_UNROLL)]
            xc = [gc_ref[0, t] for t in ts]
            xr = [gr_ref[0, 0, t] for t in ts]
            gc = [-jnp.exp(a_log_c) * _softplus(x + dt_c) for x in xc]
            gr = [-jnp.exp(a_log_r) * _softplus(x + dt_r) for x in xr]
            gcs = [split3(jnp.concatenate([jnp.where(lane < hb, g, 0.0),
                                           jnp.where((lane >= hb) & (lane < 2 * hb), g, 0.0)], axis=0))
                   for g in gc]
            grs = [split3(jnp.concatenate([jnp.where(sub < hb, g, 0.0),
                                           jnp.where((sub >= hb) & (sub < 2 * hb), g, 0.0)], axis=1))
                   for g in gr]
            gam_c = [sum(jnp.dot(tri_c, p, preferred_element_type=F32) for p in ps) for ps in gcs]
            gam_r = [sum(jnp.dot(p, tri_r, preferred_element_type=F32) for p in ps) for ps in grs]
            for u, t in enumerate(ts):
                gc_s[off + t] = jnp.where(lane < 2 * hb, gam_c[u], 1.0 / (1.0 + jnp.exp(-xc[u])))
                gr_s[off + t] = jnp.where(sub < 2 * hb, gam_r[u], 1.0 / (1.0 + jnp.exp(-xr[u])))
            return carry

        lax.fori_loop(0, n_chunks // GATE_UNROLL, body, 0)

    gates_into(gcc_ref, grc_ref, n_ctx, 0)
    gates_into(gcx_ref, grx_ref, n_lat, n_ctx)

    eye64 = (ri == ci).astype(F32)
    r128 = lax.broadcasted_iota(jnp.int32, (DN_HEAD_DIM, DN_HEAD_DIM), 0)
    c128 = lax.broadcasted_iota(jnp.int32, (DN_HEAD_DIM, DN_HEAD_DIM), 1)
    eye128 = (r128 == c128).astype(F32)
    onw = onw_ref[...]

    def chunk_steps(groups, out_mode):
        chains = [c for grp in groups for c in grp]
        rng = range(len(chains))
        cs = [d * hb + hh for d, hh, _ in chains]
        r0s = [pl.multiple_of(chunk * CHUNK, CHUNK) for _, _, chunk in chains]
        hss = [slice(hh * DN_HEAD_DIM, (hh + 1) * DN_HEAD_DIM) for _, hh, _ in chains]
        q = [q_s[pl.ds(r0s[i], CHUNK), hss[i]] for i in rng]
        k = [k_s[pl.ds(r0s[i], CHUNK), hss[i]] for i in rng]
        v = [v_s[pl.ds(r0s[i], CHUNK), hss[i]] for i in rng]
        gcol = [gc_s[chains[i][2]] for i in rng]
        grow = [gr_s[chains[i][2]] for i in rng]
        gam_c = [gcol[i][:, cs[i]:cs[i] + 1] for i in rng]
        beta_c = [gcol[i][:, 2 * hb + cs[i]:2 * hb + cs[i] + 1] for i in rng]
        gam_r = [grow[i][cs[i]:cs[i] + 1, :] for i in rng]
        fwd = [chains[i][0] == 0 for i in rng]
        g_tot = [gam_c[i][CHUNK - 1:CHUNK, :] if fwd[i] else gam_c[i][0:1, :] for i in rng]
        decay = [jnp.exp(jnp.where((ci <= ri) if fwd[i] else (ci >= ri), gam_c[i] - gam_r[i], -jnp.inf))
                 for i in rng]
        qk_kk = [_mm_nt(jnp.concatenate([q[i], k[i]], axis=0), k[i]) for i in rng]
        qkd = [qk_kk[i][:CHUNK] * decay[i] for i in rng]
        m = [jnp.where((ci < ri) if fwd[i] else (ci > ri), qk_kk[i][CHUNK:] * beta_c[i] * decay[i], 0.0)
             for i in rng]
        t_inv = [eye64 - m[i] for i in rng]
        pw = m
        for _ in range(5):
            pw = [_mm(pw[i], pw[i]) for i in rng]
            t_inv = [t_inv[i] + _mm(t_inv[i], pw[i]) for i in rng]
        e_gam = [jnp.exp(gam_c[i]) for i in rng]
        uw = [_mm(t_inv[i], jnp.concatenate([v[i] * beta_c[i], k[i] * (beta_c[i] * e_gam[i])], axis=1))
              for i in rng]
        qkw = [_mm_tn(k[i] * jnp.exp(g_tot[i] - gam_c[i]), uw[i]) for i in rng]
        p_mat = [jnp.exp(g_tot[i]) * eye128 - qkw[i][:, DN_HEAD_DIM:] for i in rng]
        if out_mode is not None:
            ow = [_mm(qkd[i], uw[i]) for i in rng]
            r_mat = [q[i] * e_gam[i] - ow[i][:, DN_HEAD_DIM:] for i in rng]

        state = {}
        pos = 0
        for grp in groups:
            idx = range(pos, pos + len(grp))
            pos += len(grp)
            s_old = [state[cs[i]] if cs[i] in state else s_s[cs[i]] for i in idx]
            s_new = [_mm(p_mat[i], s_old[n]) + qkw[i][:, :DN_HEAD_DIM] for n, i in enumerate(idx)]
            for n, i in enumerate(idx):
                state[cs[i]] = s_new[n]
            if out_mode is None:
                continue
            o = [_mm(r_mat[i], s_old[n]) + ow[i][:, :DN_HEAD_DIM] for n, i in enumerate(idx)]
            for n, i in enumerate(idx):
                rl = pl.multiple_of(r0s[i] - ctx_len, CHUNK)
                if out_mode == "store":
                    o_s[pl.ds(rl, CHUNK), hss[i]] = o[n]
                else:
                    tot = o[n] + o_s[pl.ds(rl, CHUNK), hss[i]]
                    y = tot * lax.rsqrt(jnp.mean(tot * tot, axis=-1, keepdims=True) + EPS) * onw
                    zz = z_ref[0, pl.ds(rl, CHUNK), hss[i]].astype(F32)
                    out_ref[0, pl.ds(rl, CHUNK), hss[i]] = (y * _silu(zz)).astype(out_ref.dtype)
        for c, val in state.items():
            s_s[c] = val

    s_s[...] = jnp.zeros(s_s.shape, F32)
    heads = range(hb)

    def groups_at(first, last, i):
        return [(0, hh, first + i) for hh in heads] + [(1, hh, last - i) for hh in heads]

    def ctx_body(j, carry):
        chunk_steps([groups_at(0, n_ctx - 1, 2 * j), groups_at(0, n_ctx - 1, 2 * j + 1)], None)
        return carry

    lax.fori_loop(0, n_ctx // 2, ctx_body, 0)

    def lat_body(mode, j, carry):
        lo, hi = n_ctx, n_ctx + n_lat - 1
        chunk_steps([groups_at(lo, hi, 2 * j), groups_at(lo, hi, 2 * j + 1)], mode)
        return carry

    lax.fori_loop(0, n_lat // 4, functools.partial(lat_body, "store"), 0)
    lax.fori_loop(n_lat // 4, n_lat // 2, functools.partial(lat_body, "final"), 0)


def _conv_shift_matrices():
    side = [j - CONV_K // 2 for j in range(CONV_K) if j != CONV_K // 2]
    win = CONV_TILE + 2 * CONV_HALO
    sh = np.zeros((3, len(side) * CONV_TILE, win), np.float32)
    for d in range(3):
        for n, off in enumerate(side):
            for i in range(CONV_TILE):
                m = i + off + d * CONV_HALO
                if 0 <= m < win:
                    sh[d, n * CONV_TILE + i, m] = 1.0
    return jnp.asarray(sh, BF16)


def _deltanet(qkv_x, qkv_c, z_x, gcol_x, grow_x, gcol_c, grow_c, conv_w, pcol, prow, onw, *, hb):
    b, seq, _ = qkv_x.shape
    ctx_len = qkv_c.shape[1]
    ng = DN_HEADS // hb
    w = hb * DN_HEAD_DIM
    n_lat, n_ctx = seq // CHUNK, ctx_len // CHUNK
    assert n_lat % 4 == 0 and n_ctx % 2 == 0 and seq >= 2 * CONV_TILE and ctx_len >= 2 * CONV_TILE
    assert n_lat % GATE_UNROLL == 0 and n_ctx % GATE_UNROLL == 0
    sh = _conv_shift_matrices()

    def col(part):
        return lambda bi, g: (bi, 0, part * ng + g)

    def cwcol(part):
        return lambda bi, g: (0, part * ng + g)

    return pl.pallas_call(
        functools.partial(_dn_kernel, hb=hb, seq=seq, ctx_len=ctx_len),
        grid=(b, ng),
        in_specs=[
            pl.BlockSpec((1, seq, w), col(0)),
            pl.BlockSpec((1, seq, w), col(1)),
            pl.BlockSpec((1, seq, w), col(2)),
            pl.BlockSpec((1, ctx_len, w), col(0)),
            pl.BlockSpec((1, ctx_len, w), col(1)),
            pl.BlockSpec((1, ctx_len, w), col(2)),
            pl.BlockSpec((1, seq, w), lambda bi, g: (bi, 0, g)),
            pl.BlockSpec((1, n_lat, CHUNK, LANES), lambda bi, g: (bi, 0, 0, g)),
            pl.BlockSpec((1, 1, n_lat, 4 * hb, CHUNK), lambda bi, g: (bi, g, 0, 0, 0)),
            pl.BlockSpec((1, n_ctx, CHUNK, LANES), lambda bi, g: (bi, 0, 0, g)),
            pl.BlockSpec((1, 1, n_ctx, 4 * hb, CHUNK), lambda bi, g: (bi, g, 0, 0, 0)),
            pl.BlockSpec((CONV_K, w), cwcol(0)),
            pl.BlockSpec((CONV_K, w), cwcol(1)),
            pl.BlockSpec((CONV_K, w), cwcol(2)),
            pl.BlockSpec((1, 2, LANES), lambda bi, g: (g, 0, 0)),
            pl.BlockSpec((1, 4 * hb, 2), lambda bi, g: (g, 0, 0)),
            pl.BlockSpec((1, DN_HEAD_DIM), lambda bi, g: (0, 0)),
            pl.BlockSpec(sh.shape, lambda bi, g: (0, 0, 0)),
        ],
        out_specs=pl.BlockSpec((1, seq, w), lambda bi, g: (bi, 0, g)),
        out_shape=jax.ShapeDtypeStruct((b, seq, DN_HEADS * DN_HEAD_DIM), BF16),
        scratch_shapes=[
            pltpu.VMEM((ctx_len + seq, w), F32),
            pltpu.VMEM((ctx_len + seq, w), F32),
            pltpu.VMEM((ctx_len + seq, w), F32),
            pltpu.VMEM((n_ctx + n_lat, CHUNK, LANES), F32),
            pltpu.VMEM((n_ctx + n_lat, 4 * hb, CHUNK), F32),
            pltpu.VMEM((2 * hb, DN_HEAD_DIM, DN_HEAD_DIM), F32),
            pltpu.VMEM((seq, w), F32),
        ],
        compiler_params=pltpu.CompilerParams(
            dimension_semantics=("arbitrary", "arbitrary"), vmem_limit_bytes=VMEM_LIMIT),
        name="deltanet",
    )(qkv_x, qkv_x, qkv_x, qkv_c, qkv_c, qkv_c, z_x, gcol_x, grow_x, gcol_c, grow_c,
      conv_w, conv_w, conv_w, pcol, prow, onw, sh)


def _na_kernel(q_ref, k_ref, v_ref, kc_ref, vc_ref, qw_ref, kw_ref, bias_ref, seg_ref, out_ref,
               qn_s, kn_s, knc_s, *, seq, ctx_len):
    rows = seq // GRID_W
    win_rows = min(WIN_ROWS, rows)
    n_win = win_rows * GRID_W
    n_pairs = NA_HEADS // 2

    def head_rms(x, wgt):
        ss = jnp.dot((x * x).astype(BF16), seg_ref[...], preferred_element_type=F32)
        return x * lax.rsqrt(ss * (1.0 / NA_HEAD_DIM) + EPS) * wgt

    kw = kw_ref[...]
    qw = qw_ref[...] * ((NA_HEAD_DIM ** -0.5) * LOG2E)
    tile = 256

    def norm_body(t, carry):
        r0 = pl.multiple_of(t * tile, tile)
        kn_s[pl.ds(r0, tile), :] = head_rms(k_ref[0, pl.ds(r0, tile), :].astype(F32), kw).astype(BF16)
        qn_s[pl.ds(r0, tile), :] = head_rms(q_ref[0, pl.ds(r0, tile), :].astype(F32), qw).astype(BF16)
        return carry

    lax.fori_loop(0, seq // tile, norm_body, 0)
    knc_s[...] = head_rms(kc_ref[0].astype(F32), kw).astype(BF16)

    lane2 = lax.broadcasted_iota(jnp.int32, (GRID_W, LANES), 1)
    lo2 = lane2 < NA_HEAD_DIM
    ones_w = jnp.ones((n_win, LANES), BF16)
    ones_c = jnp.ones((ctx_len, LANES), BF16)
    zero = jnp.zeros((GRID_W, LANES), BF16)

    def row_body(step, carry):
        streams = [(u, j) for u in range(NA_ROW_UNROLL) for j in range(n_pairs)]
        ns = range(len(streams))
        row = [step * NA_ROW_UNROLL + u for u, _ in streams]
        r_start = [jnp.clip(r - win_rows // 2, 0, rows - win_rows) for r in row]
        case = [r_start[n] - row[n] + (WIN_ROWS - 1) for n in ns]
        q0 = [pl.multiple_of(r * GRID_W, GRID_W) for r in row]
        k0 = [pl.multiple_of(r_start[n] * GRID_W, GRID_W) for n in ns]
        lss = [slice(j * LANES, (j + 1) * LANES) for _, j in streams]
        q2 = [qn_s[pl.ds(q0[n], GRID_W), lss[n]] for n in ns]
        qs = [jnp.concatenate([jnp.where(lo2, q2[n], zero), jnp.where(lo2, zero, q2[n])], axis=0)
              for n in ns]
        s_w = [_mm_nt(qs[n], kn_s[pl.ds(k0[n], n_win), lss[n]]) + bias_ref[case[n], streams[n][1]]
               for n in ns]
        s_c = [_mm_nt(qs[n], knc_s[:, lss[n]]) for n in ns]
        mx = [s_c[n][:, 0:LANES] for n in ns]
        for n in ns:
            for t in range(1, ctx_len // LANES):
                mx[n] = jnp.maximum(mx[n], s_c[n][:, t * LANES:(t + 1) * LANES])
            for t in range(n_win // LANES):
                mx[n] = jnp.maximum(mx[n], s_w[n][:, t * LANES:(t + 1) * LANES])
        mx = [jnp.max(mx[n], axis=-1, keepdims=True) for n in ns]
        p_w = [jnp.exp2(s_w[n] - mx[n]).astype(BF16) for n in ns]
        p_c = [jnp.exp2(s_c[n] - mx[n]).astype(BF16) for n in ns]
        o = [_mm(p_w[n], jnp.concatenate([v_ref[0, pl.ds(k0[n], n_win), lss[n]], ones_w], axis=1))
             + _mm(p_c[n], jnp.concatenate([vc_ref[0, :, lss[n]], ones_c], axis=1)) for n in ns]
        for n in ns:
            on = o[n][:, :LANES] * (1.0 / o[n][:, LANES:])
            out_ref[0, pl.ds(q0[n], GRID_W), lss[n]] = jnp.where(lo2, on[:GRID_W], on[GRID_W:]).astype(out_ref.dtype)
        return carry

    lax.fori_loop(0, rows // NA_ROW_UNROLL, row_body, 0)


def _natten(na_x, na_c, qw, kw, bias):
    b, seq, _ = na_x.shape
    ctx_len = na_c.shape[1]
    wq = NA_HEADS * NA_HEAD_DIM
    assert seq % 256 == 0 and ctx_len % LANES == 0 and (seq // GRID_W) % NA_ROW_UNROLL == 0
    head_of = np.arange(wq) // NA_HEAD_DIM
    seg = jnp.asarray(head_of[:, None] == head_of[None, :], BF16)
    return pl.pallas_call(
        functools.partial(_na_kernel, seq=seq, ctx_len=ctx_len),
        grid=(b,),
        in_specs=[
            pl.BlockSpec((1, seq, wq), lambda bi: (bi, 0, 0)),
            pl.BlockSpec((1, seq, wq), lambda bi: (bi, 0, 1)),
            pl.BlockSpec((1, seq, wq), lambda bi: (bi, 0, 2)),
            pl.BlockSpec((1, ctx_len, wq), lambda bi: (bi, 0, 1)),
            pl.BlockSpec((1, ctx_len, wq), lambda bi: (bi, 0, 2)),
            pl.BlockSpec((1, wq), lambda bi: (0, 0)),
            pl.BlockSpec((1, wq), lambda bi: (0, 0)),
            pl.BlockSpec(bias.shape, lambda bi: (0, 0, 0, 0)),
            pl.BlockSpec((wq, wq), lambda bi: (0, 0)),
        ],
        out_specs=pl.BlockSpec((1, seq, wq), lambda bi: (bi, 0, 0)),
        out_shape=jax.ShapeDtypeStruct((b, seq, wq), BF16),
        scratch_shapes=[
            pltpu.VMEM((seq, wq), BF16),
            pltpu.VMEM((seq, wq), BF16),
            pltpu.VMEM((ctx_len, wq), BF16),
        ],
        compiler_params=pltpu.CompilerParams(
            dimension_semantics=("arbitrary",), vmem_limit_bytes=VMEM_LIMIT),
        name="natten",
    )(na_x, na_x, na_x, na_c, na_c, qw, kw, bias, seg)


def _na_bias_table(rpb, rows):
    win_rows = min(WIN_ROWS, rows)
    cols = np.arange(GRID_W)
    win_start = np.clip(cols - WIN_COLS // 2, 0, GRID_W - WIN_COLS)
    kc = cols[None, :]
    valid = (kc >= win_start[:, None]) & (kc < win_start[:, None] + WIN_COLS)
    rel = np.clip(kc - cols[:, None] + WIN_COLS - 1, 0, 2 * WIN_COLS - 2)
    g = rpb[:, :, rel] * LOG2E
    n_case = WIN_ROWS
    tabs = []
    for case in range(n_case):
        dr = np.clip(case + np.arange(win_rows), 0, 2 * WIN_ROWS - 2)
        t = jnp.transpose(g[:, dr], (0, 2, 1, 3))
        t = jnp.where(valid[None, :, None, :], t, -1e30)
        tabs.append(t.reshape(NA_HEADS // 2, 2 * GRID_W, win_rows * GRID_W))
    return jnp.stack(tabs, axis=0).astype(F32)


def _outffn_kernel(x_ref, dn_ref, na_ref, mod_ref, nw_ref, wo_ref, wi_ref, wf_ref, out_ref, *, d_ff, splits):
    half = dn_ref.shape[-1]
    attn = (jnp.dot(dn_ref[0], wo_ref[0:half, :], preferred_element_type=F32)
            + jnp.dot(na_ref[0], wo_ref[half:, :], preferred_element_type=F32))
    x1 = x_ref[0] + mod_ref[0, 2:3, :] * attn
    ms = jnp.mean(x1 * x1, axis=-1, keepdims=True)
    h = (x1 * lax.rsqrt(ms + EPS)) * nw_ref[...]
    hb = (h * (1.0 + mod_ref[0, 4:5, :]) + mod_ref[0, 3:4, :]).astype(BF16)
    acc = None
    off = 0
    for n in splits:
        gate = jnp.dot(hb, wi_ref[:, off:off + n], preferred_element_type=F32)
        up = jnp.dot(hb, wi_ref[:, d_ff + off:d_ff + off + n], preferred_element_type=F32)
        part = jnp.dot((_silu(gate) * up).astype(BF16), wf_ref[off:off + n, :], preferred_element_type=F32)
        acc = part if acc is None else acc + part
        off += n
    out_ref[0] = x1 + mod_ref[0, 5:6, :] * acc


def _out_ffn(x, dn, na, mods, norm_w, w_out, w_ffn_in, w_ffn_out, *, tm):
    b, l, d = x.shape
    half = dn.shape[-1]
    d_ff = w_ffn_out.shape[0]
    assert sum(FFN_SPLITS) == d_ff
    tm = min(tm, l)
    row_map = lambda bi, i: (bi, i, 0)
    const = lambda bi, i: (0, 0)
    single = pl.Buffered(1)
    return pl.pallas_call(
        functools.partial(_outffn_kernel, d_ff=d_ff, splits=FFN_SPLITS),
        grid=(b, l // tm),
        in_specs=[
            pl.BlockSpec((1, tm, d), row_map),
            pl.BlockSpec((1, tm, half), row_map),
            pl.BlockSpec((1, tm, half), row_map),
            pl.BlockSpec((1, 6, d), lambda bi, i: (bi, 0, 0)),
            pl.BlockSpec((1, d), const),
            pl.BlockSpec(w_out.shape, const, pipeline_mode=single),
            pl.BlockSpec(w_ffn_in.shape, const, pipeline_mode=single),
            pl.BlockSpec(w_ffn_out.shape, const, pipeline_mode=single),
        ],
        out_specs=pl.BlockSpec((1, tm, d), row_map),
        out_shape=jax.ShapeDtypeStruct((b, l, d), F32),
        compiler_params=pltpu.CompilerParams(
            dimension_semantics=("arbitrary", "arbitrary"), vmem_limit_bytes=VMEM_LIMIT),
        name="out_ffn",
    )(x, dn, na, mods, norm_w, w_out, w_ffn_in, w_ffn_out)


def _gate_layouts(ab, hb):
    b, l, n = ab.shape
    ng = n // LANES
    nc = l // CHUNK
    col = ab.reshape(b, nc, CHUNK, n)
    row = ab.reshape(b, nc, CHUNK, ng, LANES)[..., :4 * hb]
    row = jnp.transpose(row, (0, 3, 1, 4, 2))
    return col, row


def kernel(x, c, ctx, c_ctx, norm1_w, norm2_w, w_ada, b_ada, w_in, dn_conv_w, dn_A_log, dn_dt_bias,
           dn_out_norm_w, na_q_norm_w, na_k_norm_w, na_rpb, w_out, w_ffn_in, w_ffn_out):
    assert w_in.shape[0] == 1, "single-layer problem"
    b, seq, d = x.shape
    hb = DN_HB
    ng = DN_HEADS // hb
    dn_w = DN_HEADS * DN_HEAD_DIM
    na_w = NA_HEADS * NA_HEAD_DIM
    rows = seq // GRID_W

    n_mod_rows = -(-(b + 1) // 8) * 8
    c_all = jnp.zeros((n_mod_rows, d), F32).at[:b].set(c).at[b].set(c_ctx)
    mod = _adaln(c_all, w_ada[0], b_ada[0][None, :])
    mod_x = mod[:b].reshape(b, 6, d)
    mod_c = mod[b:b + 1].reshape(1, 6, d)

    w0 = w_in[0]
    gate0 = 4 * dn_w
    na0 = gate0 + 4 * DN_HEADS
    gate_cols = w0[:, gate0:na0].reshape(d, 4, ng, hb)
    gate_cols = jnp.transpose(gate_cols, (0, 2, 1, 3)).reshape(d, ng, 4 * hb)
    gate_cols = jnp.pad(gate_cols, ((0, 0), (0, 0), (0, LANES - 4 * hb))).reshape(d, ng * LANES)
    w_perm = jnp.concatenate([w0[:, :gate0], w0[:, na0:], gate_cols], axis=1).astype(BF16)

    proj = functools.partial(_in_proj, norm_w=norm1_w[0][None, :], w_perm=w_perm,
                             n_dn=3 * dn_w, n_z=dn_w, n_na=3 * na_w, tm=ROW_TILE)
    qkv_x, z_x, na_x, ab_x = proj(x, mod_x, per_batch_mod=True)
    qkv_c, _, na_c, ab_c = proj(ctx, mod_c, per_batch_mod=False)

    def group_params(p):
        return jnp.transpose(p[0].reshape(2, ng, hb), (1, 0, 2)).reshape(ng, 2 * hb)

    pg = jnp.stack([group_params(dn_A_log), group_params(dn_dt_bias)], axis=1)
    pcol = jnp.pad(pg, ((0, 0), (0, 0), (0, LANES - 2 * hb)))
    prow = jnp.pad(jnp.transpose(pg, (0, 2, 1)), ((0, 0), (0, 2 * hb), (0, 0)))
    gcol_x, grow_x = _gate_layouts(ab_x, hb)
    gcol_c, grow_c = _gate_layouts(ab_c, hb)
    dn_x = _deltanet(qkv_x, qkv_c, z_x, gcol_x, grow_x, gcol_c, grow_c, dn_conv_w[0], pcol, prow,
                     dn_out_norm_w[0][None, :], hb=hb)

    bias = _na_bias_table(na_rpb[0], rows)
    qw = jnp.tile(na_q_norm_w[0], NA_HEADS)[None, :]
    kw = jnp.tile(na_k_norm_w[0], NA_HEADS)[None, :]
    na_o = _natten(na_x, na_c, qw, kw, bias)

    return _out_ffn(x, dn_x, na_o, mod_x, norm2_w[0][None, :], w_out[0].astype(BF16),
                    w_ffn_in[0].astype(BF16), w_ffn_out[0].astype(BF16), tm=ROW_TILE)
```

```python
import functools

import numpy as np
import jax
import jax.numpy as jnp
from jax import lax
from jax.experimental import pallas as pl
from jax.experimental.pallas import tpu as pltpu

F32 = jnp.float32
BF16 = jnp.bfloat16
HIGHEST = lax.Precision.HIGHEST

EPS = 1e-6
LOG2E = 1.4426950408889634
CHUNK = 64
CONV_K = 5
DN_HEAD_DIM = 128
DN_HEADS = 4
NA_HEAD_DIM = 64
NA_HEADS = 8
GRID_W = 64
WIN_ROWS = 8
WIN_COLS = 16
LANES = 128
VMEM_LIMIT = 56 * 1024 * 1024

DN_HB = 4
CONV_TILE = 128
CONV_HALO = 16
GATE_UNROLL = 4
NA_ROW_UNROLL = 4
IN_ROW_TILE = 1024
ROW_TILE = 512
FFN_SPLITS = (1024, 1024, 768)


def _silu(x):
    return x * (1.0 / (1.0 + jnp.exp(-x)))


def _softplus(x):
    return jnp.maximum(x, 0.0) + jnp.log(1.0 + jnp.exp(-jnp.abs(x)))


def _mm(a, b):
    return jnp.dot(a.astype(BF16), b.astype(BF16), preferred_element_type=F32)


def _mm_nt(a, b):
    return lax.dot_general(a.astype(BF16), b.astype(BF16), (((1,), (1,)), ((), ())),
                           preferred_element_type=F32)


def _mm_tn(a, b):
    return lax.dot_general(a.astype(BF16), b.astype(BF16), (((0,), (0,)), ((), ())),
                           preferred_element_type=F32)


def _mm_f32(a, b):
    return lax.dot_general(a, b, (((1,), (0,)), ((), ())), precision=HIGHEST,
                           preferred_element_type=F32)


def _adaln_kernel(c_ref, w_ref, b_ref, o_ref):
    o_ref[...] = _mm_f32(_silu(c_ref[...]), w_ref[...]) + b_ref[...]


def _adaln(c_all, w_ada, b_ada):
    rows, d = c_all.shape
    n = w_ada.shape[1]
    tn = 1024
    return pl.pallas_call(
        _adaln_kernel,
        grid=(n // tn,),
        in_specs=[
            pl.BlockSpec((rows, d), lambda j: (0, 0)),
            pl.BlockSpec((d, tn), lambda j: (0, j)),
            pl.BlockSpec((1, tn), lambda j: (0, j)),
        ],
        out_specs=pl.BlockSpec((rows, tn), lambda j: (0, j)),
        out_shape=jax.ShapeDtypeStruct((rows, n), F32),
        compiler_params=pltpu.CompilerParams(
            dimension_semantics=("arbitrary",), vmem_limit_bytes=VMEM_LIMIT),
        name="adaln",
    )(c_all, w_ada, b_ada)


def _inproj_kernel(x_ref, mod_ref, nw_ref, w_ref, *out_refs, widths):
    x = x_ref[0]
    ms = jnp.mean(x * x, axis=-1, keepdims=True)
    h = (x * lax.rsqrt(ms + EPS)) * nw_ref[...]
    h = h * (1.0 + mod_ref[0, 1:2, :]) + mod_ref[0, 0:1, :]
    hb = h.astype(BF16)
    off = 0
    for o_ref, n in zip(out_refs, widths):
        o_ref[0] = jnp.dot(hb, w_ref[:, off:off + n], preferred_element_type=F32).astype(o_ref.dtype)
        off += n


def _in_proj(x, mods, norm_w, w_cols, *, groups, per_batch_mod, tm):
    b, l, d = x.shape
    widths = tuple(n for n, _ in groups)
    assert sum(widths) == w_cols.shape[1]
    tm = min(tm, l)
    mod_map = (lambda bi, i: (bi, 0, 0)) if per_batch_mod else (lambda bi, i: (0, 0, 0))
    row_map = lambda bi, i: (bi, i, 0)
    return pl.pallas_call(
        functools.partial(_inproj_kernel, widths=widths),
        grid=(b, l // tm),
        in_specs=[
            pl.BlockSpec((1, tm, d), row_map),
            pl.BlockSpec((1, 6, d), mod_map),
            pl.BlockSpec((1, d), lambda bi, i: (0, 0)),
            pl.BlockSpec(w_cols.shape, lambda bi, i: (0, 0)),
        ],
        out_specs=[pl.BlockSpec((1, tm, n), row_map) for n in widths],
        out_shape=[jax.ShapeDtypeStruct((b, l, n), dt) for n, dt in groups],
        compiler_params=pltpu.CompilerParams(
            dimension_semantics=("arbitrary", "arbitrary"), vmem_limit_bytes=VMEM_LIMIT),
        name="in_proj",
    )(x, mods, norm_w, w_cols)


def _dn_kernel(qx_ref, kx_ref, vx_ref, qc_ref, kc_ref, vc_ref, z_ref,
               gcx_ref, gcc_ref,
               cwq_ref, cwk_ref, cwv_ref, pc_ref, onw_ref, sh_ref,
               out_ref,
               q_s, k_s, v_s, gc_s, gr_s, s_s, o_s, *, hb, seq, ctx_len):
    n_ctx = ctx_len // CHUNK
    n_lat = seq // CHUNK
    tile = CONV_TILE
    win = tile + 2 * CONV_HALO
    side_taps = [j for j in range(CONV_K) if j != CONV_K // 2]

    streams = ((cwq_ref, q_s, "q"), (cwk_ref, k_s, "k"), (cwv_ref, v_s, "v"))

    def conv_tiles(srcs, dst_off, r0, a0, sh):
        zs = [jnp.dot(sh, src[0, pl.ds(a0, win), :], preferred_element_type=F32) for src in srcs]
        ys = []
        for src, z, (cw_ref, _, _) in zip(srcs, zs, streams):
            cw = cw_ref[...]
            acc = cw[CONV_K // 2:CONV_K // 2 + 1, :] * src[0, pl.ds(r0, tile), :].astype(F32)
            for n, j in enumerate(side_taps):
                acc = acc + cw[j:j + 1, :] * z[n * tile:(n + 1) * tile, :]
            ys.append(_silu(acc))
        for y, (_, dst_ref, mode) in zip(ys, streams):
            if mode != "v":
                parts = []
                for hh in range(hb):
                    seg = y[:, hh * DN_HEAD_DIM:(hh + 1) * DN_HEAD_DIM]
                    inv = lax.rsqrt(jnp.sum(seg * seg, axis=-1, keepdims=True) + EPS)
                    if mode == "q":
                        inv = inv * (DN_HEAD_DIM ** -0.5)
                    parts.append(seg * inv)
                y = parts[0] if hb == 1 else jnp.concatenate(parts, axis=1)
            dst_ref[pl.ds(dst_off + r0, tile), :] = y

    def conv_into(srcs, n_rows, dst_off):
        n_tiles = n_rows // tile
        conv_tiles(srcs, dst_off, 0, 0, sh_ref[0])
        if n_tiles > 2:
            def body(t, carry):
                r0 = pl.multiple_of(t * tile, tile)
                conv_tiles(srcs, dst_off, r0, pl.multiple_of(r0 - CONV_HALO, CONV_HALO), sh_ref[1])
                return carry

            lax.fori_loop(1, n_tiles - 1, body, 0)
        conv_tiles(srcs, dst_off, n_rows - tile, n_rows - win, sh_ref[2])

    conv_into((qc_ref, kc_ref, vc_ref), ctx_len, 0)
    conv_into((qx_ref, kx_ref, vx_ref), seq, ctx_len)

    ri = lax.broadcasted_iota(jnp.int32, (CHUNK, CHUNK), 0)
    ci = lax.broadcasted_iota(jnp.int32, (CHUNK, CHUNK), 1)
    lower = jnp.where(ci <= ri, 1.0, 0.0).astype(BF16)
    upper = jnp.where(ci >= ri, 1.0, 0.0).astype(BF16)
    tri_c = jnp.concatenate([lower, upper], axis=1)
    tri_r = jnp.concatenate([upper, lower], axis=0)
    a_log_c, dt_c = pc_ref[0, 0:1, :], pc_ref[0, 1:2, :]
    lane = lax.broadcasted_iota(jnp.int32, (CHUNK, LANES), 1)
    sub = lax.broadcasted_iota(jnp.int32, (4 * hb, CHUNK), 0)
    to_rows = jnp.where(lax.broadcasted_iota(jnp.int32, (4 * hb, LANES), 0)
                        == lax.broadcasted_iota(jnp.int32, (4 * hb, LANES), 1), 1.0, 0.0).astype(BF16)

    def split3(x):
        x1 = x.astype(BF16)
        r1 = x - x1.astype(F32)
        x2 = r1.astype(BF16)
        x3 = (r1 - x2.astype(F32)).astype(BF16)
        return x1, x2, x3

    def gates_into(gc_ref, n_chunks, off):
        def body(t4, carry):
            ts = [t4 * GATE_UNROLL + u for u in range(GATE_UNROLL)]
            xc = [gc_ref[0, t] for t in ts]
            val = [jnp.where(lane < 2 * hb, -jnp.exp(a_log_c) * _softplus(x + dt_c), 1.0 / (1.0 + jnp.exp(-x)))
                   for x in xc]
            vs = [split3(v) for v in val]
            val_r = [sum(lax.dot_general(to_rows, p, (((1,), (1,)), ((), ())), preferred_element_type=F32)
                         for p in ps) for ps in vs]
            gcs = [[jnp.concatenate([jnp.where(lane < hb, p, 0.0),
                                     jnp.where((lane >= hb) & (lane < 2 * hb), p, 0.0)], axis=0) for p in ps]
                   for ps in vs]
            grs = [split3(jnp.concatenate([jnp.where(sub < hb, v, 0.0),
                                           jnp.where((sub >= hb) & (sub < 2 * hb), v, 0.0)], axis=1))
                   for v in val_r]
            gam_c = [sum(jnp.dot(tri_c, p, preferred_element_type=F32) for p in ps) for ps in gcs]
            gam_r = [sum(jnp.dot(p, tri_r, preferred_element_type=F32) for p in ps) for ps in grs]
            for u, t in enumerate(ts):
                gc_s[off + t] = jnp.where(lane < 2 * hb, gam_c[u], val[u])
                gr_s[off + t] = jnp.where(sub < 2 * hb, gam_r[u], val_r[u])
            return carry

        lax.fori_loop(0, n_chunks // GATE_UNROLL, body, 0)

    gates_into(gcc_ref, n_ctx, 0)
    gates_into(gcx_ref, n_lat, n_ctx)

    eye64 = (ri == ci).astype(F32)
    r128 = lax.broadcasted_iota(jnp.int32, (DN_HEAD_DIM, DN_HEAD_DIM), 0)
    c128 = lax.broadcasted_iota(jnp.int32, (DN_HEAD_DIM, DN_HEAD_DIM), 1)
    eye128 = (r128 == c128).astype(F32)
    onw = onw_ref[...]

    def chunk_steps(groups, out_mode):
        chains = [c for grp in groups for c in grp]
        rng = range(len(chains))
        cs = [d * hb + hh for d, hh, _ in chains]
        r0s = [pl.multiple_of(chunk * CHUNK, CHUNK) for _, _, chunk in chains]
        hss = [slice(hh * DN_HEAD_DIM, (hh + 1) * DN_HEAD_DIM) for _, hh, _ in chains]
        q = [q_s[pl.ds(r0s[i], CHUNK), hss[i]] for i in rng]
        k = [k_s[pl.ds(r0s[i], CHUNK), hss[i]] for i in rng]
        v = [v_s[pl.ds(r0s[i], CHUNK), hss[i]] for i in rng]
        gcol = [gc_s[chains[i][2]] for i in rng]
        grow = [gr_s[chains[i][2]] for i in rng]
        gam_c = [gcol[i][:, cs[i]:cs[i] + 1] for i in rng]
        beta_c = [gcol[i][:, 2 * hb + cs[i]:2 * hb + cs[i] + 1] for i in rng]
        gam_r = [grow[i][cs[i]:cs[i] + 1, :] for i in rng]
        fwd = [chains[i][0] == 0 for i in rng]
        g_tot = [gam_c[i][CHUNK - 1:CHUNK, :] if fwd[i] else gam_c[i][0:1, :] for i in rng]
        decay = [jnp.exp(jnp.where((ci <= ri) if fwd[i] else (ci >= ri), gam_c[i] - gam_r[i], -jnp.inf))
                 for i in rng]
        qk_kk = [_mm_nt(jnp.concatenate([q[i], k[i]], axis=0), k[i]) for i in rng]
        qkd = [qk_kk[i][:CHUNK] * decay[i] for i in rng]
        m = [jnp.where((ci < ri) if fwd[i] else (ci > ri), qk_kk[i][CHUNK:] * beta_c[i] * decay[i], 0.0)
             for i in rng]
        t_inv = [eye64 - m[i] for i in rng]
        pw = m
        for _ in range(5):
            pw = [_mm(pw[i], pw[i]) for i in rng]
            t_inv = [t_inv[i] + _mm(t_inv[i], pw[i]) for i in rng]
        e_gam = [jnp.exp(gam_c[i]) for i in rng]
        uw = [_mm(t_inv[i], jnp.concatenate([v[i] * beta_c[i], k[i] * (beta_c[i] * e_gam[i])], axis=1))
              for i in rng]
        qkw = [_mm_tn(k[i] * jnp.exp(g_tot[i] - gam_c[i]), uw[i]) for i in rng]
        p_mat = [jnp.exp(g_tot[i]) * eye128 - qkw[i][:, DN_HEAD_DIM:] for i in rng]
        if out_mode is not None:
            ow = [_mm(qkd[i], uw[i]) for i in rng]
            r_mat = [q[i] * e_gam[i] - ow[i][:, DN_HEAD_DIM:] for i in rng]

        state = {}
        pos = 0
        for grp in groups:
            idx = range(pos, pos + len(grp))
            pos += len(grp)
            s_old = [state[cs[i]] if cs[i] in state else s_s[cs[i]] for i in idx]
            s_new = [_mm(p_mat[i], s_old[n]) + qkw[i][:, :DN_HEAD_DIM] for n, i in enumerate(idx)]
            for n, i in enumerate(idx):
                state[cs[i]] = s_new[n]
            if out_mode is None:
                continue
            o = [_mm(r_mat[i], s_old[n]) + ow[i][:, :DN_HEAD_DIM] for n, i in enumerate(idx)]
            for n, i in enumerate(idx):
                rl = pl.multiple_of(r0s[i] - ctx_len, CHUNK)
                if out_mode == "store":
                    o_s[pl.ds(rl, CHUNK), hss[i]] = o[n]
                else:
                    tot = o[n] + o_s[pl.ds(rl, CHUNK), hss[i]]
                    y = tot * lax.rsqrt(jnp.mean(tot * tot, axis=-1, keepdims=True) + EPS) * onw
                    zz = z_ref[0, pl.ds(rl, CHUNK), hss[i]].astype(F32)
                    out_ref[0, pl.ds(rl, CHUNK), hss[i]] = (y * _silu(zz)).astype(out_ref.dtype)
        for c, val in state.items():
            s_s[c] = val

    s_s[...] = jnp.zeros(s_s.shape, F32)
    heads = range(hb)

    def groups_at(first, last, i):
        return [(0, hh, first + i) for hh in heads] + [(1, hh, last - i) for hh in heads]

    def ctx_body(j, carry):
        chunk_steps([groups_at(0, n_ctx - 1, 2 * j), groups_at(0, n_ctx - 1, 2 * j + 1)], None)
        return carry

    lax.fori_loop(0, n_ctx // 2, ctx_body, 0)

    def lat_body(mode, j, carry):
        lo, hi = n_ctx, n_ctx + n_lat - 1
        chunk_steps([groups_at(lo, hi, 2 * j), groups_at(lo, hi, 2 * j + 1)], mode)
        return carry

    lax.fori_loop(0, n_lat // 4, functools.partial(lat_body, "store"), 0)
    lax.fori_loop(n_lat // 4, n_lat // 2, functools.partial(lat_body, "final"), 0)


def _conv_shift_matrices():
    side = [j - CONV_K // 2 for j in range(CONV_K) if j != CONV_K // 2]
    win = CONV_TILE + 2 * CONV_HALO
    sh = np.zeros((3, len(side) * CONV_TILE, win), np.float32)
    for d in range(3):
        for n, off in enumerate(side):
            for i in range(CONV_TILE):
                m = i + off + d * CONV_HALO
                if 0 <= m < win:
                    sh[d, n * CONV_TILE + i, m] = 1.0
    return jnp.asarray(sh, BF16)


def _deltanet(qkv_x, qkv_c, z_x, gcol_x, gcol_c, conv_w, pcol, onw, *, hb):
    b, seq, _ = qkv_x.shape
    ctx_len = qkv_c.shape[1]
    ng = DN_HEADS // hb
    w = hb * DN_HEAD_DIM
    n_lat, n_ctx = seq // CHUNK, ctx_len // CHUNK
    assert n_lat % 4 == 0 and n_ctx % 2 == 0 and seq >= 2 * CONV_TILE and ctx_len >= 2 * CONV_TILE
    assert n_lat % GATE_UNROLL == 0 and n_ctx % GATE_UNROLL == 0
    sh = _conv_shift_matrices()

    def col(part):
        return lambda bi, g: (bi, 0, part * ng + g)

    def cwcol(part):
        return lambda bi, g: (0, part * ng + g)

    return pl.pallas_call(
        functools.partial(_dn_kernel, hb=hb, seq=seq, ctx_len=ctx_len),
        grid=(b, ng),
        in_specs=[
            pl.BlockSpec((1, seq, w), col(0)),
            pl.BlockSpec((1, seq, w), col(1)),
            pl.BlockSpec((1, seq, w), col(2)),
            pl.BlockSpec((1, ctx_len, w), col(0)),
            pl.BlockSpec((1, ctx_len, w), col(1)),
            pl.BlockSpec((1, ctx_len, w), col(2)),
            pl.BlockSpec((1, seq, w), lambda bi, g: (bi, 0, g)),
            pl.BlockSpec((1, n_lat, CHUNK, LANES), lambda bi, g: (bi, 0, 0, g)),
            pl.BlockSpec((1, n_ctx, CHUNK, LANES), lambda bi, g: (bi, 0, 0, g)),
            pl.BlockSpec((CONV_K, w), cwcol(0)),
            pl.BlockSpec((CONV_K, w), cwcol(1)),
            pl.BlockSpec((CONV_K, w), cwcol(2)),
            pl.BlockSpec((1, 2, LANES), lambda bi, g: (g, 0, 0)),
            pl.BlockSpec((1, DN_HEAD_DIM), lambda bi, g: (0, 0)),
            pl.BlockSpec(sh.shape, lambda bi, g: (0, 0, 0)),
        ],
        out_specs=pl.BlockSpec((1, seq, w), lambda bi, g: (bi, 0, g)),
        out_shape=jax.ShapeDtypeStruct((b, seq, DN_HEADS * DN_HEAD_DIM), BF16),
        scratch_shapes=[
            pltpu.VMEM((ctx_len + seq, w), F32),
            pltpu.VMEM((ctx_len + seq, w), F32),
            pltpu.VMEM((ctx_len + seq, w), F32),
            pltpu.VMEM((n_ctx + n_lat, CHUNK, LANES), F32),
            pltpu.VMEM((n_ctx + n_lat, 4 * hb, CHUNK), F32),
            pltpu.VMEM((2 * hb, DN_HEAD_DIM, DN_HEAD_DIM), F32),
            pltpu.VMEM((seq, w), F32),
        ],
        compiler_params=pltpu.CompilerParams(
            dimension_semantics=("arbitrary", "arbitrary"), vmem_limit_bytes=VMEM_LIMIT),
        name="deltanet",
    )(qkv_x, qkv_x, qkv_x, qkv_c, qkv_c, qkv_c, z_x, gcol_x, gcol_c,
      conv_w, conv_w, conv_w, pcol, onw, sh)


def _na_kernel(q_ref, k_ref, v_ref, kc_ref, vc_ref, qw_ref, kw_ref, bias_ref, seg_ref, out_ref,
               qn_s, kn_s, knc_s, *, seq, ctx_len):
    rows = seq // GRID_W
    win_rows = min(WIN_ROWS, rows)
    n_win = win_rows * GRID_W
    n_pairs = NA_HEADS // 2

    def head_rms(x, wgt):
        ss = jnp.dot((x * x).astype(BF16), seg_ref[...], preferred_element_type=F32)
        return x * lax.rsqrt(ss * (1.0 / NA_HEAD_DIM) + EPS) * wgt

    kw = kw_ref[...]
    qw = qw_ref[...] * ((NA_HEAD_DIM ** -0.5) * LOG2E)
    tile = 256

    def norm_body(t, carry):
        r0 = pl.multiple_of(t * tile, tile)
        kn_s[pl.ds(r0, tile), :] = head_rms(k_ref[0, pl.ds(r0, tile), :].astype(F32), kw).astype(BF16)
        qn_s[pl.ds(r0, tile), :] = head_rms(q_ref[0, pl.ds(r0, tile), :].astype(F32), qw).astype(BF16)
        return carry

    lax.fori_loop(0, seq // tile, norm_body, 0)
    knc_s[...] = head_rms(kc_ref[0].astype(F32), kw).astype(BF16)

    lane2 = lax.broadcasted_iota(jnp.int32, (GRID_W, LANES), 1)
    lo2 = lane2 < NA_HEAD_DIM
    ones_w = jnp.ones((n_win, LANES), BF16)
    ones_c = jnp.ones((ctx_len, LANES), BF16)
    zero = jnp.zeros((GRID_W, LANES), BF16)

    def row_body(step, carry):
        streams = [(u, j) for u in range(NA_ROW_UNROLL) for j in range(n_pairs)]
        ns = range(len(streams))
        row = [step * NA_ROW_UNROLL + u for u, _ in streams]
        r_start = [jnp.clip(r - win_rows // 2, 0, rows - win_rows) for r in row]
        case = [r_start[n] - row[n] + (WIN_ROWS - 1) for n in ns]
        q0 = [pl.multiple_of(r * GRID_W, GRID_W) for r in row]
        k0 = [pl.multiple_of(r_start[n] * GRID_W, GRID_W) for n in ns]
        lss = [slice(j * LANES, (j + 1) * LANES) for _, j in streams]
        q2 = [qn_s[pl.ds(q0[n], GRID_W), lss[n]] for n in ns]
        qs = [jnp.concatenate([jnp.where(lo2, q2[n], zero), jnp.where(lo2, zero, q2[n])], axis=0)
              for n in ns]
        s_w = [_mm_nt(qs[n], kn_s[pl.ds(k0[n], n_win), lss[n]]) + bias_ref[case[n], streams[n][1]]
               for n in ns]
        s_c = [_mm_nt(qs[n], knc_s[:, lss[n]]) for n in ns]
        mx = [s_c[n][:, 0:LANES] for n in ns]
        for n in ns:
            for t in range(1, ctx_len // LANES):
                mx[n] = jnp.maximum(mx[n], s_c[n][:, t * LANES:(t + 1) * LANES])
            for t in range(n_win // LANES):
                mx[n] = jnp.maximum(mx[n], s_w[n][:, t * LANES:(t + 1) * LANES])
        mx = [jnp.max(mx[n], axis=-1, keepdims=True) for n in ns]
        p_w = [jnp.exp2(s_w[n] - mx[n]).astype(BF16) for n in ns]
        p_c = [jnp.exp2(s_c[n] - mx[n]).astype(BF16) for n in ns]
        o = [_mm(p_w[n], jnp.concatenate([v_ref[0, pl.ds(k0[n], n_win), lss[n]], ones_w], axis=1))
             + _mm(p_c[n], jnp.concatenate([vc_ref[0, :, lss[n]], ones_c], axis=1)) for n in ns]
        for n in ns:
            on = o[n][:, :LANES] * (1.0 / o[n][:, LANES:])
            out_ref[0, pl.ds(q0[n], GRID_W), lss[n]] = jnp.where(lo2, on[:GRID_W], on[GRID_W:]).astype(out_ref.dtype)
        return carry

    lax.fori_loop(0, rows // NA_ROW_UNROLL, row_body, 0)


def _natten(na_x, na_c, qw, kw, bias):
    b, seq, _ = na_x.shape
    ctx_len = na_c.shape[1]
    wq = NA_HEADS * NA_HEAD_DIM
    assert seq % 256 == 0 and ctx_len % LANES == 0 and (seq // GRID_W) % NA_ROW_UNROLL == 0
    head_of = np.arange(wq) // NA_HEAD_DIM
    seg = jnp.asarray(head_of[:, None] == head_of[None, :], BF16)
    return pl.pallas_call(
        functools.partial(_na_kernel, seq=seq, ctx_len=ctx_len),
        grid=(b,),
        in_specs=[
            pl.BlockSpec((1, seq, wq), lambda bi: (bi, 0, 0)),
            pl.BlockSpec((1, seq, wq), lambda bi: (bi, 0, 1)),
            pl.BlockSpec((1, seq, wq), lambda bi: (bi, 0, 2)),
            pl.BlockSpec((1, ctx_len, wq), lambda bi: (bi, 0, 0)),
            pl.BlockSpec((1, ctx_len, wq), lambda bi: (bi, 0, 1)),
            pl.BlockSpec((1, wq), lambda bi: (0, 0)),
            pl.BlockSpec((1, wq), lambda bi: (0, 0)),
            pl.BlockSpec(bias.shape, lambda bi: (0, 0, 0, 0)),
            pl.BlockSpec((wq, wq), lambda bi: (0, 0)),
        ],
        out_specs=pl.BlockSpec((1, seq, wq), lambda bi: (bi, 0, 0)),
        out_shape=jax.ShapeDtypeStruct((b, seq, wq), BF16),
        scratch_shapes=[
            pltpu.VMEM((seq, wq), BF16),
            pltpu.VMEM((seq, wq), BF16),
            pltpu.VMEM((ctx_len, wq), BF16),
        ],
        compiler_params=pltpu.CompilerParams(
            dimension_semantics=("arbitrary",), vmem_limit_bytes=VMEM_LIMIT),
        name="natten",
    )(na_x, na_x, na_x, na_c, na_c, qw, kw, bias, seg)


def _na_bias_table(rpb, rows):
    win_rows = min(WIN_ROWS, rows)
    cols = np.arange(GRID_W)
    win_start = np.clip(cols - WIN_COLS // 2, 0, GRID_W - WIN_COLS)
    kc = cols[None, :]
    valid = (kc >= win_start[:, None]) & (kc < win_start[:, None] + WIN_COLS)
    rel = np.clip(kc - cols[:, None] + WIN_COLS - 1, 0, 2 * WIN_COLS - 2)
    onehot = (rel[None, :, :] == np.arange(2 * WIN_COLS - 1)[:, None, None]).astype(np.float32)
    g = jnp.einsum("hdr,rck->hdck", rpb * LOG2E, onehot, precision=HIGHEST)
    n_case = WIN_ROWS
    tabs = []
    for case in range(n_case):
        dr = np.clip(case + np.arange(win_rows), 0, 2 * WIN_ROWS - 2)
        t = jnp.transpose(g[:, dr], (0, 2, 1, 3))
        t = jnp.where(valid[None, :, None, :], t, -1e30)
        tabs.append(t.reshape(NA_HEADS // 2, 2 * GRID_W, win_rows * GRID_W))
    return jnp.stack(tabs, axis=0).astype(F32)


def _outffn_kernel(x_ref, dn_ref, na_ref, mod_ref, nw_ref, wo_ref, wi_ref, wf_ref, out_ref, *, d_ff, splits):
    half = dn_ref.shape[-1]
    attn = (jnp.dot(dn_ref[0], wo_ref[0:half, :], preferred_element_type=F32)
            + jnp.dot(na_ref[0], wo_ref[half:, :], preferred_element_type=F32))
    x1 = x_ref[0] + mod_ref[0, 2:3, :] * attn
    ms = jnp.mean(x1 * x1, axis=-1, keepdims=True)
    h = (x1 * lax.rsqrt(ms + EPS)) * nw_ref[...]
    hb = (h * (1.0 + mod_ref[0, 4:5, :]) + mod_ref[0, 3:4, :]).astype(BF16)
    acc = None
    off = 0
    for n in splits:
        gate = jnp.dot(hb, wi_ref[:, off:off + n], preferred_element_type=F32)
        up = jnp.dot(hb, wi_ref[:, d_ff + off:d_ff + off + n], preferred_element_type=F32)
        part = jnp.dot((_silu(gate) * up).astype(BF16), wf_ref[off:off + n, :], preferred_element_type=F32)
        acc = part if acc is None else acc + part
        off += n
    out_ref[0] = x1 + mod_ref[0, 5:6, :] * acc


def _out_ffn(x, dn, na, mods, norm_w, w_out, w_ffn_in, w_ffn_out, *, tm):
    b, l, d = x.shape
    half = dn.shape[-1]
    d_ff = w_ffn_out.shape[0]
    assert sum(FFN_SPLITS) == d_ff
    tm = min(tm, l)
    row_map = lambda bi, i: (bi, i, 0)
    const = lambda bi, i: (0, 0)
    single = pl.Buffered(1)
    return pl.pallas_call(
        functools.partial(_outffn_kernel, d_ff=d_ff, splits=FFN_SPLITS),
        grid=(b, l // tm),
        in_specs=[
            pl.BlockSpec((1, tm, d), row_map),
            pl.BlockSpec((1, tm, half), row_map),
            pl.BlockSpec((1, tm, half), row_map),
            pl.BlockSpec((1, 6, d), lambda bi, i: (bi, 0, 0)),
            pl.BlockSpec((1, d), const),
            pl.BlockSpec(w_out.shape, const, pipeline_mode=single),
            pl.BlockSpec(w_ffn_in.shape, const, pipeline_mode=single),
            pl.BlockSpec(w_ffn_out.shape, const, pipeline_mode=single),
        ],
        out_specs=pl.BlockSpec((1, tm, d), row_map),
        out_shape=jax.ShapeDtypeStruct((b, l, d), F32),
        compiler_params=pltpu.CompilerParams(
            dimension_semantics=("arbitrary", "arbitrary"), vmem_limit_bytes=VMEM_LIMIT),
        name="out_ffn",
    )(x, dn, na, mods, norm_w, w_out, w_ffn_in, w_ffn_out)


def kernel(x, c, ctx, c_ctx, norm1_w, norm2_w, w_ada, b_ada, w_in, dn_conv_w, dn_A_log, dn_dt_bias,
           dn_out_norm_w, na_q_norm_w, na_k_norm_w, na_rpb, w_out, w_ffn_in, w_ffn_out):
    assert w_in.shape[0] == 1, "single-layer problem"
    b, seq, d = x.shape
    hb = DN_HB
    ng = DN_HEADS // hb
    dn_w = DN_HEADS * DN_HEAD_DIM
    na_w = NA_HEADS * NA_HEAD_DIM
    rows = seq // GRID_W

    n_mod_rows = -(-(b + 1) // 8) * 8
    c_all = jnp.zeros((n_mod_rows, d), F32).at[:b].set(c).at[b].set(c_ctx)
    mod = _adaln(c_all, w_ada[0], b_ada[0][None, :])
    mod_x = mod[:b].reshape(b, 6, d)
    mod_c = mod[b:b + 1].reshape(1, 6, d)

    w0 = w_in[0]
    gate0 = 4 * dn_w
    na0 = gate0 + 4 * DN_HEADS
    gate_cols = w0[:, gate0:na0].reshape(d, 4, ng, hb)
    gate_cols = jnp.transpose(gate_cols, (0, 2, 1, 3)).reshape(d, ng, 4 * hb)
    gate_cols = jnp.pad(gate_cols, ((0, 0), (0, 0), (0, LANES - 4 * hb))).reshape(d, ng * LANES)
    w_lat = jnp.concatenate([w0[:, :gate0], w0[:, na0:], gate_cols], axis=1).astype(BF16)
    w_ctx = jnp.concatenate([w0[:, :3 * dn_w], w0[:, na0 + na_w:], gate_cols], axis=1).astype(BF16)
    n_gate = ng * LANES
    nw1 = norm1_w[0][None, :]
    qkv_x, z_x, na_x, ab_x = _in_proj(
        x, mod_x, nw1, w_lat, per_batch_mod=True, tm=IN_ROW_TILE,
        groups=((3 * dn_w, BF16), (dn_w, BF16), (3 * na_w, BF16), (n_gate, F32)))
    qkv_c, na_c, ab_c = _in_proj(
        ctx, mod_c, nw1, w_ctx, per_batch_mod=False, tm=IN_ROW_TILE,
        groups=((3 * dn_w, BF16), (2 * na_w, BF16), (n_gate, F32)))

    def group_params(p):
        return jnp.transpose(p[0].reshape(2, ng, hb), (1, 0, 2)).reshape(ng, 2 * hb)

    pg = jnp.stack([group_params(dn_A_log), group_params(dn_dt_bias)], axis=1)
    pcol = jnp.pad(pg, ((0, 0), (0, 0), (0, LANES - 2 * hb)))
    gcol_x = ab_x.reshape(b, seq // CHUNK, CHUNK, ng * LANES)
    gcol_c = ab_c.reshape(b, ctx.shape[1] // CHUNK, CHUNK, ng * LANES)
    dn_x = _deltanet(qkv_x, qkv_c, z_x, gcol_x, gcol_c, dn_conv_w[0], pcol,
                     dn_out_norm_w[0][None, :], hb=hb)

    bias = _na_bias_table(na_rpb[0], rows)
    qw = jnp.tile(na_q_norm_w[0], NA_HEADS)[None, :]
    kw = jnp.tile(na_k_norm_w[0], NA_HEADS)[None, :]
    na_o = _natten(na_x, na_c, qw, kw, bias)

    return _out_ffn(x, dn_x, na_o, mod_x, norm2_w[0][None, :], w_out[0].astype(BF16),
                    w_ffn_in[0].astype(BF16), w_ffn_out[0].astype(BF16), tm=ROW_TILE)
```

```python
import functools

import numpy as np
import jax
import jax.numpy as jnp
from jax import lax
from jax.experimental import pallas as pl
from jax.experimental.pallas import tpu as pltpu

F32 = jnp.float32
BF16 = jnp.bfloat16
HIGHEST = lax.Precision.HIGHEST

EPS = 1e-6
LOG2E = 1.4426950408889634
CHUNK = 64
CONV_K = 5
DN_HEAD_DIM = 128
DN_HEADS = 4
NA_HEAD_DIM = 64
NA_HEADS = 8
GRID_W = 64
WIN_ROWS = 8
WIN_COLS = 16
LANES = 128
VMEM_LIMIT = 56 * 1024 * 1024

DN_HB = 4
CONV_TILE = 128
CONV_HALO = 16
GATE_UNROLL = 8
NA_ROW_UNROLL = 4
IN_ROW_TILE = 1024
ROW_TILE = 512
FFN_SPLITS = (1024, 1024, 768)


def _silu(x):
    return x * (1.0 / (1.0 + jnp.exp(-x)))


def _softplus(x):
    return jnp.maximum(x, 0.0) + jnp.log(1.0 + jnp.exp(-jnp.abs(x)))


def _mm(a, b):
    return jnp.dot(a.astype(BF16), b.astype(BF16), preferred_element_type=F32)


def _mm_nt(a, b):
    return lax.dot_general(a.astype(BF16), b.astype(BF16), (((1,), (1,)), ((), ())),
                           preferred_element_type=F32)


def _mm_tn(a, b):
    return lax.dot_general(a.astype(BF16), b.astype(BF16), (((0,), (0,)), ((), ())),
                           preferred_element_type=F32)


def _mm_f32(a, b):
    return lax.dot_general(a, b, (((1,), (0,)), ((), ())), precision=HIGHEST,
                           preferred_element_type=F32)


def _adaln_kernel(c_ref, w_ref, b_ref, o_ref):
    o_ref[...] = _mm_f32(_silu(c_ref[...]), w_ref[...]) + b_ref[...]


def _adaln(c_all, w_ada, b_ada):
    rows, d = c_all.shape
    n = w_ada.shape[1]
    tn = 1024
    return pl.pallas_call(
        _adaln_kernel,
        grid=(n // tn,),
        in_specs=[
            pl.BlockSpec((rows, d), lambda j: (0, 0)),
            pl.BlockSpec((d, tn), lambda j: (0, j)),
            pl.BlockSpec((1, tn), lambda j: (0, j)),
        ],
        out_specs=pl.BlockSpec((rows, tn), lambda j: (0, j)),
        out_shape=jax.ShapeDtypeStruct((rows, n), F32),
        compiler_params=pltpu.CompilerParams(
            dimension_semantics=("arbitrary",), vmem_limit_bytes=VMEM_LIMIT),
        name="adaln",
    )(c_all, w_ada, b_ada)


def _inproj_kernel(x_ref, mod_ref, nw_ref, w_ref, *out_refs, widths):
    x = x_ref[0]
    ms = jnp.mean(x * x, axis=-1, keepdims=True)
    h = (x * lax.rsqrt(ms + EPS)) * nw_ref[...]
    h = h * (1.0 + mod_ref[0, 1:2, :]) + mod_ref[0, 0:1, :]
    hb = h.astype(BF16)
    off = 0
    for o_ref, n in zip(out_refs, widths):
        o_ref[0] = jnp.dot(hb, w_ref[:, off:off + n], preferred_element_type=F32).astype(o_ref.dtype)
        off += n


def _in_proj(x, mods, norm_w, w_cols, *, groups, per_batch_mod, tm):
    b, l, d = x.shape
    widths = tuple(n for n, _ in groups)
    assert sum(widths) == w_cols.shape[1]
    tm = min(tm, l)
    mod_map = (lambda bi, i: (bi, 0, 0)) if per_batch_mod else (lambda bi, i: (0, 0, 0))
    row_map = lambda bi, i: (bi, i, 0)
    return pl.pallas_call(
        functools.partial(_inproj_kernel, widths=widths),
        grid=(b, l // tm),
        in_specs=[
            pl.BlockSpec((1, tm, d), row_map),
            pl.BlockSpec((1, 6, d), mod_map),
            pl.BlockSpec((1, d), lambda bi, i: (0, 0)),
            pl.BlockSpec(w_cols.shape, lambda bi, i: (0, 0)),
        ],
        out_specs=[pl.BlockSpec((1, tm, n), row_map) for n in widths],
        out_shape=[jax.ShapeDtypeStruct((b, l, n), dt) for n, dt in groups],
        compiler_params=pltpu.CompilerParams(
            dimension_semantics=("arbitrary", "arbitrary"), vmem_limit_bytes=VMEM_LIMIT),
        name="in_proj",
    )(x, mods, norm_w, w_cols)


def _dn_kernel(qx_ref, kx_ref, vx_ref, qc_ref, kc_ref, vc_ref, z_ref,
               gcx_ref, gcc_ref,
               cwq_ref, cwk_ref, cwv_ref, pc_ref, onw_ref, sh_ref,
               out_ref,
               q_s, k_s, v_s, gc_s, gr_s, s_s, o_s, *, hb, seq, ctx_len):
    n_ctx = ctx_len // CHUNK
    n_lat = seq // CHUNK
    tile = CONV_TILE
    win = tile + 2 * CONV_HALO
    side_taps = [j for j in range(CONV_K) if j != CONV_K // 2]

    streams = ((cwq_ref, q_s, "q"), (cwk_ref, k_s, "k"), (cwv_ref, v_s, "v"))

    def conv_tiles(srcs, dst_off, r0, a0, sh):
        zs = [jnp.dot(sh, src[0, pl.ds(a0, win), :], preferred_element_type=F32) for src in srcs]
        ys = []
        for src, z, (cw_ref, _, _) in zip(srcs, zs, streams):
            cw = cw_ref[...]
            acc = cw[CONV_K // 2:CONV_K // 2 + 1, :] * src[0, pl.ds(r0, tile), :].astype(F32)
            for n, j in enumerate(side_taps):
                acc = acc + cw[j:j + 1, :] * z[n * tile:(n + 1) * tile, :]
            ys.append(_silu(acc))
        for y, (_, dst_ref, mode) in zip(ys, streams):
            if mode != "v":
                parts = []
                for hh in range(hb):
                    seg = y[:, hh * DN_HEAD_DIM:(hh + 1) * DN_HEAD_DIM]
                    inv = lax.rsqrt(jnp.sum(seg * seg, axis=-1, keepdims=True) + EPS)
                    if mode == "q":
                        inv = inv * (DN_HEAD_DIM ** -0.5)
                    parts.append(seg * inv)
                y = parts[0] if hb == 1 else jnp.concatenate(parts, axis=1)
            dst_ref[pl.ds(dst_off + r0, tile), :] = y

    def conv_into(srcs, n_rows, dst_off):
        n_tiles = n_rows // tile
        conv_tiles(srcs, dst_off, 0, 0, sh_ref[0])
        if n_tiles > 2:
            def body(t, carry):
                r0 = pl.multiple_of(t * tile, tile)
                conv_tiles(srcs, dst_off, r0, pl.multiple_of(r0 - CONV_HALO, CONV_HALO), sh_ref[1])
                return carry

            lax.fori_loop(1, n_tiles - 1, body, 0)
        conv_tiles(srcs, dst_off, n_rows - tile, n_rows - win, sh_ref[2])

    conv_into((qc_ref, kc_ref, vc_ref), ctx_len, 0)
    conv_into((qx_ref, kx_ref, vx_ref), seq, ctx_len)

    n_hp = hb // 2
    ri = lax.broadcasted_iota(jnp.int32, (CHUNK, CHUNK), 0)
    ci = lax.broadcasted_iota(jnp.int32, (CHUNK, CHUNK), 1)
    lower = jnp.where(ci <= ri, 1.0, 0.0).astype(BF16)
    upper = jnp.where(ci >= ri, 1.0, 0.0).astype(BF16)
    tri_c = jnp.concatenate([lower, upper], axis=1)
    a_log_c, dt_c = pc_ref[0, 0:1, :], pc_ref[0, 1:2, :]
    lane = lax.broadcasted_iota(jnp.int32, (CHUNK, LANES), 1)
    pr = lax.broadcasted_iota(jnp.int32, (8, LANES), 0)
    pl_ = lax.broadcasted_iota(jnp.int32, (8, LANES), 1)
    pick = jnp.where((pr < 2 * n_hp) & (pl_ == (pr // n_hp) * hb + 2 * (pr % n_hp)), 1.0, 0.0).astype(BF16)

    def split3(x):
        x1 = x.astype(BF16)
        r1 = x - x1.astype(F32)
        x2 = r1.astype(BF16)
        x3 = (r1 - x2.astype(F32)).astype(BF16)
        return x1, x2, x3

    def gates_into(gc_ref, n_chunks, off):
        unroll = min(GATE_UNROLL, n_chunks)

        def body(step, carry):
            ts = [step * unroll + u for u in range(unroll)]
            xc = [gc_ref[0, t] for t in ts]
            val = [jnp.where(lane < 2 * hb, -jnp.exp(a_log_c) * _softplus(x + dt_c), 1.0 / (1.0 + jnp.exp(-x)))
                   for x in xc]
            gcs = [split3(jnp.concatenate([jnp.where(lane < hb, v, 0.0),
                                           jnp.where((lane >= hb) & (lane < 2 * hb), v, 0.0)], axis=0))
                   for v in val]
            gam_c = [sum(jnp.dot(tri_c, p, preferred_element_type=F32) for p in ps) for ps in gcs]
            both = [split3(jnp.concatenate([g, pltpu.roll(g, LANES - 1, 1)], axis=0)) for g in gam_c]
            gam_r = [sum(lax.dot_general(pick, p, (((1,), (1,)), ((), ())), preferred_element_type=F32)
                         for p in ps) for ps in both]
            for u, t in enumerate(ts):
                gc_s[off + t] = jnp.where(lane < 2 * hb, gam_c[u], val[u])
                gr_s[off + t] = gam_r[u]
            return carry

        lax.fori_loop(0, n_chunks // unroll, body, 0)

    gates_into(gcc_ref, n_ctx, 0)
    gates_into(gcx_ref, n_lat, n_ctx)

    r128 = lax.broadcasted_iota(jnp.int32, (DN_HEAD_DIM, DN_HEAD_DIM), 0)
    c128 = lax.broadcasted_iota(jnp.int32, (DN_HEAD_DIM, DN_HEAD_DIM), 1)
    eye128 = (r128 == c128).astype(F32)
    onw = onw_ref[...]
    prow = lax.broadcasted_iota(jnp.int32, (CHUNK, LANES), 0)
    pcol = lane % CHUNK
    lo = lane < CHUNK
    eye_p = jnp.where(pcol == prow, 1.0, 0.0)
    before_eq = (pcol <= prow, pcol >= prow)
    strict = (pcol < prow, pcol > prow)
    zeros_w = jnp.zeros((CHUNK, 2 * DN_HEAD_DIM), BF16)

    def block_diag(x):
        xb = x.astype(BF16)
        zero = jnp.zeros_like(xb)
        return jnp.concatenate([jnp.where(lo, xb, zero), jnp.where(lo, zero, xb)], axis=0)

    def chunk_steps(groups, out_mode):
        pairs = [p for grp in groups for p in grp]
        rng = range(len(pairs))
        r0s = [pl.multiple_of(chunk * CHUNK, CHUNK) for _, _, chunk in pairs]
        wide = [slice(hp * 2 * DN_HEAD_DIM, (hp + 1) * 2 * DN_HEAD_DIM) for _, hp, _ in pairs]
        q = [q_s[pl.ds(r0s[i], CHUNK), wide[i]] for i in rng]
        k = [k_s[pl.ds(r0s[i], CHUNK), wide[i]] for i in rng]
        v = [v_s[pl.ds(r0s[i], CHUNK), wide[i]] for i in rng]
        gcol = [gc_s[pairs[i][2]] for i in rng]
        grow = [gr_s[pairs[i][2]] for i in rng]
        cs = [[d * hb + 2 * hp + e for e in range(2)] for d, hp, _ in pairs]
        dirs = [d for d, _, _ in pairs]
        gam_c = [[gcol[i][:, c:c + 1] for c in cs[i]] for i in rng]
        beta_c = [[gcol[i][:, 2 * hb + c:2 * hb + c + 1] for c in cs[i]] for i in rng]
        g_tot = [[g[CHUNK - 1:CHUNK, :] if dirs[i] == 0 else g[0:1, :] for g in gam_c[i]] for i in rng]
        gam_cp = [jnp.where(lo, gam_c[i][0], gam_c[i][1]) for i in rng]
        beta_cp = [jnp.where(lo, beta_c[i][0], beta_c[i][1]) for i in rng]
        gam_rp = [grow[i][dirs[i] * n_hp + pairs[i][1]:dirs[i] * n_hp + pairs[i][1] + 1, :] for i in rng]
        decay = [jnp.exp(jnp.where(before_eq[dirs[i]], gam_cp[i] - gam_rp[i], -jnp.inf)) for i in rng]
        kb = [k[i].astype(BF16) for i in rng]
        k_bd = [jnp.concatenate([jnp.concatenate([kb[i][:, :DN_HEAD_DIM], zeros_w[:, :DN_HEAD_DIM]], axis=1),
                                 jnp.concatenate([zeros_w[:, :DN_HEAD_DIM], kb[i][:, DN_HEAD_DIM:]], axis=1)],
                                axis=0) for i in rng]
        qk_kk = [_mm_nt(jnp.concatenate([q[i], k[i]], axis=0), k_bd[i]) for i in rng]
        qkd = [qk_kk[i][:CHUNK] * decay[i] for i in rng]
        m = [jnp.where(strict[dirs[i]], qk_kk[i][CHUNK:] * beta_cp[i] * decay[i], 0.0) for i in rng]
        t_inv = [eye_p - m[i] for i in rng]
        pw = [_mm(m[i], block_diag(m[i])) for i in rng]
        for _ in range(4):
            res = [_mm(jnp.concatenate([pw[i], t_inv[i]], axis=0), block_diag(pw[i])) for i in rng]
            pw = [res[i][:CHUNK] for i in rng]
            t_inv = [t_inv[i] + res[i][CHUNK:] for i in rng]
        t_inv = [t_inv[i] + _mm(t_inv[i], block_diag(pw[i])) for i in rng]
        e_gam = [[jnp.exp(g) for g in gam_c[i]] for i in rng]
        hd = [slice(0, DN_HEAD_DIM), slice(DN_HEAD_DIM, 2 * DN_HEAD_DIM)]
        rhs = [[jnp.concatenate([v[i][:, hd[e]] * beta_c[i][e], k[i][:, hd[e]] * (beta_c[i][e] * e_gam[i][e])],
                                axis=1).astype(BF16) for e in range(2)] for i in rng]
        rhs_bd = [jnp.concatenate([jnp.concatenate([rhs[i][0], zeros_w], axis=1),
                                   jnp.concatenate([zeros_w, rhs[i][1]], axis=1)], axis=0) for i in rng]
        uw = [_mm(t_inv[i], rhs_bd[i]) for i in rng]
        uw_h = [[uw[i][:, e * 2 * DN_HEAD_DIM:(e + 1) * 2 * DN_HEAD_DIM] for e in range(2)] for i in rng]
        qkw = [[_mm_tn(k[i][:, hd[e]] * jnp.exp(g_tot[i][e] - gam_c[i][e]), uw_h[i][e]) for e in range(2)]
               for i in rng]
        p_mat = [[jnp.exp(g_tot[i][e]) * eye128 - qkw[i][e][:, DN_HEAD_DIM:] for e in range(2)] for i in rng]
        if out_mode is not None:
            uw_bd = [jnp.concatenate([jnp.concatenate([uw_h[i][0].astype(BF16), zeros_w], axis=1),
                                      jnp.concatenate([zeros_w, uw_h[i][1].astype(BF16)], axis=1)], axis=0)
                     for i in rng]
            ow = [_mm(qkd[i], uw_bd[i]) for i in rng]
            ow_h = [[ow[i][:, e * 2 * DN_HEAD_DIM:(e + 1) * 2 * DN_HEAD_DIM] for e in range(2)] for i in rng]
            r_mat = [[q[i][:, hd[e]] * e_gam[i][e] - ow_h[i][e][:, DN_HEAD_DIM:] for e in range(2)] for i in rng]

        state = {}
        pos = 0
        for grp in groups:
            idx = range(pos, pos + len(grp))
            pos += len(grp)
            ch = [(i, e) for i in idx for e in range(2)]
            s_old = [state[cs[i][e]] if cs[i][e] in state else s_s[cs[i][e]] for i, e in ch]
            s_new = [_mm(p_mat[i][e], s_old[n]) + qkw[i][e][:, :DN_HEAD_DIM] for n, (i, e) in enumerate(ch)]
            for n, (i, e) in enumerate(ch):
                state[cs[i][e]] = s_new[n]
            if out_mode is None:
                continue
            o = [_mm(r_mat[i][e], s_old[n]) + ow_h[i][e][:, :DN_HEAD_DIM] for n, (i, e) in enumerate(ch)]
            for n, (i, e) in enumerate(ch):
                rl = pl.multiple_of(r0s[i] - ctx_len, CHUNK)
                hh = 2 * pairs[i][1] + e
                hs = slice(hh * DN_HEAD_DIM, (hh + 1) * DN_HEAD_DIM)
                if out_mode == "store":
                    o_s[pl.ds(rl, CHUNK), hs] = o[n]
                else:
                    tot = o[n] + o_s[pl.ds(rl, CHUNK), hs]
                    y = tot * lax.rsqrt(jnp.mean(tot * tot, axis=-1, keepdims=True) + EPS) * onw
                    zz = z_ref[0, pl.ds(rl, CHUNK), hs].astype(F32)
                    out_ref[0, pl.ds(rl, CHUNK), hs] = (y * _silu(zz)).astype(out_ref.dtype)
        for c, val in state.items():
            s_s[c] = val

    s_s[...] = jnp.zeros(s_s.shape, F32)
    head_pairs = range(n_hp)

    def groups_at(first, last, i):
        return [(0, hp, first + i) for hp in head_pairs] + [(1, hp, last - i) for hp in head_pairs]

    def ctx_body(j, carry):
        chunk_steps([groups_at(0, n_ctx - 1, 2 * j), groups_at(0, n_ctx - 1, 2 * j + 1)], None)
        return carry

    lax.fori_loop(0, n_ctx // 2, ctx_body, 0)

    def lat_body(mode, j, carry):
        lo, hi = n_ctx, n_ctx + n_lat - 1
        chunk_steps([groups_at(lo, hi, 2 * j), groups_at(lo, hi, 2 * j + 1)], mode)
        return carry

    lax.fori_loop(0, n_lat // 4, functools.partial(lat_body, "store"), 0)
    lax.fori_loop(n_lat // 4, n_lat // 2, functools.partial(lat_body, "final"), 0)


def _conv_shift_matrices():
    side = [j - CONV_K // 2 for j in range(CONV_K) if j != CONV_K // 2]
    win = CONV_TILE + 2 * CONV_HALO
    sh = np.zeros((3, len(side) * CONV_TILE, win), np.float32)
    for d in range(3):
        for n, off in enumerate(side):
            for i in range(CONV_TILE):
                m = i + off + d * CONV_HALO
                if 0 <= m < win:
                    sh[d, n * CONV_TILE + i, m] = 1.0
    return jnp.asarray(sh, BF16)


def _deltanet(qkv_x, qkv_c, z_x, gcol_x, gcol_c, conv_w, pcol, onw, *, hb):
    b, seq, _ = qkv_x.shape
    ctx_len = qkv_c.shape[1]
    ng = DN_HEADS // hb
    w = hb * DN_HEAD_DIM
    n_lat, n_ctx = seq // CHUNK, ctx_len // CHUNK
    assert n_lat % 4 == 0 and n_ctx % 2 == 0 and seq >= 2 * CONV_TILE and ctx_len >= 2 * CONV_TILE
    assert n_lat % GATE_UNROLL == 0 and GATE_UNROLL % n_ctx == 0 and hb % 2 == 0 and hb <= 8
    sh = _conv_shift_matrices()

    def col(part):
        return lambda bi, g: (bi, 0, part * ng + g)

    def cwcol(part):
        return lambda bi, g: (0, part * ng + g)

    return pl.pallas_call(
        functools.partial(_dn_kernel, hb=hb, seq=seq, ctx_len=ctx_len),
        grid=(b, ng),
        in_specs=[
            pl.BlockSpec((1, seq, w), col(0)),
            pl.BlockSpec((1, seq, w), col(1)),
            pl.BlockSpec((1, seq, w), col(2)),
            pl.BlockSpec((1, ctx_len, w), col(0)),
            pl.BlockSpec((1, ctx_len, w), col(1)),
            pl.BlockSpec((1, ctx_len, w), col(2)),
            pl.BlockSpec((1, seq, w), lambda bi, g: (bi, 0, g)),
            pl.BlockSpec((1, n_lat, CHUNK, LANES), lambda bi, g: (bi, 0, 0, g)),
            pl.BlockSpec((1, n_ctx, CHUNK, LANES), lambda bi, g: (bi, 0, 0, g)),
            pl.BlockSpec((CONV_K, w), cwcol(0)),
            pl.BlockSpec((CONV_K, w), cwcol(1)),
            pl.BlockSpec((CONV_K, w), cwcol(2)),
            pl.BlockSpec((1, 2, LANES), lambda bi, g: (g, 0, 0)),
            pl.BlockSpec((1, DN_HEAD_DIM), lambda bi, g: (0, 0)),
            pl.BlockSpec(sh.shape, lambda bi, g: (0, 0, 0)),
        ],
        out_specs=pl.BlockSpec((1, seq, w), lambda bi, g: (bi, 0, g)),
        out_shape=jax.ShapeDtypeStruct((b, seq, DN_HEADS * DN_HEAD_DIM), BF16),
        scratch_shapes=[
            pltpu.VMEM((ctx_len + seq, w), F32),
            pltpu.VMEM((ctx_len + seq, w), F32),
            pltpu.VMEM((ctx_len + seq, w), F32),
            pltpu.VMEM((n_ctx + n_lat, CHUNK, LANES), F32),
            pltpu.VMEM((n_ctx + n_lat, 8, LANES), F32),
            pltpu.VMEM((2 * hb, DN_HEAD_DIM, DN_HEAD_DIM), F32),
            pltpu.VMEM((seq, w), F32),
        ],
        compiler_params=pltpu.CompilerParams(
            dimension_semantics=("arbitrary", "arbitrary"), vmem_limit_bytes=VMEM_LIMIT),
        name="deltanet",
    )(qkv_x, qkv_x, qkv_x, qkv_c, qkv_c, qkv_c, z_x, gcol_x, gcol_c,
      conv_w, conv_w, conv_w, pcol, onw, sh)


def _na_kernel(q_ref, k_ref, v_ref, kc_ref, vc_ref, qw_ref, kw_ref, bias_ref, seg_ref, out_ref,
               qn_s, kn_s, knc_s, *, seq, ctx_len):
    rows = seq // GRID_W
    win_rows = min(WIN_ROWS, rows)
    n_win = win_rows * GRID_W
    n_pairs = NA_HEADS // 2

    def head_rms(x, wgt):
        ss = jnp.dot((x * x).astype(BF16), seg_ref[...], preferred_element_type=F32)
        return x * lax.rsqrt(ss * (1.0 / NA_HEAD_DIM) + EPS) * wgt

    kw = kw_ref[...]
    qw = qw_ref[...] * ((NA_HEAD_DIM ** -0.5) * LOG2E)
    tile = 256

    def norm_body(t, carry):
        r0 = pl.multiple_of(t * tile, tile)
        kn_s[pl.ds(r0, tile), :] = head_rms(k_ref[0, pl.ds(r0, tile), :].astype(F32), kw).astype(BF16)
        qn_s[pl.ds(r0, tile), :] = head_rms(q_ref[0, pl.ds(r0, tile), :].astype(F32), qw).astype(BF16)
        return carry

    lax.fori_loop(0, seq // tile, norm_body, 0)
    knc_s[...] = head_rms(kc_ref[0].astype(F32), kw).astype(BF16)

    lane2 = lax.broadcasted_iota(jnp.int32, (GRID_W, LANES), 1)
    lo2 = lane2 < NA_HEAD_DIM
    ones_w = jnp.ones((n_win, LANES), BF16)
    ones_c = jnp.ones((ctx_len, LANES), BF16)
    zero = jnp.zeros((GRID_W, LANES), BF16)

    def row_body(step, carry):
        streams = [(u, j) for u in range(NA_ROW_UNROLL) for j in range(n_pairs)]
        ns = range(len(streams))
        row = [step * NA_ROW_UNROLL + u for u, _ in streams]
        r_start = [jnp.clip(r - win_rows // 2, 0, rows - win_rows) for r in row]
        case = [r_start[n] - row[n] + (WIN_ROWS - 1) for n in ns]
        q0 = [pl.multiple_of(r * GRID_W, GRID_W) for r in row]
        k0 = [pl.multiple_of(r_start[n] * GRID_W, GRID_W) for n in ns]
        lss = [slice(j * LANES, (j + 1) * LANES) for _, j in streams]
        q2 = [qn_s[pl.ds(q0[n], GRID_W), lss[n]] for n in ns]
        qs = [jnp.concatenate([jnp.where(lo2, q2[n], zero), jnp.where(lo2, zero, q2[n])], axis=0)
              for n in ns]
        s_w = [_mm_nt(qs[n], kn_s[pl.ds(k0[n], n_win), lss[n]]) + bias_ref[case[n], streams[n][1]]
               for n in ns]
        s_c = [_mm_nt(qs[n], knc_s[:, lss[n]]) for n in ns]
        mx = [s_c[n][:, 0:LANES] for n in ns]
        for n in ns:
            for t in range(1, ctx_len // LANES):
                mx[n] = jnp.maximum(mx[n], s_c[n][:, t * LANES:(t + 1) * LANES])
            for t in range(n_win // LANES):
                mx[n] = jnp.maximum(mx[n], s_w[n][:, t * LANES:(t + 1) * LANES])
        mx = [jnp.max(mx[n], axis=-1, keepdims=True) for n in ns]
        p_w = [jnp.exp2(s_w[n] - mx[n]).astype(BF16) for n in ns]
        p_c = [jnp.exp2(s_c[n] - mx[n]).astype(BF16) for n in ns]
        o = [_mm(p_w[n], jnp.concatenate([v_ref[0, pl.ds(k0[n], n_win), lss[n]], ones_w], axis=1))
             + _mm(p_c[n], jnp.concatenate([vc_ref[0, :, lss[n]], ones_c], axis=1)) for n in ns]
        for n in ns:
            on = o[n][:, :LANES] * (1.0 / o[n][:, LANES:])
            out_ref[0, pl.ds(q0[n], GRID_W), lss[n]] = jnp.where(lo2, on[:GRID_W], on[GRID_W:]).astype(out_ref.dtype)
        return carry

    lax.fori_loop(0, rows // NA_ROW_UNROLL, row_body, 0)


def _natten(na_x, na_c, qw, kw, bias):
    b, seq, _ = na_x.shape
    ctx_len = na_c.shape[1]
    wq = NA_HEADS * NA_HEAD_DIM
    assert seq % 256 == 0 and ctx_len % LANES == 0 and (seq // GRID_W) % NA_ROW_UNROLL == 0
    head_of = np.arange(wq) // NA_HEAD_DIM
    seg = jnp.asarray(head_of[:, None] == head_of[None, :], BF16)
    return pl.pallas_call(
        functools.partial(_na_kernel, seq=seq, ctx_len=ctx_len),
        grid=(b,),
        in_specs=[
            pl.BlockSpec((1, seq, wq), lambda bi: (bi, 0, 0)),
            pl.BlockSpec((1, seq, wq), lambda bi: (bi, 0, 1)),
            pl.BlockSpec((1, seq, wq), lambda bi: (bi, 0, 2)),
            pl.BlockSpec((1, ctx_len, wq), lambda bi: (bi, 0, 0)),
            pl.BlockSpec((1, ctx_len, wq), lambda bi: (bi, 0, 1)),
            pl.BlockSpec((1, wq), lambda bi: (0, 0)),
            pl.BlockSpec((1, wq), lambda bi: (0, 0)),
            pl.BlockSpec(bias.shape, lambda bi: (0, 0, 0, 0)),
            pl.BlockSpec((wq, wq), lambda bi: (0, 0)),
        ],
        out_specs=pl.BlockSpec((1, seq, wq), lambda bi: (bi, 0, 0)),
        out_shape=jax.ShapeDtypeStruct((b, seq, wq), BF16),
        scratch_shapes=[
            pltpu.VMEM((seq, wq), BF16),
            pltpu.VMEM((seq, wq), BF16),
            pltpu.VMEM((ctx_len, wq), BF16),
        ],
        compiler_params=pltpu.CompilerParams(
            dimension_semantics=("arbitrary",), vmem_limit_bytes=VMEM_LIMIT),
        name="natten",
    )(na_x, na_x, na_x, na_c, na_c, qw, kw, bias, seg)


def _na_bias_table(rpb, rows):
    win_rows = min(WIN_ROWS, rows)
    cols = np.arange(GRID_W)
    win_start = np.clip(cols - WIN_COLS // 2, 0, GRID_W - WIN_COLS)
    kc = cols[None, :]
    valid = (kc >= win_start[:, None]) & (kc < win_start[:, None] + WIN_COLS)
    rel = np.clip(kc - cols[:, None] + WIN_COLS - 1, 0, 2 * WIN_COLS - 2)
    onehot = (rel[None, :, :] == np.arange(2 * WIN_COLS - 1)[:, None, None]).astype(np.float32)
    g = jnp.einsum("hdr,rck->hdck", rpb * LOG2E, onehot, precision=HIGHEST)
    n_case = WIN_ROWS
    tabs = []
    for case in range(n_case):
        dr = np.clip(case + np.arange(win_rows), 0, 2 * WIN_ROWS - 2)
        t = jnp.transpose(g[:, dr], (0, 2, 1, 3))
        t = jnp.where(valid[None, :, None, :], t, -1e30)
        tabs.append(t.reshape(NA_HEADS // 2, 2 * GRID_W, win_rows * GRID_W))
    return jnp.stack(tabs, axis=0).astype(F32)


def _outffn_kernel(x_ref, dn_ref, na_ref, mod_ref, nw_ref, wo_ref, wi_ref, wf_ref, out_ref, *, d_ff, splits):
    half = dn_ref.shape[-1]
    attn = (jnp.dot(dn_ref[0], wo_ref[0:half, :], preferred_element_type=F32)
            + jnp.dot(na_ref[0], wo_ref[half:, :], preferred_element_type=F32))
    x1 = x_ref[0] + mod_ref[0, 2:3, :] * attn
    ms = jnp.mean(x1 * x1, axis=-1, keepdims=True)
    h = (x1 * lax.rsqrt(ms + EPS)) * nw_ref[...]
    hb = (h * (1.0 + mod_ref[0, 4:5, :]) + mod_ref[0, 3:4, :]).astype(BF16)
    acc = None
    off = 0
    for n in splits:
        gate = jnp.dot(hb, wi_ref[:, off:off + n], preferred_element_type=F32)
        up = jnp.dot(hb, wi_ref[:, d_ff + off:d_ff + off + n], preferred_element_type=F32)
        part = jnp.dot((_silu(gate) * up).astype(BF16), wf_ref[off:off + n, :], preferred_element_type=F32)
        acc = part if acc is None else acc + part
        off += n
    out_ref[0] = x1 + mod_ref[0, 5:6, :] * acc


def _out_ffn(x, dn, na, mods, norm_w, w_out, w_ffn_in, w_ffn_out, *, tm):
    b, l, d = x.shape
    half = dn.shape[-1]
    d_ff = w_ffn_out.shape[0]
    assert sum(FFN_SPLITS) == d_ff
    tm = min(tm, l)
    row_map = lambda bi, i: (bi, i, 0)
    const = lambda bi, i: (0, 0)
    single = pl.Buffered(1)
    return pl.pallas_call(
        functools.partial(_outffn_kernel, d_ff=d_ff, splits=FFN_SPLITS),
        grid=(b, l // tm),
        in_specs=[
            pl.BlockSpec((1, tm, d), row_map),
            pl.BlockSpec((1, tm, half), row_map),
            pl.BlockSpec((1, tm, half), row_map),
            pl.BlockSpec((1, 6, d), lambda bi, i: (bi, 0, 0)),
            pl.BlockSpec((1, d), const),
            pl.BlockSpec(w_out.shape, const, pipeline_mode=single),
            pl.BlockSpec(w_ffn_in.shape, const, pipeline_mode=single),
            pl.BlockSpec(w_ffn_out.shape, const, pipeline_mode=single),
        ],
        out_specs=pl.BlockSpec((1, tm, d), row_map),
        out_shape=jax.ShapeDtypeStruct((b, l, d), F32),
        compiler_params=pltpu.CompilerParams(
            dimension_semantics=("arbitrary", "arbitrary"), vmem_limit_bytes=VMEM_LIMIT),
        name="out_ffn",
    )(x, dn, na, mods, norm_w, w_out, w_ffn_in, w_ffn_out)


def kernel(x, c, ctx, c_ctx, norm1_w, norm2_w, w_ada, b_ada, w_in, dn_conv_w, dn_A_log, dn_dt_bias,
           dn_out_norm_w, na_q_norm_w, na_k_norm_w, na_rpb, w_out, w_ffn_in, w_ffn_out):
    assert w_in.shape[0] == 1, "single-layer problem"
    b, seq, d = x.shape
    hb = DN_HB
    ng = DN_HEADS // hb
    dn_w = DN_HEADS * DN_HEAD_DIM
    na_w = NA_HEADS * NA_HEAD_DIM
    rows = seq // GRID_W

    n_mod_rows = -(-(b + 1) // 8) * 8
    c_all = jnp.zeros((n_mod_rows, d), F32).at[:b].set(c).at[b].set(c_ctx)
    mod = _adaln(c_all, w_ada[0], b_ada[0][None, :])
    mod_x = mod[:b].reshape(b, 6, d)
    mod_c = mod[b:b + 1].reshape(1, 6, d)

    w0 = w_in[0]
    gate0 = 4 * dn_w
    na0 = gate0 + 4 * DN_HEADS
    gate_cols = w0[:, gate0:na0].reshape(d, 4, ng, hb)
    gate_cols = jnp.transpose(gate_cols, (0, 2, 1, 3)).reshape(d, ng, 4 * hb)
    gate_cols = jnp.pad(gate_cols, ((0, 0), (0, 0), (0, LANES - 4 * hb))).reshape(d, ng * LANES)
    w_lat = jnp.concatenate([w0[:, :gate0], w0[:, na0:], gate_cols], axis=1).astype(BF16)
    w_ctx = jnp.concatenate([w0[:, :3 * dn_w], w0[:, na0 + na_w:], gate_cols], axis=1).astype(BF16)
    n_gate = ng * LANES
    nw1 = norm1_w[0][None, :]
    qkv_x, z_x, na_x, ab_x = _in_proj(
        x, mod_x, nw1, w_lat, per_batch_mod=True, tm=IN_ROW_TILE,
        groups=((3 * dn_w, BF16), (dn_w, BF16), (3 * na_w, BF16), (n_gate, F32)))
    qkv_c, na_c, ab_c = _in_proj(
        ctx, mod_c, nw1, w_ctx, per_batch_mod=False, tm=IN_ROW_TILE,
        groups=((3 * dn_w, BF16), (2 * na_w, BF16), (n_gate, F32)))

    def group_params(p):
        return jnp.transpose(p[0].reshape(2, ng, hb), (1, 0, 2)).reshape(ng, 2 * hb)

    pg = jnp.stack([group_params(dn_A_log), group_params(dn_dt_bias)], axis=1)
    pcol = jnp.pad(pg, ((0, 0), (0, 0), (0, LANES - 2 * hb)))
    gcol_x = ab_x.reshape(b, seq // CHUNK, CHUNK, ng * LANES)
    gcol_c = ab_c.reshape(b, ctx.shape[1] // CHUNK, CHUNK, ng * LANES)
    dn_x = _deltanet(qkv_x, qkv_c, z_x, gcol_x, gcol_c, dn_conv_w[0], pcol,
                     dn_out_norm_w[0][None, :], hb=hb)

    bias = _na_bias_table(na_rpb[0], rows)
    qw = jnp.tile(na_q_norm_w[0], NA_HEADS)[None, :]
    kw = jnp.tile(na_k_norm_w[0], NA_HEADS)[None, :]
    na_o = _natten(na_x, na_c, qw, kw, bias)

    return _out_ffn(x, dn_x, na_o, mod_x, norm2_w[0][None, :], w_out[0].astype(BF16),
                    w_ffn_in[0].astype(BF16), w_ffn_out[0].astype(BF16), tm=ROW_TILE)
```

```python
import functools

import numpy as np
import jax
import jax.numpy as jnp
from jax import lax
from jax.experimental import pallas as pl
from jax.experimental.pallas import tpu as pltpu

F32 = jnp.float32
BF16 = jnp.bfloat16
HIGHEST = lax.Precision.HIGHEST

EPS = 1e-6
LOG2E = 1.4426950408889634
CHUNK = 64
CONV_K = 5
DN_HEAD_DIM = 128
DN_HEADS = 4
NA_HEAD_DIM = 64
NA_HEADS = 8
GRID_W = 64
WIN_ROWS = 8
WIN_COLS = 16
LANES = 128
VMEM_LIMIT = 56 * 1024 * 1024

DN_HB = 4
CONV_TILE = 128
CONV_HALO = 16
GATE_UNROLL = 8
NA_ROW_UNROLL = 4
IN_ROW_TILE = 1024
ROW_TILE = 512
FFN_SPLITS = (1024, 1024, 768)


def _silu(x):
    return x * (1.0 / (1.0 + jnp.exp(-x)))


def _softplus(x):
    return jnp.maximum(x, 0.0) + jnp.log(1.0 + jnp.exp(-jnp.abs(x)))


def _mm(a, b):
    return jnp.dot(a.astype(BF16), b.astype(BF16), preferred_element_type=F32)


def _mm_nt(a, b):
    return lax.dot_general(a.astype(BF16), b.astype(BF16), (((1,), (1,)), ((), ())),
                           preferred_element_type=F32)


def _mm_tn(a, b):
    return lax.dot_general(a.astype(BF16), b.astype(BF16), (((0,), (0,)), ((), ())),
                           preferred_element_type=F32)


def _mm_f32(a, b):
    return lax.dot_general(a, b, (((1,), (0,)), ((), ())), precision=HIGHEST,
                           preferred_element_type=F32)


def _adaln_kernel(c_ref, w_ref, b_ref, o_ref):
    o_ref[...] = _mm_f32(_silu(c_ref[...]), w_ref[...]) + b_ref[...]


def _adaln(c_all, w_ada, b_ada):
    rows, d = c_all.shape
    n = w_ada.shape[1]
    tn = 1024
    return pl.pallas_call(
        _adaln_kernel,
        grid=(n // tn,),
        in_specs=[
            pl.BlockSpec((rows, d), lambda j: (0, 0)),
            pl.BlockSpec((d, tn), lambda j: (0, j)),
            pl.BlockSpec((1, tn), lambda j: (0, j)),
        ],
        out_specs=pl.BlockSpec((rows, tn), lambda j: (0, j)),
        out_shape=jax.ShapeDtypeStruct((rows, n), F32),
        compiler_params=pltpu.CompilerParams(
            dimension_semantics=("arbitrary",), vmem_limit_bytes=VMEM_LIMIT),
        name="adaln",
    )(c_all, w_ada, b_ada)


def _inproj_kernel(x_ref, mod_ref, nw_ref, w_ref, *out_refs, widths):
    x = x_ref[0]
    ms = jnp.mean(x * x, axis=-1, keepdims=True)
    h = (x * lax.rsqrt(ms + EPS)) * nw_ref[...]
    h = h * (1.0 + mod_ref[0, 1:2, :]) + mod_ref[0, 0:1, :]
    hb = h.astype(BF16)
    off = 0
    for o_ref, n in zip(out_refs, widths):
        o_ref[0] = jnp.dot(hb, w_ref[:, off:off + n], preferred_element_type=F32).astype(o_ref.dtype)
        off += n


def _in_proj(x, mods, norm_w, w_cols, *, groups, per_batch_mod, tm):
    b, l, d = x.shape
    widths = tuple(n for n, _ in groups)
    assert sum(widths) == w_cols.shape[1]
    tm = min(tm, l)
    mod_map = (lambda bi, i: (bi, 0, 0)) if per_batch_mod else (lambda bi, i: (0, 0, 0))
    row_map = lambda bi, i: (bi, i, 0)
    return pl.pallas_call(
        functools.partial(_inproj_kernel, widths=widths),
        grid=(b, l // tm),
        in_specs=[
            pl.BlockSpec((1, tm, d), row_map),
            pl.BlockSpec((1, 6, d), mod_map),
            pl.BlockSpec((1, d), lambda bi, i: (0, 0)),
            pl.BlockSpec(w_cols.shape, lambda bi, i: (0, 0)),
        ],
        out_specs=[pl.BlockSpec((1, tm, n), row_map) for n in widths],
        out_shape=[jax.ShapeDtypeStruct((b, l, n), dt) for n, dt in groups],
        compiler_params=pltpu.CompilerParams(
            dimension_semantics=("arbitrary", "arbitrary"), vmem_limit_bytes=VMEM_LIMIT),
        name="in_proj",
    )(x, mods, norm_w, w_cols)


def _dn_kernel(qx_ref, kx_ref, vx_ref, qc_ref, kc_ref, vc_ref, z_ref,
               gcx_ref, gcc_ref,
               cwq_ref, cwk_ref, cwv_ref, pc_ref, onw_ref, sh_ref,
               out_ref,
               q_s, k_s, v_s, gc_s, gr_s, s_s, o_s, *, hb, seq, ctx_len):
    n_ctx = ctx_len // CHUNK
    n_lat = seq // CHUNK
    tile = CONV_TILE
    win = tile + 2 * CONV_HALO
    side_taps = [j for j in range(CONV_K) if j != CONV_K // 2]

    step = pl.program_id(0)
    slot_p = step % 2
    slot_c = 1 - slot_p

    streams = ((cwq_ref, q_s, "q"), (cwk_ref, k_s, "k"), (cwv_ref, v_s, "v"))
    lat_srcs = (qx_ref, kx_ref, vx_ref)
    ctx_srcs = (qc_ref, kc_ref, vc_ref)

    def conv_stages(srcs, n_rows, dst_off, t):
        n_tiles = n_rows // tile
        if isinstance(t, int):
            r0 = t * tile
            a0 = min(max(r0 - CONV_HALO, 0), n_rows - win)
            variant = 0 if t == 0 else (2 if t == n_tiles - 1 else 1)
        else:
            r0 = pl.multiple_of(t * tile, tile)
            a0 = pl.multiple_of(jnp.clip(r0 - CONV_HALO, 0, n_rows - win), CONV_HALO)
            variant = jnp.where(t == 0, 0, jnp.where(t == n_tiles - 1, 2, 1))
        sh = sh_ref[variant]
        zs = [jnp.dot(sh, src[0, pl.ds(a0, win), :], preferred_element_type=F32) for src in srcs]
        yield
        ys = []
        for src, z, (cw_ref, _, _) in zip(srcs, zs, streams):
            cw = cw_ref[...]
            acc = cw[CONV_K // 2:CONV_K // 2 + 1, :] * src[0, pl.ds(r0, tile), :].astype(F32)
            for n, j in enumerate(side_taps):
                acc = acc + cw[j:j + 1, :] * z[n * tile:(n + 1) * tile, :]
            ys.append(_silu(acc))
            yield
        for y, (_, dst_ref, mode) in zip(ys, streams):
            if mode != "v":
                parts = []
                for hh in range(hb):
                    seg = y[:, hh * DN_HEAD_DIM:(hh + 1) * DN_HEAD_DIM]
                    inv = lax.rsqrt(jnp.sum(seg * seg, axis=-1, keepdims=True) + EPS)
                    if mode == "q":
                        inv = inv * (DN_HEAD_DIM ** -0.5)
                    parts.append(seg * inv)
                y = parts[0] if hb == 1 else jnp.concatenate(parts, axis=1)
            dst_ref[slot_p, pl.ds(dst_off + r0, tile), :] = y.astype(dst_ref.dtype)
            yield

    def conv_tile(srcs, n_rows, dst_off, t):
        for _ in conv_stages(srcs, n_rows, dst_off, t):
            pass

    @pl.when(step == 0)
    def _():
        for t in range(ctx_len // tile):
            conv_tile(ctx_srcs, ctx_len, 0, t)

        def body(t, carry):
            conv_tile(lat_srcs, seq, ctx_len, t)
            return carry

        lax.fori_loop(0, seq // tile, body, 0)

    n_hp = hb // 2
    ri = lax.broadcasted_iota(jnp.int32, (CHUNK, CHUNK), 0)
    ci = lax.broadcasted_iota(jnp.int32, (CHUNK, CHUNK), 1)
    lower = jnp.where(ci <= ri, 1.0, 0.0).astype(BF16)
    upper = jnp.where(ci >= ri, 1.0, 0.0).astype(BF16)
    tri_c = jnp.concatenate([lower, upper], axis=1)
    a_log_c, dt_c = pc_ref[0, 0:1, :], pc_ref[0, 1:2, :]
    lane = lax.broadcasted_iota(jnp.int32, (CHUNK, LANES), 1)
    pr = lax.broadcasted_iota(jnp.int32, (8, LANES), 0)
    pl_ = lax.broadcasted_iota(jnp.int32, (8, LANES), 1)
    pick = jnp.where((pr < 2 * n_hp) & (pl_ == (pr // n_hp) * hb + 2 * (pr % n_hp)), 1.0, 0.0).astype(BF16)

    def split3(x):
        x1 = x.astype(BF16)
        r1 = x - x1.astype(F32)
        x2 = r1.astype(BF16)
        x3 = (r1 - x2.astype(F32)).astype(BF16)
        return x1, x2, x3

    def gates_into(gc_ref, n_chunks, off):
        unroll = min(GATE_UNROLL, n_chunks)

        def body(it, carry):
            ts = [it * unroll + u for u in range(unroll)]
            xc = [gc_ref[0, t] for t in ts]
            val = [jnp.where(lane < 2 * hb, -jnp.exp(a_log_c) * _softplus(x + dt_c), 1.0 / (1.0 + jnp.exp(-x)))
                   for x in xc]
            gcs = [split3(jnp.concatenate([jnp.where(lane < hb, v, 0.0),
                                           jnp.where((lane >= hb) & (lane < 2 * hb), v, 0.0)], axis=0))
                   for v in val]
            gam_c = [sum(jnp.dot(tri_c, p, preferred_element_type=F32) for p in ps) for ps in gcs]
            both = [split3(jnp.concatenate([g, pltpu.roll(g, LANES - 1, 1)], axis=0)) for g in gam_c]
            gam_r = [sum(lax.dot_general(pick, p, (((1,), (1,)), ((), ())), preferred_element_type=F32)
                         for p in ps) for ps in both]
            for u, t in enumerate(ts):
                gc_s[off + t] = jnp.where(lane < 2 * hb, gam_c[u], val[u])
                gr_s[off + t] = gam_r[u]
            return carry

        lax.fori_loop(0, n_chunks // unroll, body, 0)

    onw = onw_ref[...]
    prow = lax.broadcasted_iota(jnp.int32, (CHUNK, LANES), 0)
    pcol = lane % CHUNK
    lo = lane < CHUNK
    eye_p = jnp.where(pcol == prow, 1.0, 0.0)
    before_eq = (pcol <= prow, pcol >= prow)
    strict = (pcol < prow, pcol > prow)
    zeros_w = jnp.zeros((CHUNK, 2 * DN_HEAD_DIM), BF16)

    def block_diag(x):
        xb = x.astype(BF16)
        zero = jnp.zeros_like(xb)
        return jnp.concatenate([jnp.where(lo, xb, zero), jnp.where(lo, zero, xb)], axis=0)

    def chunk_steps(groups, out_mode, side=None):
        def tick():
            if side is not None:
                next(side, None)

        pairs = [p for grp in groups for p in grp]
        rng = range(len(pairs))
        r0s = [pl.multiple_of(chunk * CHUNK, CHUNK) for _, _, chunk in pairs]
        wide = [slice(hp * 2 * DN_HEAD_DIM, (hp + 1) * 2 * DN_HEAD_DIM) for _, hp, _ in pairs]
        q = [q_s[slot_c, pl.ds(r0s[i], CHUNK), wide[i]].astype(F32) for i in rng]
        k = [k_s[slot_c, pl.ds(r0s[i], CHUNK), wide[i]].astype(F32) for i in rng]
        v = [v_s[slot_c, pl.ds(r0s[i], CHUNK), wide[i]].astype(F32) for i in rng]
        gcol = [gc_s[pairs[i][2]] for i in rng]
        grow = [gr_s[pairs[i][2]] for i in rng]
        cs = [[d * hb + 2 * hp + e for e in range(2)] for d, hp, _ in pairs]
        dirs = [d for d, _, _ in pairs]
        gam_c = [[gcol[i][:, c:c + 1] for c in cs[i]] for i in rng]
        beta_c = [[gcol[i][:, 2 * hb + c:2 * hb + c + 1] for c in cs[i]] for i in rng]
        g_tot = [[g[CHUNK - 1:CHUNK, :] if dirs[i] == 0 else g[0:1, :] for g in gam_c[i]] for i in rng]
        gam_cp = [jnp.where(lo, gam_c[i][0], gam_c[i][1]) for i in rng]
        beta_cp = [jnp.where(lo, beta_c[i][0], beta_c[i][1]) for i in rng]
        gam_rp = [grow[i][dirs[i] * n_hp + pairs[i][1]:dirs[i] * n_hp + pairs[i][1] + 1, :] for i in rng]
        decay = [jnp.exp(jnp.where(before_eq[dirs[i]], gam_cp[i] - gam_rp[i], -jnp.inf)) for i in rng]
        kb = [k[i].astype(BF16) for i in rng]
        k_bd = [jnp.concatenate([jnp.concatenate([kb[i][:, :DN_HEAD_DIM], zeros_w[:, :DN_HEAD_DIM]], axis=1),
                                 jnp.concatenate([zeros_w[:, :DN_HEAD_DIM], kb[i][:, DN_HEAD_DIM:]], axis=1)],
                                axis=0) for i in rng]
        qk_kk = [_mm_nt(jnp.concatenate([q[i], k[i]], axis=0), k_bd[i]) for i in rng]
        tick()
        qkd = [qk_kk[i][:CHUNK] * decay[i] for i in rng]
        m = [jnp.where(strict[dirs[i]], qk_kk[i][CHUNK:] * beta_cp[i] * decay[i], 0.0) for i in rng]
        t_inv = [eye_p - m[i] for i in rng]
        pw = [_mm(m[i], block_diag(m[i])) for i in rng]
        tick()
        for _ in range(4):
            res = [_mm(jnp.concatenate([pw[i], t_inv[i]], axis=0), block_diag(pw[i])) for i in rng]
            tick()
            pw = [res[i][:CHUNK] for i in rng]
            t_inv = [t_inv[i] + res[i][CHUNK:] for i in rng]
        t_inv = [t_inv[i] + _mm(t_inv[i], block_diag(pw[i])) for i in rng]
        e_gam = [[jnp.exp(g) for g in gam_c[i]] for i in rng]
        hd = [slice(0, DN_HEAD_DIM), slice(DN_HEAD_DIM, 2 * DN_HEAD_DIM)]
        rhs = [[jnp.concatenate([v[i][:, hd[e]] * beta_c[i][e], k[i][:, hd[e]] * (beta_c[i][e] * e_gam[i][e])],
                                axis=1).astype(BF16) for e in range(2)] for i in rng]
        rhs_bd = [jnp.concatenate([jnp.concatenate([rhs[i][0], zeros_w], axis=1),
                                   jnp.concatenate([zeros_w, rhs[i][1]], axis=1)], axis=0) for i in rng]
        uw = [_mm(t_inv[i], rhs_bd[i]) for i in rng]
        tick()
        u_h = [[uw[i][:, (2 * e) * DN_HEAD_DIM:(2 * e + 1) * DN_HEAD_DIM] for e in range(2)] for i in rng]
        w_h = [[uw[i][:, (2 * e + 1) * DN_HEAD_DIM:(2 * e + 2) * DN_HEAD_DIM] for e in range(2)] for i in rng]
        kdec = [[(k[i][:, hd[e]] * jnp.exp(g_tot[i][e] - gam_c[i][e])).astype(BF16) for e in range(2)] for i in rng]
        s_scale = [[jnp.exp(g) for g in g_tot[i]] for i in rng]
        if out_mode is None:
            lhs = [[w_h[i][e].astype(BF16) for e in range(2)] for i in rng]
        else:
            lhs = [[jnp.concatenate([q[i][:, hd[e]] * e_gam[i][e], w_h[i][e]], axis=0).astype(BF16)
                    for e in range(2)] for i in rng]

        state = {}
        pos = 0
        for grp in groups:
            idx = range(pos, pos + len(grp))
            pos += len(grp)
            ch = [(i, e) for i in idx for e in range(2)]
            s_old = {(i, e): (state[cs[i][e]] if cs[i][e] in state else s_s[cs[i][e]]) for i, e in ch}
            prod = {(i, e): _mm(lhs[i][e], s_old[(i, e)]) for i, e in ch}
            tick()
            v_new = {(i, e): u_h[i][e] - prod[(i, e)][-CHUNK:] for i, e in ch}
            for i, e in ch:
                state[cs[i][e]] = s_scale[i][e] * s_old[(i, e)] + _mm_tn(kdec[i][e], v_new[(i, e)])
            if out_mode is None:
                continue
            zeros_h = zeros_w[:, :DN_HEAD_DIM]
            v_bd = {i: jnp.concatenate(
                [jnp.concatenate([v_new[(i, 0)].astype(BF16), zeros_h], axis=1),
                 jnp.concatenate([zeros_h, v_new[(i, 1)].astype(BF16)], axis=1)], axis=0) for i in idx}
            intra = {i: _mm(qkd[i], v_bd[i]) for i in idx}
            for i, e in ch:
                o = prod[(i, e)][:CHUNK] + intra[i][:, hd[e]]
                rl = pl.multiple_of(r0s[i] - ctx_len, CHUNK)
                hh = 2 * pairs[i][1] + e
                hs = slice(hh * DN_HEAD_DIM, (hh + 1) * DN_HEAD_DIM)
                if out_mode == "store":
                    o_s[pl.ds(rl, CHUNK), hs] = o
                else:
                    tot = o + o_s[pl.ds(rl, CHUNK), hs]
                    y = tot * lax.rsqrt(jnp.mean(tot * tot, axis=-1, keepdims=True) + EPS) * onw
                    zz = z_ref[0, pl.ds(rl, CHUNK), hs].astype(F32)
                    out_ref[0, pl.ds(rl, CHUNK), hs] = (y * _silu(zz)).astype(out_ref.dtype)
        for c, val in state.items():
            s_s[c] = val
        if side is not None:
            for _ in side:
                pass

    head_pairs = range(n_hp)

    def groups_at(first, last, i):
        return [(0, hp, first + i) for hp in head_pairs] + [(1, hp, last - i) for hp in head_pairs]

    @pl.when(step > 0)
    def _():
        gates_into(gcc_ref, n_ctx, 0)
        gates_into(gcx_ref, n_lat, n_ctx)
        s_s[...] = jnp.zeros(s_s.shape, F32)

        for j in range(n_ctx // 2):
            chunk_steps([groups_at(0, n_ctx - 1, 2 * j), groups_at(0, n_ctx - 1, 2 * j + 1)], None,
                        side=conv_stages(ctx_srcs, ctx_len, 0, j))

        def lat_body(mode, j, carry):
            first, last = n_ctx, n_ctx + n_lat - 1
            chunk_steps([groups_at(first, last, 2 * j), groups_at(first, last, 2 * j + 1)], mode,
                        side=conv_stages(lat_srcs, seq, ctx_len, j))
            return carry

        lax.fori_loop(0, n_lat // 4, functools.partial(lat_body, "store"), 0)
        lax.fori_loop(n_lat // 4, n_lat // 2, functools.partial(lat_body, "final"), 0)


def _conv_shift_matrices():
    side = [j - CONV_K // 2 for j in range(CONV_K) if j != CONV_K // 2]
    win = CONV_TILE + 2 * CONV_HALO
    sh = np.zeros((3, len(side) * CONV_TILE, win), np.float32)
    for d in range(3):
        for n, off in enumerate(side):
            for i in range(CONV_TILE):
                m = i + off + d * CONV_HALO
                if 0 <= m < win:
                    sh[d, n * CONV_TILE + i, m] = 1.0
    return jnp.asarray(sh, BF16)


def _deltanet(qkv_x, qkv_c, z_x, gcol_x, gcol_c, conv_w, pcol, onw, *, hb):
    b, seq, _ = qkv_x.shape
    ctx_len = qkv_c.shape[1]
    ng = DN_HEADS // hb
    w = hb * DN_HEAD_DIM
    n_lat, n_ctx = seq // CHUNK, ctx_len // CHUNK
    assert n_lat % 4 == 0 and n_ctx % 2 == 0 and seq >= 2 * CONV_TILE and ctx_len >= 2 * CONV_TILE
    assert n_lat % GATE_UNROLL == 0 and GATE_UNROLL % n_ctx == 0 and hb % 2 == 0 and hb <= 8
    sh = _conv_shift_matrices()

    assert CONV_TILE == 2 * CHUNK
    assert ng == 1
    last = b - 1

    def col(part):
        return lambda s, g: (jnp.minimum(s, last), 0, part * ng + g)

    def cur(*rest):
        return lambda s, g: (jnp.maximum(s - 1, 0),) + tuple(g if r == "g" else r for r in rest)

    def cwcol(part):
        return lambda bi, g: (0, part * ng + g)

    return pl.pallas_call(
        functools.partial(_dn_kernel, hb=hb, seq=seq, ctx_len=ctx_len),
        grid=(b + 1, ng),
        in_specs=[
            pl.BlockSpec((1, seq, w), col(0)),
            pl.BlockSpec((1, seq, w), col(1)),
            pl.BlockSpec((1, seq, w), col(2)),
            pl.BlockSpec((1, ctx_len, w), col(0)),
            pl.BlockSpec((1, ctx_len, w), col(1)),
            pl.BlockSpec((1, ctx_len, w), col(2)),
            pl.BlockSpec((1, seq, w), cur(0, "g")),
            pl.BlockSpec((1, n_lat, CHUNK, LANES), cur(0, 0, "g")),
            pl.BlockSpec((1, n_ctx, CHUNK, LANES), cur(0, 0, "g")),
            pl.BlockSpec((CONV_K, w), cwcol(0)),
            pl.BlockSpec((CONV_K, w), cwcol(1)),
            pl.BlockSpec((CONV_K, w), cwcol(2)),
            pl.BlockSpec((1, 2, LANES), lambda bi, g: (g, 0, 0)),
            pl.BlockSpec((1, DN_HEAD_DIM), lambda bi, g: (0, 0)),
            pl.BlockSpec(sh.shape, lambda bi, g: (0, 0, 0)),
        ],
        out_specs=pl.BlockSpec((1, seq, w), cur(0, "g")),
        out_shape=jax.ShapeDtypeStruct((b, seq, DN_HEADS * DN_HEAD_DIM), BF16),
        scratch_shapes=[
            pltpu.VMEM((2, ctx_len + seq, w), BF16),
            pltpu.VMEM((2, ctx_len + seq, w), BF16),
            pltpu.VMEM((2, ctx_len + seq, w), BF16),
            pltpu.VMEM((n_ctx + n_lat, CHUNK, LANES), F32),
            pltpu.VMEM((n_ctx + n_lat, 8, LANES), F32),
            pltpu.VMEM((2 * hb, DN_HEAD_DIM, DN_HEAD_DIM), F32),
            pltpu.VMEM((seq, w), F32),
        ],
        compiler_params=pltpu.CompilerParams(
            dimension_semantics=("arbitrary", "arbitrary"), vmem_limit_bytes=VMEM_LIMIT),
        name="deltanet",
    )(qkv_x, qkv_x, qkv_x, qkv_c, qkv_c, qkv_c, z_x, gcol_x, gcol_c,
      conv_w, conv_w, conv_w, pcol, onw, sh)


def _na_kernel(q_ref, k_ref, v_ref, kc_ref, vc_ref, qw_ref, kw_ref, bias_ref, seg_ref, out_ref,
               qn_s, kn_s, knc_s, *, seq, ctx_len):
    rows = seq // GRID_W
    win_rows = min(WIN_ROWS, rows)
    n_win = win_rows * GRID_W
    n_pairs = NA_HEADS // 2

    def head_rms(x, wgt):
        ss = jnp.dot((x * x).astype(BF16), seg_ref[...], preferred_element_type=F32)
        return x * lax.rsqrt(ss * (1.0 / NA_HEAD_DIM) + EPS) * wgt

    kw = kw_ref[...]
    qw = qw_ref[...] * ((NA_HEAD_DIM ** -0.5) * LOG2E)
    tile = 256

    def norm_body(t, carry):
        r0 = pl.multiple_of(t * tile, tile)
        kn_s[pl.ds(r0, tile), :] = head_rms(k_ref[0, pl.ds(r0, tile), :].astype(F32), kw).astype(BF16)
        qn_s[pl.ds(r0, tile), :] = head_rms(q_ref[0, pl.ds(r0, tile), :].astype(F32), qw).astype(BF16)
        return carry

    lax.fori_loop(0, seq // tile, norm_body, 0)
    knc_s[...] = head_rms(kc_ref[0].astype(F32), kw).astype(BF16)

    lane2 = lax.broadcasted_iota(jnp.int32, (GRID_W, LANES), 1)
    lo2 = lane2 < NA_HEAD_DIM
    ones_w = jnp.ones((n_win, LANES), BF16)
    ones_c = jnp.ones((ctx_len, LANES), BF16)
    zero = jnp.zeros((GRID_W, LANES), BF16)

    def row_body(step, carry):
        streams = [(u, j) for u in range(NA_ROW_UNROLL) for j in range(n_pairs)]
        ns = range(len(streams))
        row = [step * NA_ROW_UNROLL + u for u, _ in streams]
        r_start = [jnp.clip(r - win_rows // 2, 0, rows - win_rows) for r in row]
        case = [r_start[n] - row[n] + (WIN_ROWS - 1) for n in ns]
        q0 = [pl.multiple_of(r * GRID_W, GRID_W) for r in row]
        k0 = [pl.multiple_of(r_start[n] * GRID_W, GRID_W) for n in ns]
        lss = [slice(j * LANES, (j + 1) * LANES) for _, j in streams]
        q2 = [qn_s[pl.ds(q0[n], GRID_W), lss[n]] for n in ns]
        qs = [jnp.concatenate([jnp.where(lo2, q2[n], zero), jnp.where(lo2, zero, q2[n])], axis=0)
              for n in ns]
        s_w = [_mm_nt(qs[n], kn_s[pl.ds(k0[n], n_win), lss[n]]) + bias_ref[case[n], streams[n][1]]
               for n in ns]
        s_c = [_mm_nt(qs[n], knc_s[:, lss[n]]) for n in ns]
        mx = [s_c[n][:, 0:LANES] for n in ns]
        for n in ns:
            for t in range(1, ctx_len // LANES):
                mx[n] = jnp.maximum(mx[n], s_c[n][:, t * LANES:(t + 1) * LANES])
            for t in range(n_win // LANES):
                mx[n] = jnp.maximum(mx[n], s_w[n][:, t * LANES:(t + 1) * LANES])
        mx = [jnp.max(mx[n], axis=-1, keepdims=True) for n in ns]
        p_w = [jnp.exp2(s_w[n] - mx[n]).astype(BF16) for n in ns]
        p_c = [jnp.exp2(s_c[n] - mx[n]).astype(BF16) for n in ns]
        o = [_mm(p_w[n], jnp.concatenate([v_ref[0, pl.ds(k0[n], n_win), lss[n]], ones_w], axis=1))
             + _mm(p_c[n], jnp.concatenate([vc_ref[0, :, lss[n]], ones_c], axis=1)) for n in ns]
        for n in ns:
            on = o[n][:, :LANES] * (1.0 / o[n][:, LANES:])
            out_ref[0, pl.ds(q0[n], GRID_W), lss[n]] = jnp.where(lo2, on[:GRID_W], on[GRID_W:]).astype(out_ref.dtype)
        return carry

    lax.fori_loop(0, rows // NA_ROW_UNROLL, row_body, 0)


def _natten(na_x, na_c, qw, kw, bias):
    b, seq, _ = na_x.shape
    ctx_len = na_c.shape[1]
    wq = NA_HEADS * NA_HEAD_DIM
    assert seq % 256 == 0 and ctx_len % LANES == 0 and (seq // GRID_W) % NA_ROW_UNROLL == 0
    head_of = np.arange(wq) // NA_HEAD_DIM
    seg = jnp.asarray(head_of[:, None] == head_of[None, :], BF16)
    return pl.pallas_call(
        functools.partial(_na_kernel, seq=seq, ctx_len=ctx_len),
        grid=(b,),
        in_specs=[
            pl.BlockSpec((1, seq, wq), lambda bi: (bi, 0, 0)),
            pl.BlockSpec((1, seq, wq), lambda bi: (bi, 0, 1)),
            pl.BlockSpec((1, seq, wq), lambda bi: (bi, 0, 2)),
            pl.BlockSpec((1, ctx_len, wq), lambda bi: (bi, 0, 0)),
            pl.BlockSpec((1, ctx_len, wq), lambda bi: (bi, 0, 1)),
            pl.BlockSpec((1, wq), lambda bi: (0, 0)),
            pl.BlockSpec((1, wq), lambda bi: (0, 0)),
            pl.BlockSpec(bias.shape, lambda bi: (0, 0, 0, 0)),
            pl.BlockSpec((wq, wq), lambda bi: (0, 0)),
        ],
        out_specs=pl.BlockSpec((1, seq, wq), lambda bi: (bi, 0, 0)),
        out_shape=jax.ShapeDtypeStruct((b, seq, wq), BF16),
        scratch_shapes=[
            pltpu.VMEM((seq, wq), BF16),
            pltpu.VMEM((seq, wq), BF16),
            pltpu.VMEM((ctx_len, wq), BF16),
        ],
        compiler_params=pltpu.CompilerParams(
            dimension_semantics=("arbitrary",), vmem_limit_bytes=VMEM_LIMIT),
        name="natten",
    )(na_x, na_x, na_x, na_c, na_c, qw, kw, bias, seg)


def _na_bias_table(rpb, rows):
    win_rows = min(WIN_ROWS, rows)
    cols = np.arange(GRID_W)
    win_start = np.clip(cols - WIN_COLS // 2, 0, GRID_W - WIN_COLS)
    kc = cols[None, :]
    valid = (kc >= win_start[:, None]) & (kc < win_start[:, None] + WIN_COLS)
    rel = np.clip(kc - cols[:, None] + WIN_COLS - 1, 0, 2 * WIN_COLS - 2)
    onehot = (rel[None, :, :] == np.arange(2 * WIN_COLS - 1)[:, None, None]).astype(np.float32)
    g = jnp.einsum("hdr,rck->hdck", rpb * LOG2E, onehot, precision=HIGHEST)
    n_case = WIN_ROWS
    tabs = []
    for case in range(n_case):
        dr = np.clip(case + np.arange(win_rows), 0, 2 * WIN_ROWS - 2)
        t = jnp.transpose(g[:, dr], (0, 2, 1, 3))
        t = jnp.where(valid[None, :, None, :], t, -1e30)
        tabs.append(t.reshape(NA_HEADS // 2, 2 * GRID_W, win_rows * GRID_W))
    return jnp.stack(tabs, axis=0).astype(F32)


def _outffn_kernel(x_ref, dn_ref, na_ref, mod_ref, nw_ref, wo_ref, wi_ref, wf_ref, out_ref, *, d_ff, splits):
    half = dn_ref.shape[-1]
    attn = (jnp.dot(dn_ref[0], wo_ref[0:half, :], preferred_element_type=F32)
            + jnp.dot(na_ref[0], wo_ref[half:, :], preferred_element_type=F32))
    x1 = x_ref[0] + mod_ref[0, 2:3, :] * attn
    ms = jnp.mean(x1 * x1, axis=-1, keepdims=True)
    h = (x1 * lax.rsqrt(ms + EPS)) * nw_ref[...]
    hb = (h * (1.0 + mod_ref[0, 4:5, :]) + mod_ref[0, 3:4, :]).astype(BF16)
    acc = None
    off = 0
    for n in splits:
        gate = jnp.dot(hb, wi_ref[:, off:off + n], preferred_element_type=F32)
        up = jnp.dot(hb, wi_ref[:, d_ff + off:d_ff + off + n], preferred_element_type=F32)
        part = jnp.dot((_silu(gate) * up).astype(BF16), wf_ref[off:off + n, :], preferred_element_type=F32)
        acc = part if acc is None else acc + part
        off += n
    out_ref[0] = x1 + mod_ref[0, 5:6, :] * acc


def _out_ffn(x, dn, na, mods, norm_w, w_out, w_ffn_in, w_ffn_out, *, tm):
    b, l, d = x.shape
    half = dn.shape[-1]
    d_ff = w_ffn_out.shape[0]
    assert sum(FFN_SPLITS) == d_ff
    tm = min(tm, l)
    row_map = lambda bi, i: (bi, i, 0)
    const = lambda bi, i: (0, 0)
    single = pl.Buffered(1)
    return pl.pallas_call(
        functools.partial(_outffn_kernel, d_ff=d_ff, splits=FFN_SPLITS),
        grid=(b, l // tm),
        in_specs=[
            pl.BlockSpec((1, tm, d), row_map),
            pl.BlockSpec((1, tm, half), row_map),
            pl.BlockSpec((1, tm, half), row_map),
            pl.BlockSpec((1, 6, d), lambda bi, i: (bi, 0, 0)),
            pl.BlockSpec((1, d), const),
            pl.BlockSpec(w_out.shape, const, pipeline_mode=single),
            pl.BlockSpec(w_ffn_in.shape, const, pipeline_mode=single),
            pl.BlockSpec(w_ffn_out.shape, const, pipeline_mode=single),
        ],
        out_specs=pl.BlockSpec((1, tm, d), row_map),
        out_shape=jax.ShapeDtypeStruct((b, l, d), F32),
        compiler_params=pltpu.CompilerParams(
            dimension_semantics=("arbitrary", "arbitrary"), vmem_limit_bytes=VMEM_LIMIT),
        name="out_ffn",
    )(x, dn, na, mods, norm_w, w_out, w_ffn_in, w_ffn_out)


def kernel(x, c, ctx, c_ctx, norm1_w, norm2_w, w_ada, b_ada, w_in, dn_conv_w, dn_A_log, dn_dt_bias,
           dn_out_norm_w, na_q_norm_w, na_k_norm_w, na_rpb, w_out, w_ffn_in, w_ffn_out):
    assert w_in.shape[0] == 1, "single-layer problem"
    b, seq, d = x.shape
    hb = DN_HB
    ng = DN_HEADS // hb
    dn_w = DN_HEADS * DN_HEAD_DIM
    na_w = NA_HEADS * NA_HEAD_DIM
    rows = seq // GRID_W

    n_mod_rows = -(-(b + 1) // 8) * 8
    c_all = jnp.zeros((n_mod_rows, d), F32).at[:b].set(c).at[b].set(c_ctx)
    mod = _adaln(c_all, w_ada[0], b_ada[0][None, :])
    mod_x = mod[:b].reshape(b, 6, d)
    mod_c = mod[b:b + 1].reshape(1, 6, d)

    w0 = w_in[0]
    gate0 = 4 * dn_w
    na0 = gate0 + 4 * DN_HEADS
    gate_cols = w0[:, gate0:na0].reshape(d, 4, ng, hb)
    gate_cols = jnp.transpose(gate_cols, (0, 2, 1, 3)).reshape(d, ng, 4 * hb)
    gate_cols = jnp.pad(gate_cols, ((0, 0), (0, 0), (0, LANES - 4 * hb))).reshape(d, ng * LANES)
    w_lat = jnp.concatenate([w0[:, :gate0], w0[:, na0:], gate_cols], axis=1).astype(BF16)
    w_ctx = jnp.concatenate([w0[:, :3 * dn_w], w0[:, na0 + na_w:], gate_cols], axis=1).astype(BF16)
    n_gate = ng * LANES
    nw1 = norm1_w[0][None, :]
    qkv_x, z_x, na_x, ab_x = _in_proj(
        x, mod_x, nw1, w_lat, per_batch_mod=True, tm=IN_ROW_TILE,
        groups=((3 * dn_w, BF16), (dn_w, BF16), (3 * na_w, BF16), (n_gate, F32)))
    qkv_c, na_c, ab_c = _in_proj(
        ctx, mod_c, nw1, w_ctx, per_batch_mod=False, tm=IN_ROW_TILE,
        groups=((3 * dn_w, BF16), (2 * na_w, BF16), (n_gate, F32)))

    def group_params(p):
        return jnp.transpose(p[0].reshape(2, ng, hb), (1, 0, 2)).reshape(ng, 2 * hb)

    pg = jnp.stack([group_params(dn_A_log), group_params(dn_dt_bias)], axis=1)
    pcol = jnp.pad(pg, ((0, 0), (0, 0), (0, LANES - 2 * hb)))
    gcol_x = ab_x.reshape(b, seq // CHUNK, CHUNK, ng * LANES)
    gcol_c = ab_c.reshape(b, ctx.shape[1] // CHUNK, CHUNK, ng * LANES)
    dn_x = _deltanet(qkv_x, qkv_c, z_x, gcol_x, gcol_c, dn_conv_w[0], pcol,
                     dn_out_norm_w[0][None, :], hb=hb)

    bias = _na_bias_table(na_rpb[0], rows)
    qw = jnp.tile(na_q_norm_w[0], NA_HEADS)[None, :]
    kw = jnp.tile(na_k_norm_w[0], NA_HEADS)[None, :]
    na_o = _natten(na_x, na_c, qw, kw, bias)

    return _out_ffn(x, dn_x, na_o, mod_x, norm2_w[0][None, :], w_out[0].astype(BF16),
                    w_ffn_in[0].astype(BF16), w_ffn_out[0].astype(BF16), tm=ROW_TILE)
```

```python
import functools

import numpy as np
import jax
import jax.numpy as jnp
from jax import lax
from jax.experimental import pallas as pl
from jax.experimental.pallas import tpu as pltpu

F32 = jnp.float32
BF16 = jnp.bfloat16
HIGHEST = lax.Precision.HIGHEST

EPS = 1e-6
LOG2E = 1.4426950408889634
CHUNK = 64
CONV_K = 5
DN_HEAD_DIM = 128
DN_HEADS = 4
NA_HEAD_DIM = 64
NA_HEADS = 8
GRID_W = 64
WIN_ROWS = 8
WIN_COLS = 16
LANES = 128
VMEM_LIMIT = 56 * 1024 * 1024

DN_HB = 4
CONV_TILE = 128
CONV_HALO = 16
GATE_UNROLL = 8
NA_ROW_UNROLL = 8
IN_ROW_TILE = 1024
ROW_TILE = 512
FFN_SPLITS = (1024, 1024, 768)


def _silu(x):
    return x * (1.0 / (1.0 + jnp.exp(-x)))


def _softplus(x):
    return jnp.maximum(x, 0.0) + jnp.log(1.0 + jnp.exp(-jnp.abs(x)))


def _mm(a, b):
    return jnp.dot(a.astype(BF16), b.astype(BF16), preferred_element_type=F32)


def _mm_nt(a, b):
    return lax.dot_general(a.astype(BF16), b.astype(BF16), (((1,), (1,)), ((), ())),
                           preferred_element_type=F32)


def _mm_tn(a, b):
    return lax.dot_general(a.astype(BF16), b.astype(BF16), (((0,), (0,)), ((), ())),
                           preferred_element_type=F32)


def _mm_f32(a, b):
    return lax.dot_general(a, b, (((1,), (0,)), ((), ())), precision=HIGHEST,
                           preferred_element_type=F32)


def _adaln_kernel(c_ref, w_ref, b_ref, o_ref):
    o_ref[...] = _mm_f32(_silu(c_ref[...]), w_ref[...]) + b_ref[...]


def _adaln(c_all, w_ada, b_ada):
    rows, d = c_all.shape
    n = w_ada.shape[1]
    tn = 1024
    return pl.pallas_call(
        _adaln_kernel,
        grid=(n // tn,),
        in_specs=[
            pl.BlockSpec((rows, d), lambda j: (0, 0)),
            pl.BlockSpec((d, tn), lambda j: (0, j)),
            pl.BlockSpec((1, tn), lambda j: (0, j)),
        ],
        out_specs=pl.BlockSpec((rows, tn), lambda j: (0, j)),
        out_shape=jax.ShapeDtypeStruct((rows, n), F32),
        compiler_params=pltpu.CompilerParams(
            dimension_semantics=("arbitrary",), vmem_limit_bytes=VMEM_LIMIT),
        name="adaln",
    )(c_all, w_ada, b_ada)


def _inproj_kernel(x_ref, mod_ref, nw_ref, w_ref, *out_refs, widths):
    x = x_ref[0]
    ms = jnp.mean(x * x, axis=-1, keepdims=True)
    h = (x * lax.rsqrt(ms + EPS)) * nw_ref[...]
    h = h * (1.0 + mod_ref[0, 1:2, :]) + mod_ref[0, 0:1, :]
    hb = h.astype(BF16)
    off = 0
    for o_ref, n in zip(out_refs, widths):
        o_ref[0] = jnp.dot(hb, w_ref[:, off:off + n], preferred_element_type=F32).astype(o_ref.dtype)
        off += n


def _in_proj(x, mods, norm_w, w_cols, *, groups, per_batch_mod, tm):
    b, l, d = x.shape
    widths = tuple(n for n, _ in groups)
    assert sum(widths) == w_cols.shape[1]
    tm = min(tm, l)
    mod_map = (lambda bi, i: (bi, 0, 0)) if per_batch_mod else (lambda bi, i: (0, 0, 0))
    row_map = lambda bi, i: (bi, i, 0)
    return pl.pallas_call(
        functools.partial(_inproj_kernel, widths=widths),
        grid=(b, l // tm),
        in_specs=[
            pl.BlockSpec((1, tm, d), row_map),
            pl.BlockSpec((1, 6, d), mod_map),
            pl.BlockSpec((1, d), lambda bi, i: (0, 0)),
            pl.BlockSpec(w_cols.shape, lambda bi, i: (0, 0)),
        ],
        out_specs=[pl.BlockSpec((1, tm, n), row_map) for n in widths],
        out_shape=[jax.ShapeDtypeStruct((b, l, n), dt) for n, dt in groups],
        compiler_params=pltpu.CompilerParams(
            dimension_semantics=("arbitrary", "arbitrary"), vmem_limit_bytes=VMEM_LIMIT),
        name="in_proj",
    )(x, mods, norm_w, w_cols)


def _dn_kernel(qx_ref, kx_ref, vx_ref, qc_ref, kc_ref, vc_ref, z_ref,
               gcx_ref, gcc_ref,
               cwq_ref, cwk_ref, cwv_ref, pc_ref, onw_ref, sh_ref,
               out_ref,
               q_s, k_s, v_s, gc_s, gr_s, s_s, o_s, *, hb, seq, ctx_len):
    n_ctx = ctx_len // CHUNK
    n_lat = seq // CHUNK
    tile = CONV_TILE
    win = tile + 2 * CONV_HALO
    side_taps = [j for j in range(CONV_K) if j != CONV_K // 2]

    step = pl.program_id(0)
    slot_p = step % 2
    slot_c = 1 - slot_p

    streams = ((cwq_ref, q_s, "q"), (cwk_ref, k_s, "k"), (cwv_ref, v_s, "v"))
    lat_srcs = (qx_ref, kx_ref, vx_ref)
    ctx_srcs = (qc_ref, kc_ref, vc_ref)

    def conv_stages(srcs, n_rows, dst_off, t):
        n_tiles = n_rows // tile
        if isinstance(t, int):
            r0 = t * tile
            a0 = min(max(r0 - CONV_HALO, 0), n_rows - win)
            variant = 0 if t == 0 else (2 if t == n_tiles - 1 else 1)
        else:
            r0 = pl.multiple_of(t * tile, tile)
            a0 = pl.multiple_of(jnp.clip(r0 - CONV_HALO, 0, n_rows - win), CONV_HALO)
            variant = jnp.where(t == 0, 0, jnp.where(t == n_tiles - 1, 2, 1))
        sh = sh_ref[variant]
        zs = [jnp.dot(sh, src[0, pl.ds(a0, win), :], preferred_element_type=F32) for src in srcs]
        yield
        ys = []
        for src, z, (cw_ref, _, _) in zip(srcs, zs, streams):
            cw = cw_ref[...]
            acc = cw[CONV_K // 2:CONV_K // 2 + 1, :] * src[0, pl.ds(r0, tile), :].astype(F32)
            for n, j in enumerate(side_taps):
                acc = acc + cw[j:j + 1, :] * z[n * tile:(n + 1) * tile, :]
            ys.append(_silu(acc))
            yield
        for y, (_, dst_ref, mode) in zip(ys, streams):
            if mode != "v":
                parts = []
                for hh in range(hb):
                    seg = y[:, hh * DN_HEAD_DIM:(hh + 1) * DN_HEAD_DIM]
                    inv = lax.rsqrt(jnp.sum(seg * seg, axis=-1, keepdims=True) + EPS)
                    if mode == "q":
                        inv = inv * (DN_HEAD_DIM ** -0.5)
                    parts.append(seg * inv)
                y = parts[0] if hb == 1 else jnp.concatenate(parts, axis=1)
            dst_ref[slot_p, pl.ds(dst_off + r0, tile), :] = y.astype(dst_ref.dtype)
            yield

    def conv_tile(srcs, n_rows, dst_off, t):
        for _ in conv_stages(srcs, n_rows, dst_off, t):
            pass

    @pl.when(step == 0)
    def _():
        for t in range(ctx_len // tile):
            conv_tile(ctx_srcs, ctx_len, 0, t)

        def body(t, carry):
            conv_tile(lat_srcs, seq, ctx_len, t)
            return carry

        lax.fori_loop(0, seq // tile, body, 0)

    n_hp = hb // 2
    ri = lax.broadcasted_iota(jnp.int32, (CHUNK, CHUNK), 0)
    ci = lax.broadcasted_iota(jnp.int32, (CHUNK, CHUNK), 1)
    lower = jnp.where(ci <= ri, 1.0, 0.0).astype(BF16)
    upper = jnp.where(ci >= ri, 1.0, 0.0).astype(BF16)
    tri_c = jnp.concatenate([lower, upper], axis=1)
    a_log_c, dt_c = pc_ref[0, 0:1, :], pc_ref[0, 1:2, :]
    lane = lax.broadcasted_iota(jnp.int32, (CHUNK, LANES), 1)
    pr = lax.broadcasted_iota(jnp.int32, (8, LANES), 0)
    pl_ = lax.broadcasted_iota(jnp.int32, (8, LANES), 1)
    pick = jnp.where((pr < 2 * n_hp) & (pl_ == (pr // n_hp) * hb + 2 * (pr % n_hp)), 1.0, 0.0).astype(BF16)

    def split3(x):
        x1 = x.astype(BF16)
        r1 = x - x1.astype(F32)
        x2 = r1.astype(BF16)
        x3 = (r1 - x2.astype(F32)).astype(BF16)
        return x1, x2, x3

    def gate_stages(gc_ref, ts, off):
        xc = [gc_ref[0, t] for t in ts]
        val = [jnp.where(lane < 2 * hb, -jnp.exp(a_log_c) * _softplus(x + dt_c), 1.0 / (1.0 + jnp.exp(-x)))
               for x in xc]
        yield
        gcs = [split3(jnp.concatenate([jnp.where(lane < hb, v, 0.0),
                                       jnp.where((lane >= hb) & (lane < 2 * hb), v, 0.0)], axis=0))
               for v in val]
        gam_c = [sum(jnp.dot(tri_c, p, preferred_element_type=F32) for p in ps) for ps in gcs]
        yield
        both = [split3(jnp.concatenate([g, pltpu.roll(g, LANES - 1, 1)], axis=0)) for g in gam_c]
        gam_r = [sum(lax.dot_general(pick, p, (((1,), (1,)), ((), ())), preferred_element_type=F32)
                     for p in ps) for ps in both]
        yield
        for u, t in enumerate(ts):
            gc_s[slot_p, off + t] = jnp.where(lane < 2 * hb, gam_c[u], val[u])
            gr_s[slot_p, off + t] = gam_r[u]
        yield

    def gates_into(gc_ref, n_chunks, off):
        unroll = min(GATE_UNROLL, n_chunks)

        def body(it, carry):
            for _ in gate_stages(gc_ref, [it * unroll + u for u in range(unroll)], off):
                pass
            return carry

        lax.fori_loop(0, n_chunks // unroll, body, 0)

    @pl.when(step == 0)
    def _():
        gates_into(gcc_ref, n_ctx, 0)
        gates_into(gcx_ref, n_lat, n_ctx)

    onw = onw_ref[...]
    prow = lax.broadcasted_iota(jnp.int32, (CHUNK, LANES), 0)
    pcol = lane % CHUNK
    lo = lane < CHUNK
    eye_p = jnp.where(pcol == prow, 1.0, 0.0)
    before_eq = (pcol <= prow, pcol >= prow)
    strict = (pcol < prow, pcol > prow)
    zeros_w = jnp.zeros((CHUNK, 2 * DN_HEAD_DIM), BF16)

    def block_diag(x):
        xb = x.astype(BF16)
        zero = jnp.zeros_like(xb)
        return jnp.concatenate([jnp.where(lo, xb, zero), jnp.where(lo, zero, xb)], axis=0)

    def chunk_steps(groups, out_mode, side=None):
        def tick():
            if side is not None:
                next(side, None)

        pairs = [p for grp in groups for p in grp]
        rng = range(len(pairs))
        r0s = [pl.multiple_of(chunk * CHUNK, CHUNK) for _, _, chunk in pairs]
        wide = [slice(hp * 2 * DN_HEAD_DIM, (hp + 1) * 2 * DN_HEAD_DIM) for _, hp, _ in pairs]
        q = [q_s[slot_c, pl.ds(r0s[i], CHUNK), wide[i]].astype(F32) for i in rng]
        k = [k_s[slot_c, pl.ds(r0s[i], CHUNK), wide[i]].astype(F32) for i in rng]
        v = [v_s[slot_c, pl.ds(r0s[i], CHUNK), wide[i]].astype(F32) for i in rng]
        gcol = [gc_s[slot_c, pairs[i][2]] for i in rng]
        grow = [gr_s[slot_c, pairs[i][2]] for i in rng]
        cs = [[d * hb + 2 * hp + e for e in range(2)] for d, hp, _ in pairs]
        dirs = [d for d, _, _ in pairs]
        gam_c = [[gcol[i][:, c:c + 1] for c in cs[i]] for i in rng]
        beta_c = [[gcol[i][:, 2 * hb + c:2 * hb + c + 1] for c in cs[i]] for i in rng]
        g_tot = [[g[CHUNK - 1:CHUNK, :] if dirs[i] == 0 else g[0:1, :] for g in gam_c[i]] for i in rng]
        gam_cp = [jnp.where(lo, gam_c[i][0], gam_c[i][1]) for i in rng]
        beta_cp = [jnp.where(lo, beta_c[i][0], beta_c[i][1]) for i in rng]
        gam_rp = [grow[i][dirs[i] * n_hp + pairs[i][1]:dirs[i] * n_hp + pairs[i][1] + 1, :] for i in rng]
        decay = [jnp.exp(jnp.where(before_eq[dirs[i]], gam_cp[i] - gam_rp[i], -jnp.inf)) for i in rng]
        kb = [k[i].astype(BF16) for i in rng]
        k_bd = [jnp.concatenate([jnp.concatenate([kb[i][:, :DN_HEAD_DIM], zeros_w[:, :DN_HEAD_DIM]], axis=1),
                                 jnp.concatenate([zeros_w[:, :DN_HEAD_DIM], kb[i][:, DN_HEAD_DIM:]], axis=1)],
                                axis=0) for i in rng]
        qk_kk = [_mm_nt(jnp.concatenate([q[i], k[i]], axis=0), k_bd[i]) for i in rng]
        tick()
        qkd = [qk_kk[i][:CHUNK] * decay[i] for i in rng]
        m = [jnp.where(strict[dirs[i]], qk_kk[i][CHUNK:] * beta_cp[i] * decay[i], 0.0) for i in rng]
        t_inv = [eye_p - m[i] for i in rng]
        pw = [_mm(m[i], block_diag(m[i])) for i in rng]
        tick()
        for _ in range(4):
            res = [_mm(jnp.concatenate([pw[i], t_inv[i]], axis=0), block_diag(pw[i])) for i in rng]
            tick()
            pw = [res[i][:CHUNK] for i in rng]
            t_inv = [t_inv[i] + res[i][CHUNK:] for i in rng]
        t_inv = [t_inv[i] + _mm(t_inv[i], block_diag(pw[i])) for i in rng]
        e_gam = [[jnp.exp(g) for g in gam_c[i]] for i in rng]
        hd = [slice(0, DN_HEAD_DIM), slice(DN_HEAD_DIM, 2 * DN_HEAD_DIM)]
        rhs = [[jnp.concatenate([v[i][:, hd[e]] * beta_c[i][e], k[i][:, hd[e]] * (beta_c[i][e] * e_gam[i][e])],
                                axis=1).astype(BF16) for e in range(2)] for i in rng]
        rhs_bd = [jnp.concatenate([jnp.concatenate([rhs[i][0], zeros_w], axis=1),
                                   jnp.concatenate([zeros_w, rhs[i][1]], axis=1)], axis=0) for i in rng]
        uw = [_mm(t_inv[i], rhs_bd[i]) for i in rng]
        tick()
        u_h = [[uw[i][:, (2 * e) * DN_HEAD_DIM:(2 * e + 1) * DN_HEAD_DIM] for e in range(2)] for i in rng]
        w_h = [[uw[i][:, (2 * e + 1) * DN_HEAD_DIM:(2 * e + 2) * DN_HEAD_DIM] for e in range(2)] for i in rng]
        kdec = [[(k[i][:, hd[e]] * jnp.exp(g_tot[i][e] - gam_c[i][e])).astype(BF16) for e in range(2)] for i in rng]
        s_scale = [[jnp.exp(g) for g in g_tot[i]] for i in rng]
        if out_mode is None:
            lhs = [[w_h[i][e].astype(BF16) for e in range(2)] for i in rng]
        else:
            lhs = [[jnp.concatenate([q[i][:, hd[e]] * e_gam[i][e], w_h[i][e]], axis=0).astype(BF16)
                    for e in range(2)] for i in rng]

        state = {}
        pos = 0
        for grp in groups:
            idx = range(pos, pos + len(grp))
            pos += len(grp)
            ch = [(i, e) for i in idx for e in range(2)]
            s_old = {(i, e): (state[cs[i][e]] if cs[i][e] in state else s_s[cs[i][e]]) for i, e in ch}
            prod = {(i, e): _mm(lhs[i][e], s_old[(i, e)]) for i, e in ch}
            tick()
            v_new = {(i, e): u_h[i][e] - prod[(i, e)][-CHUNK:] for i, e in ch}
            for i, e in ch:
                state[cs[i][e]] = s_scale[i][e] * s_old[(i, e)] + _mm_tn(kdec[i][e], v_new[(i, e)])
            tick()
            if out_mode is None:
                continue
            zeros_h = zeros_w[:, :DN_HEAD_DIM]
            v_bd = {i: jnp.concatenate(
                [jnp.concatenate([v_new[(i, 0)].astype(BF16), zeros_h], axis=1),
                 jnp.concatenate([zeros_h, v_new[(i, 1)].astype(BF16)], axis=1)], axis=0) for i in idx}
            intra = {i: _mm(qkd[i], v_bd[i]) for i in idx}
            for i, e in ch:
                o = prod[(i, e)][:CHUNK] + intra[i][:, hd[e]]
                rl = pl.multiple_of(r0s[i] - ctx_len, CHUNK)
                hh = 2 * pairs[i][1] + e
                hs = slice(hh * DN_HEAD_DIM, (hh + 1) * DN_HEAD_DIM)
                if out_mode == "store":
                    o_s[pl.ds(rl, CHUNK), hs] = o
                else:
                    tot = o + o_s[pl.ds(rl, CHUNK), hs]
                    y = tot * lax.rsqrt(jnp.mean(tot * tot, axis=-1, keepdims=True) + EPS) * onw
                    zz = z_ref[0, pl.ds(rl, CHUNK), hs].astype(F32)
                    out_ref[0, pl.ds(rl, CHUNK), hs] = (y * _silu(zz)).astype(out_ref.dtype)
        for c, val in state.items():
            s_s[c] = val
        if side is not None:
            for _ in side:
                pass

    head_pairs = range(n_hp)

    def groups_at(first, last, i):
        return [(0, hp, first + i) for hp in head_pairs] + [(1, hp, last - i) for hp in head_pairs]

    @pl.when(step > 0)
    def _():
        s_s[...] = jnp.zeros(s_s.shape, F32)

        def side_work(srcs, n_rows, dst_off, gc_ref, chunk_off, j):
            yield from conv_stages(srcs, n_rows, dst_off, j)
            yield from gate_stages(gc_ref, [2 * j, 2 * j + 1], chunk_off)

        for j in range(n_ctx // 2):
            chunk_steps([groups_at(0, n_ctx - 1, 2 * j), groups_at(0, n_ctx - 1, 2 * j + 1)], None,
                        side=side_work(ctx_srcs, ctx_len, 0, gcc_ref, 0, j))

        def lat_body(mode, j, carry):
            first, last = n_ctx, n_ctx + n_lat - 1
            chunk_steps([groups_at(first, last, 2 * j), groups_at(first, last, 2 * j + 1)], mode,
                        side=side_work(lat_srcs, seq, ctx_len, gcx_ref, n_ctx, j))
            return carry

        lax.fori_loop(0, n_lat // 4, functools.partial(lat_body, "store"), 0)
        lax.fori_loop(n_lat // 4, n_lat // 2, functools.partial(lat_body, "final"), 0)


def _conv_shift_matrices():
    side = [j - CONV_K // 2 for j in range(CONV_K) if j != CONV_K // 2]
    win = CONV_TILE + 2 * CONV_HALO
    sh = np.zeros((3, len(side) * CONV_TILE, win), np.float32)
    for d in range(3):
        for n, off in enumerate(side):
            for i in range(CONV_TILE):
                m = i + off + d * CONV_HALO
                if 0 <= m < win:
                    sh[d, n * CONV_TILE + i, m] = 1.0
    return jnp.asarray(sh, BF16)


def _deltanet(qkv_x, qkv_c, z_x, gcol_x, gcol_c, conv_w, pcol, onw, *, hb):
    b, seq, _ = qkv_x.shape
    ctx_len = qkv_c.shape[1]
    ng = DN_HEADS // hb
    w = hb * DN_HEAD_DIM
    n_lat, n_ctx = seq // CHUNK, ctx_len // CHUNK
    assert n_lat % 4 == 0 and n_ctx % 2 == 0 and seq >= 2 * CONV_TILE and ctx_len >= 2 * CONV_TILE
    assert n_lat % GATE_UNROLL == 0 and GATE_UNROLL % n_ctx == 0 and hb % 2 == 0 and hb <= 8
    sh = _conv_shift_matrices()

    assert CONV_TILE == 2 * CHUNK
    assert ng == 1
    last = b - 1

    def col(part):
        return lambda s, g: (jnp.minimum(s, last), 0, part * ng + g)

    def cur(*rest):
        return lambda s, g: (jnp.maximum(s - 1, 0),) + tuple(g if r == "g" else r for r in rest)

    def cwcol(part):
        return lambda bi, g: (0, part * ng + g)

    return pl.pallas_call(
        functools.partial(_dn_kernel, hb=hb, seq=seq, ctx_len=ctx_len),
        grid=(b + 1, ng),
        in_specs=[
            pl.BlockSpec((1, seq, w), col(0)),
            pl.BlockSpec((1, seq, w), col(1)),
            pl.BlockSpec((1, seq, w), col(2)),
            pl.BlockSpec((1, ctx_len, w), col(0)),
            pl.BlockSpec((1, ctx_len, w), col(1)),
            pl.BlockSpec((1, ctx_len, w), col(2)),
            pl.BlockSpec((1, seq, w), cur(0, "g")),
            pl.BlockSpec((1, n_lat, CHUNK, LANES), lambda s, g: (jnp.minimum(s, last), 0, 0, g)),
            pl.BlockSpec((1, n_ctx, CHUNK, LANES), lambda s, g: (jnp.minimum(s, last), 0, 0, g)),
            pl.BlockSpec((CONV_K, w), cwcol(0)),
            pl.BlockSpec((CONV_K, w), cwcol(1)),
            pl.BlockSpec((CONV_K, w), cwcol(2)),
            pl.BlockSpec((1, 2, LANES), lambda bi, g: (g, 0, 0)),
            pl.BlockSpec((1, DN_HEAD_DIM), lambda bi, g: (0, 0)),
            pl.BlockSpec(sh.shape, lambda bi, g: (0, 0, 0)),
        ],
        out_specs=pl.BlockSpec((1, seq, w), cur(0, "g")),
        out_shape=jax.ShapeDtypeStruct((b, seq, DN_HEADS * DN_HEAD_DIM), BF16),
        scratch_shapes=[
            pltpu.VMEM((2, ctx_len + seq, w), BF16),
            pltpu.VMEM((2, ctx_len + seq, w), BF16),
            pltpu.VMEM((2, ctx_len + seq, w), BF16),
            pltpu.VMEM((2, n_ctx + n_lat, CHUNK, LANES), F32),
            pltpu.VMEM((2, n_ctx + n_lat, 8, LANES), F32),
            pltpu.VMEM((2 * hb, DN_HEAD_DIM, DN_HEAD_DIM), F32),
            pltpu.VMEM((seq, w), F32),
        ],
        compiler_params=pltpu.CompilerParams(
            dimension_semantics=("arbitrary", "arbitrary"), vmem_limit_bytes=VMEM_LIMIT),
        name="deltanet",
    )(qkv_x, qkv_x, qkv_x, qkv_c, qkv_c, qkv_c, z_x, gcol_x, gcol_c,
      conv_w, conv_w, conv_w, pcol, onw, sh)


def _na_kernel(q_ref, k_ref, v_ref, kc_ref, vc_ref, qw_ref, kw_ref, bias_ref, seg_ref, out_ref,
               qn_s, kn_s, knc_s, *, seq, ctx_len):
    rows = seq // GRID_W
    win_rows = min(WIN_ROWS, rows)
    n_win = win_rows * GRID_W
    n_pairs = NA_HEADS // 2

    def head_rms(x, wgt):
        ss = jnp.dot((x * x).astype(BF16), seg_ref[...], preferred_element_type=F32)
        return x * lax.rsqrt(ss * (1.0 / NA_HEAD_DIM) + EPS) * wgt

    kw = kw_ref[...]
    qw = qw_ref[...] * ((NA_HEAD_DIM ** -0.5) * LOG2E)
    tile = 256

    def norm_body(t, carry):
        r0 = pl.multiple_of(t * tile, tile)
        kn_s[pl.ds(r0, tile), :] = head_rms(k_ref[0, pl.ds(r0, tile), :].astype(F32), kw).astype(BF16)
        qn_s[pl.ds(r0, tile), :] = head_rms(q_ref[0, pl.ds(r0, tile), :].astype(F32), qw).astype(BF16)
        return carry

    lax.fori_loop(0, seq // tile, norm_body, 0)
    knc_s[...] = head_rms(kc_ref[0].astype(F32), kw).astype(BF16)

    lane2 = lax.broadcasted_iota(jnp.int32, (GRID_W, LANES), 1)
    lo2 = lane2 < NA_HEAD_DIM
    ones_w = jnp.ones((n_win, LANES), BF16)
    ones_c = jnp.ones((ctx_len, LANES), BF16)
    zero = jnp.zeros((GRID_W, LANES), BF16)

    def row_body(step, carry):
        streams = [(u, j) for u in range(NA_ROW_UNROLL) for j in range(n_pairs)]
        ns = range(len(streams))
        row = [step * NA_ROW_UNROLL + u for u, _ in streams]
        r_start = [jnp.clip(r - win_rows // 2, 0, rows - win_rows) for r in row]
        case = [r_start[n] - row[n] + (WIN_ROWS - 1) for n in ns]
        q0 = [pl.multiple_of(r * GRID_W, GRID_W) for r in row]
        k0 = [pl.multiple_of(r_start[n] * GRID_W, GRID_W) for n in ns]
        lss = [slice(j * LANES, (j + 1) * LANES) for _, j in streams]
        q2 = [qn_s[pl.ds(q0[n], GRID_W), lss[n]] for n in ns]
        qs = [jnp.concatenate([jnp.where(lo2, q2[n], zero), jnp.where(lo2, zero, q2[n])], axis=0)
              for n in ns]
        s_w = [_mm_nt(qs[n], kn_s[pl.ds(k0[n], n_win), lss[n]]) + bias_ref[case[n], streams[n][1]]
               for n in ns]
        s_c = [_mm_nt(qs[n], knc_s[:, lss[n]]) for n in ns]
        mx = [s_c[n][:, 0:LANES] for n in ns]
        for n in ns:
            for t in range(1, ctx_len // LANES):
                mx[n] = jnp.maximum(mx[n], s_c[n][:, t * LANES:(t + 1) * LANES])
            for t in range(n_win // LANES):
                mx[n] = jnp.maximum(mx[n], s_w[n][:, t * LANES:(t + 1) * LANES])
        mx = [jnp.max(mx[n], axis=-1, keepdims=True) for n in ns]
        p_w = [jnp.exp2(s_w[n] - mx[n]).astype(BF16) for n in ns]
        p_c = [jnp.exp2(s_c[n] - mx[n]).astype(BF16) for n in ns]
        o = [_mm(p_w[n], jnp.concatenate([v_ref[0, pl.ds(k0[n], n_win), lss[n]], ones_w], axis=1))
             + _mm(p_c[n], jnp.concatenate([vc_ref[0, :, lss[n]], ones_c], axis=1)) for n in ns]
        for n in ns:
            on = o[n][:, :LANES] * (1.0 / o[n][:, LANES:])
            out_ref[0, pl.ds(q0[n], GRID_W), lss[n]] = jnp.where(lo2, on[:GRID_W], on[GRID_W:]).astype(out_ref.dtype)
        return carry

    lax.fori_loop(0, rows // NA_ROW_UNROLL, row_body, 0)


def _natten(na_x, na_c, qw, kw, bias):
    b, seq, _ = na_x.shape
    ctx_len = na_c.shape[1]
    wq = NA_HEADS * NA_HEAD_DIM
    assert seq % 256 == 0 and ctx_len % LANES == 0 and (seq // GRID_W) % NA_ROW_UNROLL == 0
    head_of = np.arange(wq) // NA_HEAD_DIM
    seg = jnp.asarray(head_of[:, None] == head_of[None, :], BF16)
    return pl.pallas_call(
        functools.partial(_na_kernel, seq=seq, ctx_len=ctx_len),
        grid=(b,),
        in_specs=[
            pl.BlockSpec((1, seq, wq), lambda bi: (bi, 0, 0)),
            pl.BlockSpec((1, seq, wq), lambda bi: (bi, 0, 1)),
            pl.BlockSpec((1, seq, wq), lambda bi: (bi, 0, 2)),
            pl.BlockSpec((1, ctx_len, wq), lambda bi: (bi, 0, 0)),
            pl.BlockSpec((1, ctx_len, wq), lambda bi: (bi, 0, 1)),
            pl.BlockSpec((1, wq), lambda bi: (0, 0)),
            pl.BlockSpec((1, wq), lambda bi: (0, 0)),
            pl.BlockSpec(bias.shape, lambda bi: (0, 0, 0, 0)),
            pl.BlockSpec((wq, wq), lambda bi: (0, 0)),
        ],
        out_specs=pl.BlockSpec((1, seq, wq), lambda bi: (bi, 0, 0)),
        out_shape=jax.ShapeDtypeStruct((b, seq, wq), BF16),
        scratch_shapes=[
            pltpu.VMEM((seq, wq), BF16),
            pltpu.VMEM((seq, wq), BF16),
            pltpu.VMEM((ctx_len, wq), BF16),
        ],
        compiler_params=pltpu.CompilerParams(
            dimension_semantics=("arbitrary",), vmem_limit_bytes=VMEM_LIMIT),
        name="natten",
    )(na_x, na_x, na_x, na_c, na_c, qw, kw, bias, seg)


def _na_bias_table(rpb, rows):
    win_rows = min(WIN_ROWS, rows)
    cols = np.arange(GRID_W)
    win_start = np.clip(cols - WIN_COLS // 2, 0, GRID_W - WIN_COLS)
    kc = cols[None, :]
    valid = (kc >= win_start[:, None]) & (kc < win_start[:, None] + WIN_COLS)
    rel = np.clip(kc - cols[:, None] + WIN_COLS - 1, 0, 2 * WIN_COLS - 2)
    assert win_rows == WIN_ROWS
    pick_rel = (rel[None, :, :] == np.arange(2 * WIN_COLS - 1)[:, None, None]).astype(np.float32)
    case_i = np.arange(WIN_ROWS)[:, None] + np.arange(win_rows)[None, :]
    pick_dr = (case_i[:, :, None] == np.arange(2 * WIN_ROWS - 1)[None, None, :]).astype(np.float32)
    tab = jnp.einsum("hdr,rck,aid->ahcik", rpb * LOG2E, pick_rel, pick_dr, precision=HIGHEST)
    tab = jnp.where(valid[None, None, :, None, :], tab, -1e30)
    return tab.reshape(WIN_ROWS, NA_HEADS // 2, 2 * GRID_W, win_rows * GRID_W).astype(F32)


def _outffn_kernel(x_ref, dn_ref, na_ref, mod_ref, nw_ref, wo_ref, wi_ref, wf_ref, out_ref, *, d_ff, splits):
    half = dn_ref.shape[-1]
    attn = (jnp.dot(dn_ref[0], wo_ref[0:half, :], preferred_element_type=F32)
            + jnp.dot(na_ref[0], wo_ref[half:, :], preferred_element_type=F32))
    x1 = x_ref[0] + mod_ref[0, 2:3, :] * attn
    ms = jnp.mean(x1 * x1, axis=-1, keepdims=True)
    h = (x1 * lax.rsqrt(ms + EPS)) * nw_ref[...]
    hb = (h * (1.0 + mod_ref[0, 4:5, :]) + mod_ref[0, 3:4, :]).astype(BF16)
    acc = None
    off = 0
    for n in splits:
        gate = jnp.dot(hb, wi_ref[:, off:off + n], preferred_element_type=F32)
        up = jnp.dot(hb, wi_ref[:, d_ff + off:d_ff + off + n], preferred_element_type=F32)
        part = jnp.dot((_silu(gate) * up).astype(BF16), wf_ref[off:off + n, :], preferred_element_type=F32)
        acc = part if acc is None else acc + part
        off += n
    out_ref[0] = x1 + mod_ref[0, 5:6, :] * acc


def _out_ffn(x, dn, na, mods, norm_w, w_out, w_ffn_in, w_ffn_out, *, tm):
    b, l, d = x.shape
    half = dn.shape[-1]
    d_ff = w_ffn_out.shape[0]
    assert sum(FFN_SPLITS) == d_ff
    tm = min(tm, l)
    row_map = lambda bi, i: (bi, i, 0)
    const = lambda bi, i: (0, 0)
    single = pl.Buffered(1)
    return pl.pallas_call(
        functools.partial(_outffn_kernel, d_ff=d_ff, splits=FFN_SPLITS),
        grid=(b, l // tm),
        in_specs=[
            pl.BlockSpec((1, tm, d), row_map),
            pl.BlockSpec((1, tm, half), row_map),
            pl.BlockSpec((1, tm, half), row_map),
            pl.BlockSpec((1, 6, d), lambda bi, i: (bi, 0, 0)),
            pl.BlockSpec((1, d), const),
            pl.BlockSpec(w_out.shape, const, pipeline_mode=single),
            pl.BlockSpec(w_ffn_in.shape, const, pipeline_mode=single),
            pl.BlockSpec(w_ffn_out.shape, const, pipeline_mode=single),
        ],
        out_specs=pl.BlockSpec((1, tm, d), row_map),
        out_shape=jax.ShapeDtypeStruct((b, l, d), F32),
        compiler_params=pltpu.CompilerParams(
            dimension_semantics=("arbitrary", "arbitrary"), vmem_limit_bytes=VMEM_LIMIT),
        name="out_ffn",
    )(x, dn, na, mods, norm_w, w_out, w_ffn_in, w_ffn_out)


def kernel(x, c, ctx, c_ctx, norm1_w, norm2_w, w_ada, b_ada, w_in, dn_conv_w, dn_A_log, dn_dt_bias,
           dn_out_norm_w, na_q_norm_w, na_k_norm_w, na_rpb, w_out, w_ffn_in, w_ffn_out):
    assert w_in.shape[0] == 1, "single-layer problem"
    b, seq, d = x.shape
    hb = DN_HB
    ng = DN_HEADS // hb
    dn_w = DN_HEADS * DN_HEAD_DIM
    na_w = NA_HEADS * NA_HEAD_DIM
    rows = seq // GRID_W

    n_mod_rows = -(-(b + 1) // 8) * 8
    c_all = jnp.zeros((n_mod_rows, d), F32).at[:b].set(c).at[b].set(c_ctx)
    mod = _adaln(c_all, w_ada[0], b_ada[0][None, :])
    mod_x = mod[:b].reshape(b, 6, d)
    mod_c = mod[b:b + 1].reshape(1, 6, d)

    w0 = w_in[0]
    gate0 = 4 * dn_w
    na0 = gate0 + 4 * DN_HEADS
    gate_cols = w0[:, gate0:na0].reshape(d, 4, ng, hb)
    gate_cols = jnp.transpose(gate_cols, (0, 2, 1, 3)).reshape(d, ng, 4 * hb)
    gate_cols = jnp.pad(gate_cols, ((0, 0), (0, 0), (0, LANES - 4 * hb))).reshape(d, ng * LANES)
    w_lat = jnp.concatenate([w0[:, :gate0], w0[:, na0:], gate_cols], axis=1).astype(BF16)
    w_ctx = jnp.concatenate([w0[:, :3 * dn_w], w0[:, na0 + na_w:], gate_cols], axis=1).astype(BF16)
    n_gate = ng * LANES
    nw1 = norm1_w[0][None, :]
    qkv_x, z_x, na_x, ab_x = _in_proj(
        x, mod_x, nw1, w_lat, per_batch_mod=True, tm=IN_ROW_TILE,
        groups=((3 * dn_w, BF16), (dn_w, BF16), (3 * na_w, BF16), (n_gate, F32)))
    qkv_c, na_c, ab_c = _in_proj(
        ctx, mod_c, nw1, w_ctx, per_batch_mod=False, tm=IN_ROW_TILE,
        groups=((3 * dn_w, BF16), (2 * na_w, BF16), (n_gate, F32)))

    def group_params(p):
        return jnp.transpose(p[0].reshape(2, ng, hb), (1, 0, 2)).reshape(ng, 2 * hb)

    pg = jnp.stack([group_params(dn_A_log), group_params(dn_dt_bias)], axis=1)
    pcol = jnp.pad(pg, ((0, 0), (0, 0), (0, LANES - 2 * hb)))
    gcol_x = ab_x.reshape(b, seq // CHUNK, CHUNK, ng * LANES)
    gcol_c = ab_c.reshape(b, ctx.shape[1] // CHUNK, CHUNK, ng * LANES)
    dn_x = _deltanet(qkv_x, qkv_c, z_x, gcol_x, gcol_c, dn_conv_w[0], pcol,
                     dn_out_norm_w[0][None, :], hb=hb)

    bias = _na_bias_table(na_rpb[0], rows)
    qw = jnp.tile(na_q_norm_w[0], NA_HEADS)[None, :]
    kw = jnp.tile(na_k_norm_w[0], NA_HEADS)[None, :]
    na_o = _natten(na_x, na_c, qw, kw, bias)

    return _out_ffn(x, dn_x, na_o, mod_x, norm2_w[0][None, :], w_out[0].astype(BF16),
                    w_ffn_in[0].astype(BF16), w_ffn_out[0].astype(BF16), tm=ROW_TILE)
```

```python
import functools

import numpy as np
import jax
import jax.numpy as jnp
from jax import lax
from jax.experimental import pallas as pl
from jax.experimental.pallas import tpu as pltpu

F32 = jnp.float32
BF16 = jnp.bfloat16
HIGHEST = lax.Precision.HIGHEST

EPS = 1e-6
LOG2E = 1.4426950408889634
CHUNK = 64
CONV_K = 5
DN_HEAD_DIM = 128
DN_HEADS = 4
NA_HEAD_DIM = 64
NA_HEADS = 8
GRID_W = 64
WIN_ROWS = 8
WIN_COLS = 16
LANES = 128
VMEM_LIMIT = 56 * 1024 * 1024

DN_HB = 4
DN_CHUNKS_PER_STEP = 4
CONV_TILE = 128
CONV_HALO = 16
GATE_UNROLL = 8
NA_ROW_UNROLL = 8
IN_ROW_TILE = 1024
ROW_TILE = 512
FFN_SPLITS = (1024, 1024, 768)


def _silu(x):
    return x * (1.0 / (1.0 + jnp.exp(-x)))


def _softplus(x):
    return jnp.maximum(x, 0.0) + jnp.log(1.0 + jnp.exp(-jnp.abs(x)))


def _mm(a, b):
    return jnp.dot(a.astype(BF16), b.astype(BF16), preferred_element_type=F32)


def _mm_nt(a, b):
    return lax.dot_general(a.astype(BF16), b.astype(BF16), (((1,), (1,)), ((), ())),
                           preferred_element_type=F32)


def _mm_tn(a, b):
    return lax.dot_general(a.astype(BF16), b.astype(BF16), (((0,), (0,)), ((), ())),
                           preferred_element_type=F32)


def _mm_f32(a, b):
    return lax.dot_general(a, b, (((1,), (0,)), ((), ())), precision=HIGHEST,
                           preferred_element_type=F32)


def _adaln_kernel(c_ref, w_ref, b_ref, o_ref):
    o_ref[...] = _mm_f32(_silu(c_ref[...]), w_ref[...]) + b_ref[...]


def _adaln(c_all, w_ada, b_ada):
    rows, d = c_all.shape
    n = w_ada.shape[1]
    tn = 1024
    return pl.pallas_call(
        _adaln_kernel,
        grid=(n // tn,),
        in_specs=[
            pl.BlockSpec((rows, d), lambda j: (0, 0)),
            pl.BlockSpec((d, tn), lambda j: (0, j)),
            pl.BlockSpec((1, tn), lambda j: (0, j)),
        ],
        out_specs=pl.BlockSpec((rows, tn), lambda j: (0, j)),
        out_shape=jax.ShapeDtypeStruct((rows, n), F32),
        compiler_params=pltpu.CompilerParams(
            dimension_semantics=("arbitrary",), vmem_limit_bytes=VMEM_LIMIT),
        name="adaln",
    )(c_all, w_ada, b_ada)


def _inproj_kernel(x_ref, mod_ref, nw_ref, w_ref, *out_refs, widths):
    x = x_ref[0]
    ms = jnp.mean(x * x, axis=-1, keepdims=True)
    h = (x * lax.rsqrt(ms + EPS)) * nw_ref[...]
    h = h * (1.0 + mod_ref[0, 1:2, :]) + mod_ref[0, 0:1, :]
    hb = h.astype(BF16)
    off = 0
    for o_ref, n in zip(out_refs, widths):
        o_ref[0] = jnp.dot(hb, w_ref[:, off:off + n], preferred_element_type=F32).astype(o_ref.dtype)
        off += n


def _in_proj(x, mods, norm_w, w_cols, *, groups, per_batch_mod, tm):
    b, l, d = x.shape
    widths = tuple(n for n, _ in groups)
    assert sum(widths) == w_cols.shape[1]
    tm = min(tm, l)
    mod_map = (lambda bi, i: (bi, 0, 0)) if per_batch_mod else (lambda bi, i: (0, 0, 0))
    row_map = lambda bi, i: (bi, i, 0)
    return pl.pallas_call(
        functools.partial(_inproj_kernel, widths=widths),
        grid=(b, l // tm),
        in_specs=[
            pl.BlockSpec((1, tm, d), row_map),
            pl.BlockSpec((1, 6, d), mod_map),
            pl.BlockSpec((1, d), lambda bi, i: (0, 0)),
            pl.BlockSpec(w_cols.shape, lambda bi, i: (0, 0)),
        ],
        out_specs=[pl.BlockSpec((1, tm, n), row_map) for n in widths],
        out_shape=[jax.ShapeDtypeStruct((b, l, n), dt) for n, dt in groups],
        compiler_params=pltpu.CompilerParams(
            dimension_semantics=("arbitrary", "arbitrary"), vmem_limit_bytes=VMEM_LIMIT),
        name="in_proj",
    )(x, mods, norm_w, w_cols)


def _dn_kernel(qx_ref, kx_ref, vx_ref, qc_ref, kc_ref, vc_ref, z_ref,
               gcx_ref, gcc_ref,
               cwq_ref, cwk_ref, cwv_ref, pc_ref, onw_ref, sh_ref,
               out_ref,
               q_s, k_s, v_s, gc_s, gr_s, s_s, o_s, *, hb, seq, ctx_len):
    n_ctx = ctx_len // CHUNK
    n_lat = seq // CHUNK
    tile = CONV_TILE
    win = tile + 2 * CONV_HALO
    side_taps = [j for j in range(CONV_K) if j != CONV_K // 2]

    step = pl.program_id(0)
    slot_p = step % 2
    slot_c = 1 - slot_p

    streams = ((cwq_ref, q_s, "q"), (cwk_ref, k_s, "k"), (cwv_ref, v_s, "v"))
    lat_srcs = (qx_ref, kx_ref, vx_ref)
    ctx_srcs = (qc_ref, kc_ref, vc_ref)

    def conv_stages(srcs, n_rows, dst_off, t):
        n_tiles = n_rows // tile
        if isinstance(t, int):
            r0 = t * tile
            a0 = min(max(r0 - CONV_HALO, 0), n_rows - win)
            variant = 0 if t == 0 else (2 if t == n_tiles - 1 else 1)
        else:
            r0 = pl.multiple_of(t * tile, tile)
            a0 = pl.multiple_of(jnp.clip(r0 - CONV_HALO, 0, n_rows - win), CONV_HALO)
            variant = jnp.where(t == 0, 0, jnp.where(t == n_tiles - 1, 2, 1))
        sh = sh_ref[variant]
        zs = [jnp.dot(sh, src[0, pl.ds(a0, win), :], preferred_element_type=F32) for src in srcs]
        yield
        ys = []
        for src, z, (cw_ref, _, _) in zip(srcs, zs, streams):
            cw = cw_ref[...]
            acc = cw[CONV_K // 2:CONV_K // 2 + 1, :] * src[0, pl.ds(r0, tile), :].astype(F32)
            for n, j in enumerate(side_taps):
                acc = acc + cw[j:j + 1, :] * z[n * tile:(n + 1) * tile, :]
            ys.append(_silu(acc))
            yield
        for y, (_, dst_ref, mode) in zip(ys, streams):
            if mode != "v":
                parts = []
                for hh in range(hb):
                    seg = y[:, hh * DN_HEAD_DIM:(hh + 1) * DN_HEAD_DIM]
                    inv = lax.rsqrt(jnp.sum(seg * seg, axis=-1, keepdims=True) + EPS)
                    if mode == "q":
                        inv = inv * (DN_HEAD_DIM ** -0.5)
                    parts.append(seg * inv)
                y = parts[0] if hb == 1 else jnp.concatenate(parts, axis=1)
            dst_ref[slot_p, pl.ds(dst_off + r0, tile), :] = y.astype(dst_ref.dtype)
            yield

    def conv_tile(srcs, n_rows, dst_off, t):
        for _ in conv_stages(srcs, n_rows, dst_off, t):
            pass

    @pl.when(step == 0)
    def _():
        for t in range(ctx_len // tile):
            conv_tile(ctx_srcs, ctx_len, 0, t)

        def body(t, carry):
            conv_tile(lat_srcs, seq, ctx_len, t)
            return carry

        lax.fori_loop(0, seq // tile, body, 0)

    n_hp = hb // 2
    ri = lax.broadcasted_iota(jnp.int32, (CHUNK, CHUNK), 0)
    ci = lax.broadcasted_iota(jnp.int32, (CHUNK, CHUNK), 1)
    lower = jnp.where(ci <= ri, 1.0, 0.0).astype(BF16)
    upper = jnp.where(ci >= ri, 1.0, 0.0).astype(BF16)
    tri_c = jnp.concatenate([lower, upper], axis=1)
    a_log_c, dt_c = pc_ref[0, 0:1, :], pc_ref[0, 1:2, :]
    lane = lax.broadcasted_iota(jnp.int32, (CHUNK, LANES), 1)
    pr = lax.broadcasted_iota(jnp.int32, (8, LANES), 0)
    pl_ = lax.broadcasted_iota(jnp.int32, (8, LANES), 1)
    pick = jnp.where((pr < 2 * n_hp) & (pl_ == (pr // n_hp) * hb + 2 * (pr % n_hp)), 1.0, 0.0).astype(BF16)

    def split3(x):
        x1 = x.astype(BF16)
        r1 = x - x1.astype(F32)
        x2 = r1.astype(BF16)
        x3 = (r1 - x2.astype(F32)).astype(BF16)
        return x1, x2, x3

    def gate_stages(gc_ref, ts, off):
        xc = [gc_ref[0, t] for t in ts]
        val = [jnp.where(lane < 2 * hb, -jnp.exp(a_log_c) * _softplus(x + dt_c), 1.0 / (1.0 + jnp.exp(-x)))
               for x in xc]
        yield
        gcs = [split3(jnp.concatenate([jnp.where(lane < hb, v, 0.0),
                                       jnp.where((lane >= hb) & (lane < 2 * hb), v, 0.0)], axis=0))
               for v in val]
        gam_c = [sum(jnp.dot(tri_c, p, preferred_element_type=F32) for p in ps) for ps in gcs]
        yield
        both = [split3(jnp.concatenate([g, pltpu.roll(g, LANES - 1, 1)], axis=0)) for g in gam_c]
        gam_r = [sum(lax.dot_general(pick, p, (((1,), (1,)), ((), ())), preferred_element_type=F32)
                     for p in ps) for ps in both]
        yield
        for u, t in enumerate(ts):
            gc_s[slot_p, off + t] = jnp.where(lane < 2 * hb, gam_c[u], val[u])
            gr_s[slot_p, off + t] = gam_r[u]
        yield

    def gates_into(gc_ref, n_chunks, off):
        unroll = min(GATE_UNROLL, n_chunks)

        def body(it, carry):
            for _ in gate_stages(gc_ref, [it * unroll + u for u in range(unroll)], off):
                pass
            return carry

        lax.fori_loop(0, n_chunks // unroll, body, 0)

    @pl.when(step == 0)
    def _():
        gates_into(gcc_ref, n_ctx, 0)
        gates_into(gcx_ref, n_lat, n_ctx)

    onw = onw_ref[...]
    prow = lax.broadcasted_iota(jnp.int32, (CHUNK, LANES), 0)
    pcol = lane % CHUNK
    lo = lane < CHUNK
    eye_p = jnp.where(pcol == prow, 1.0, 0.0)
    before_eq = (pcol <= prow, pcol >= prow)
    strict = (pcol < prow, pcol > prow)
    zeros_w = jnp.zeros((CHUNK, 2 * DN_HEAD_DIM), BF16)

    def block_diag(x):
        xb = x.astype(BF16)
        zero = jnp.zeros_like(xb)
        return jnp.concatenate([jnp.where(lo, xb, zero), jnp.where(lo, zero, xb)], axis=0)

    def chunk_steps(groups, out_mode, side=None):
        def tick():
            if side is not None:
                next(side, None)

        pairs = [p for grp in groups for p in grp]
        rng = range(len(pairs))
        r0s = [pl.multiple_of(chunk * CHUNK, CHUNK) for _, _, chunk in pairs]
        wide = [slice(hp * 2 * DN_HEAD_DIM, (hp + 1) * 2 * DN_HEAD_DIM) for _, hp, _ in pairs]
        q = [q_s[slot_c, pl.ds(r0s[i], CHUNK), wide[i]].astype(F32) for i in rng]
        k = [k_s[slot_c, pl.ds(r0s[i], CHUNK), wide[i]].astype(F32) for i in rng]
        v = [v_s[slot_c, pl.ds(r0s[i], CHUNK), wide[i]].astype(F32) for i in rng]
        gcol = [gc_s[slot_c, pairs[i][2]] for i in rng]
        grow = [gr_s[slot_c, pairs[i][2]] for i in rng]
        cs = [[d * hb + 2 * hp + e for e in range(2)] for d, hp, _ in pairs]
        dirs = [d for d, _, _ in pairs]
        gam_c = [[gcol[i][:, c:c + 1] for c in cs[i]] for i in rng]
        beta_c = [[gcol[i][:, 2 * hb + c:2 * hb + c + 1] for c in cs[i]] for i in rng]
        g_tot = [[g[CHUNK - 1:CHUNK, :] if dirs[i] == 0 else g[0:1, :] for g in gam_c[i]] for i in rng]
        gam_cp = [jnp.where(lo, gam_c[i][0], gam_c[i][1]) for i in rng]
        beta_cp = [jnp.where(lo, beta_c[i][0], beta_c[i][1]) for i in rng]
        gam_rp = [grow[i][dirs[i] * n_hp + pairs[i][1]:dirs[i] * n_hp + pairs[i][1] + 1, :] for i in rng]
        decay = [jnp.exp(jnp.where(before_eq[dirs[i]], gam_cp[i] - gam_rp[i], -jnp.inf)) for i in rng]
        kb = [k[i].astype(BF16) for i in rng]
        k_bd = [jnp.concatenate([jnp.concatenate([kb[i][:, :DN_HEAD_DIM], zeros_w[:, :DN_HEAD_DIM]], axis=1),
                                 jnp.concatenate([zeros_w[:, :DN_HEAD_DIM], kb[i][:, DN_HEAD_DIM:]], axis=1)],
                                axis=0) for i in rng]
        qk_kk = [_mm_nt(jnp.concatenate([q[i], k[i]], axis=0), k_bd[i]) for i in rng]
        tick()
        qkd = [qk_kk[i][:CHUNK] * decay[i] for i in rng]
        m = [jnp.where(strict[dirs[i]], qk_kk[i][CHUNK:] * beta_cp[i] * decay[i], 0.0) for i in rng]
        t_inv = [eye_p - m[i] for i in rng]
        pw = [_mm(m[i], block_diag(m[i])) for i in rng]
        tick()
        for _ in range(4):
            res = [_mm(jnp.concatenate([pw[i], t_inv[i]], axis=0), block_diag(pw[i])) for i in rng]
            tick()
            pw = [res[i][:CHUNK] for i in rng]
            t_inv = [t_inv[i] + res[i][CHUNK:] for i in rng]
        t_inv = [t_inv[i] + _mm(t_inv[i], block_diag(pw[i])) for i in rng]
        e_gam = [[jnp.exp(g) for g in gam_c[i]] for i in rng]
        hd = [slice(0, DN_HEAD_DIM), slice(DN_HEAD_DIM, 2 * DN_HEAD_DIM)]
        rhs = [[jnp.concatenate([v[i][:, hd[e]] * beta_c[i][e], k[i][:, hd[e]] * (beta_c[i][e] * e_gam[i][e])],
                                axis=1).astype(BF16) for e in range(2)] for i in rng]
        rhs_bd = [jnp.concatenate([jnp.concatenate([rhs[i][0], zeros_w], axis=1),
                                   jnp.concatenate([zeros_w, rhs[i][1]], axis=1)], axis=0) for i in rng]
        uw = [_mm(t_inv[i], rhs_bd[i]) for i in rng]
        tick()
        u_h = [[uw[i][:, (2 * e) * DN_HEAD_DIM:(2 * e + 1) * DN_HEAD_DIM] for e in range(2)] for i in rng]
        w_h = [[uw[i][:, (2 * e + 1) * DN_HEAD_DIM:(2 * e + 2) * DN_HEAD_DIM] for e in range(2)] for i in rng]
        kdec = [[(k[i][:, hd[e]] * jnp.exp(g_tot[i][e] - gam_c[i][e])).astype(BF16) for e in range(2)] for i in rng]
        s_scale = [[jnp.exp(g) for g in g_tot[i]] for i in rng]
        if out_mode is None:
            lhs = [[w_h[i][e].astype(BF16) for e in range(2)] for i in rng]
        else:
            lhs = [[jnp.concatenate([q[i][:, hd[e]] * e_gam[i][e], w_h[i][e]], axis=0).astype(BF16)
                    for e in range(2)] for i in rng]

        state = {}
        pos = 0
        for grp in groups:
            idx = range(pos, pos + len(grp))
            pos += len(grp)
            ch = [(i, e) for i in idx for e in range(2)]
            s_old = {(i, e): (state[cs[i][e]] if cs[i][e] in state else s_s[cs[i][e]]) for i, e in ch}
            prod = {(i, e): _mm(lhs[i][e], s_old[(i, e)]) for i, e in ch}
            tick()
            v_new = {(i, e): u_h[i][e] - prod[(i, e)][-CHUNK:] for i, e in ch}
            for i, e in ch:
                state[cs[i][e]] = s_scale[i][e] * s_old[(i, e)] + _mm_tn(kdec[i][e], v_new[(i, e)])
            tick()
            if out_mode is None:
                continue
            zeros_h = zeros_w[:, :DN_HEAD_DIM]
            v_bd = {i: jnp.concatenate(
                [jnp.concatenate([v_new[(i, 0)].astype(BF16), zeros_h], axis=1),
                 jnp.concatenate([zeros_h, v_new[(i, 1)].astype(BF16)], axis=1)], axis=0) for i in idx}
            intra = {i: _mm(qkd[i], v_bd[i]) for i in idx}
            for i, e in ch:
                o = prod[(i, e)][:CHUNK] + intra[i][:, hd[e]]
                rl = pl.multiple_of(r0s[i] - ctx_len, CHUNK)
                hh = 2 * pairs[i][1] + e
                hs = slice(hh * DN_HEAD_DIM, (hh + 1) * DN_HEAD_DIM)
                if out_mode == "store":
                    o_s[pl.ds(rl, CHUNK), hs] = o
                else:
                    tot = o + o_s[pl.ds(rl, CHUNK), hs]
                    y = tot * lax.rsqrt(jnp.mean(tot * tot, axis=-1, keepdims=True) + EPS) * onw
                    zz = z_ref[0, pl.ds(rl, CHUNK), hs].astype(F32)
                    out_ref[0, pl.ds(rl, CHUNK), hs] = (y * _silu(zz)).astype(out_ref.dtype)
        for c, val in state.items():
            s_s[c] = val
        if side is not None:
            for _ in side:
                pass

    head_pairs = range(n_hp)

    def groups_at(first, last, i):
        return [(0, hp, first + i) for hp in head_pairs] + [(1, hp, last - i) for hp in head_pairs]

    @pl.when(step > 0)
    def _():
        s_s[...] = jnp.zeros(s_s.shape, F32)

        def side_work(srcs, n_rows, dst_off, gc_ref, chunk_off, j, cps):
            for u in range(cps // 2):
                yield from conv_stages(srcs, n_rows, dst_off, (cps // 2) * j + u)
            yield from gate_stages(gc_ref, [cps * j + u for u in range(cps)], chunk_off)

        cps_c = min(DN_CHUNKS_PER_STEP, n_ctx)
        for j in range(n_ctx // cps_c):
            chunk_steps([groups_at(0, n_ctx - 1, cps_c * j + u) for u in range(cps_c)], None,
                        side=side_work(ctx_srcs, ctx_len, 0, gcc_ref, 0, j, cps_c))

        cps = DN_CHUNKS_PER_STEP

        def lat_body(mode, j, carry):
            first, last = n_ctx, n_ctx + n_lat - 1
            chunk_steps([groups_at(first, last, cps * j + u) for u in range(cps)], mode,
                        side=side_work(lat_srcs, seq, ctx_len, gcx_ref, n_ctx, j, cps))
            return carry

        half = n_lat // (2 * cps)
        lax.fori_loop(0, half, functools.partial(lat_body, "store"), 0)
        lax.fori_loop(half, 2 * half, functools.partial(lat_body, "final"), 0)


def _conv_shift_matrices():
    side = [j - CONV_K // 2 for j in range(CONV_K) if j != CONV_K // 2]
    win = CONV_TILE + 2 * CONV_HALO
    sh = np.zeros((3, len(side) * CONV_TILE, win), np.float32)
    for d in range(3):
        for n, off in enumerate(side):
            for i in range(CONV_TILE):
                m = i + off + d * CONV_HALO
                if 0 <= m < win:
                    sh[d, n * CONV_TILE + i, m] = 1.0
    return jnp.asarray(sh, BF16)


def _deltanet(qkv_x, qkv_c, z_x, gcol_x, gcol_c, conv_w, pcol, onw, *, hb):
    b, seq, _ = qkv_x.shape
    ctx_len = qkv_c.shape[1]
    ng = DN_HEADS // hb
    w = hb * DN_HEAD_DIM
    n_lat, n_ctx = seq // CHUNK, ctx_len // CHUNK
    assert n_lat % (2 * DN_CHUNKS_PER_STEP) == 0 and n_ctx % 2 == 0 and DN_CHUNKS_PER_STEP % 2 == 0
    assert DN_CHUNKS_PER_STEP % n_ctx == 0 or n_ctx % DN_CHUNKS_PER_STEP == 0
    assert seq >= 2 * CONV_TILE and ctx_len >= 2 * CONV_TILE
    assert n_lat % GATE_UNROLL == 0 and GATE_UNROLL % n_ctx == 0 and hb % 2 == 0 and hb <= 8
    sh = _conv_shift_matrices()

    assert CONV_TILE == 2 * CHUNK
    assert ng == 1
    last = b - 1

    def col(part):
        return lambda s, g: (jnp.minimum(s, last), 0, part * ng + g)

    def cur(*rest):
        return lambda s, g: (jnp.maximum(s - 1, 0),) + tuple(g if r == "g" else r for r in rest)

    def cwcol(part):
        return lambda bi, g: (0, part * ng + g)

    return pl.pallas_call(
        functools.partial(_dn_kernel, hb=hb, seq=seq, ctx_len=ctx_len),
        grid=(b + 1, ng),
        in_specs=[
            pl.BlockSpec((1, seq, w), col(0)),
            pl.BlockSpec((1, seq, w), col(1)),
            pl.BlockSpec((1, seq, w), col(2)),
            pl.BlockSpec((1, ctx_len, w), col(0)),
            pl.BlockSpec((1, ctx_len, w), col(1)),
            pl.BlockSpec((1, ctx_len, w), col(2)),
            pl.BlockSpec((1, seq, w), cur(0, "g")),
            pl.BlockSpec((1, n_lat, CHUNK, LANES), lambda s, g: (jnp.minimum(s, last), 0, 0, g)),
            pl.BlockSpec((1, n_ctx, CHUNK, LANES), lambda s, g: (jnp.minimum(s, last), 0, 0, g)),
            pl.BlockSpec((CONV_K, w), cwcol(0)),
            pl.BlockSpec((CONV_K, w), cwcol(1)),
            pl.BlockSpec((CONV_K, w), cwcol(2)),
            pl.BlockSpec((1, 2, LANES), lambda bi, g: (g, 0, 0)),
            pl.BlockSpec((1, DN_HEAD_DIM), lambda bi, g: (0, 0)),
            pl.BlockSpec(sh.shape, lambda bi, g: (0, 0, 0)),
        ],
        out_specs=pl.BlockSpec((1, seq, w), cur(0, "g")),
        out_shape=jax.ShapeDtypeStruct((b, seq, DN_HEADS * DN_HEAD_DIM), BF16),
        scratch_shapes=[
            pltpu.VMEM((2, ctx_len + seq, w), BF16),
            pltpu.VMEM((2, ctx_len + seq, w), BF16),
            pltpu.VMEM((2, ctx_len + seq, w), BF16),
            pltpu.VMEM((2, n_ctx + n_lat, CHUNK, LANES), F32),
            pltpu.VMEM((2, n_ctx + n_lat, 8, LANES), F32),
            pltpu.VMEM((2 * hb, DN_HEAD_DIM, DN_HEAD_DIM), F32),
            pltpu.VMEM((seq, w), F32),
        ],
        compiler_params=pltpu.CompilerParams(
            dimension_semantics=("arbitrary", "arbitrary"), vmem_limit_bytes=VMEM_LIMIT),
        name="deltanet",
    )(qkv_x, qkv_x, qkv_x, qkv_c, qkv_c, qkv_c, z_x, gcol_x, gcol_c,
      conv_w, conv_w, conv_w, pcol, onw, sh)


def _na_kernel(q_ref, k_ref, v_ref, kc_ref, vc_ref, qw_ref, kw_ref, bias_ref, seg_ref, out_ref,
               qn_s, kn_s, knc_s, *, seq, ctx_len):
    rows = seq // GRID_W
    win_rows = min(WIN_ROWS, rows)
    n_win = win_rows * GRID_W
    n_pairs = NA_HEADS // 2

    def head_rms(x, wgt):
        ss = jnp.dot((x * x).astype(BF16), seg_ref[...], preferred_element_type=F32)
        return x * lax.rsqrt(ss * (1.0 / NA_HEAD_DIM) + EPS) * wgt

    kw = kw_ref[...]
    qw = qw_ref[...] * ((NA_HEAD_DIM ** -0.5) * LOG2E)
    tile = 256

    def norm_body(t, carry):
        r0 = pl.multiple_of(t * tile, tile)
        kn_s[pl.ds(r0, tile), :] = head_rms(k_ref[0, pl.ds(r0, tile), :].astype(F32), kw).astype(BF16)
        qn_s[pl.ds(r0, tile), :] = head_rms(q_ref[0, pl.ds(r0, tile), :].astype(F32), qw).astype(BF16)
        return carry

    lax.fori_loop(0, seq // tile, norm_body, 0)
    knc_s[...] = head_rms(kc_ref[0].astype(F32), kw).astype(BF16)

    lane2 = lax.broadcasted_iota(jnp.int32, (GRID_W, LANES), 1)
    lo2 = lane2 < NA_HEAD_DIM
    ones_w = jnp.ones((n_win, LANES), BF16)
    ones_c = jnp.ones((ctx_len, LANES), BF16)
    zero = jnp.zeros((GRID_W, LANES), BF16)

    def row_body(step, carry):
        streams = [(u, j) for u in range(NA_ROW_UNROLL) for j in range(n_pairs)]
        ns = range(len(streams))
        row = [step * NA_ROW_UNROLL + u for u, _ in streams]
        r_start = [jnp.clip(r - win_rows // 2, 0, rows - win_rows) for r in row]
        case = [r_start[n] - row[n] + (WIN_ROWS - 1) for n in ns]
        q0 = [pl.multiple_of(r * GRID_W, GRID_W) for r in row]
        k0 = [pl.multiple_of(r_start[n] * GRID_W, GRID_W) for n in ns]
        lss = [slice(j * LANES, (j + 1) * LANES) for _, j in streams]
        q2 = [qn_s[pl.ds(q0[n], GRID_W), lss[n]] for n in ns]
        qs = [jnp.concatenate([jnp.where(lo2, q2[n], zero), jnp.where(lo2, zero, q2[n])], axis=0)
              for n in ns]
        s_w = [_mm_nt(qs[n], kn_s[pl.ds(k0[n], n_win), lss[n]]) + bias_ref[case[n], streams[n][1]]
               for n in ns]
        s_c = [_mm_nt(qs[n], knc_s[:, lss[n]]) for n in ns]
        mx = [s_c[n][:, 0:LANES] for n in ns]
        for n in ns:
            for t in range(1, ctx_len // LANES):
                mx[n] = jnp.maximum(mx[n], s_c[n][:, t * LANES:(t + 1) * LANES])
            for t in range(n_win // LANES):
                mx[n] = jnp.maximum(mx[n], s_w[n][:, t * LANES:(t + 1) * LANES])
        mx = [jnp.max(mx[n], axis=-1, keepdims=True) for n in ns]
        p_w = [jnp.exp2(s_w[n] - mx[n]).astype(BF16) for n in ns]
        p_c = [jnp.exp2(s_c[n] - mx[n]).astype(BF16) for n in ns]
        o = [_mm(p_w[n], jnp.concatenate([v_ref[0, pl.ds(k0[n], n_win), lss[n]], ones_w], axis=1))
             + _mm(p_c[n], jnp.concatenate([vc_ref[0, :, lss[n]], ones_c], axis=1)) for n in ns]
        for n in ns:
            on = o[n][:, :LANES] * (1.0 / o[n][:, LANES:])
            out_ref[0, pl.ds(q0[n], GRID_W), lss[n]] = jnp.where(lo2, on[:GRID_W], on[GRID_W:]).astype(out_ref.dtype)
        return carry

    lax.fori_loop(0, rows // NA_ROW_UNROLL, row_body, 0)


def _natten(na_x, na_c, qw, kw, bias):
    b, seq, _ = na_x.shape
    ctx_len = na_c.shape[1]
    wq = NA_HEADS * NA_HEAD_DIM
    assert seq % 256 == 0 and ctx_len % LANES == 0 and (seq // GRID_W) % NA_ROW_UNROLL == 0
    head_of = np.arange(wq) // NA_HEAD_DIM
    seg = jnp.asarray(head_of[:, None] == head_of[None, :], BF16)
    return pl.pallas_call(
        functools.partial(_na_kernel, seq=seq, ctx_len=ctx_len),
        grid=(b,),
        in_specs=[
            pl.BlockSpec((1, seq, wq), lambda bi: (bi, 0, 0)),
            pl.BlockSpec((1, seq, wq), lambda bi: (bi, 0, 1)),
            pl.BlockSpec((1, seq, wq), lambda bi: (bi, 0, 2)),
            pl.BlockSpec((1, ctx_len, wq), lambda bi: (bi, 0, 0)),
            pl.BlockSpec((1, ctx_len, wq), lambda bi: (bi, 0, 1)),
            pl.BlockSpec((1, wq), lambda bi: (0, 0)),
            pl.BlockSpec((1, wq), lambda bi: (0, 0)),
            pl.BlockSpec(bias.shape, lambda bi: (0, 0, 0, 0)),
            pl.BlockSpec((wq, wq), lambda bi: (0, 0)),
        ],
        out_specs=pl.BlockSpec((1, seq, wq), lambda bi: (bi, 0, 0)),
        out_shape=jax.ShapeDtypeStruct((b, seq, wq), BF16),
        scratch_shapes=[
            pltpu.VMEM((seq, wq), BF16),
            pltpu.VMEM((seq, wq), BF16),
            pltpu.VMEM((ctx_len, wq), BF16),
        ],
        compiler_params=pltpu.CompilerParams(
            dimension_semantics=("arbitrary",), vmem_limit_bytes=VMEM_LIMIT),
        name="natten",
    )(na_x, na_x, na_x, na_c, na_c, qw, kw, bias, seg)


def _na_bias_table(rpb, rows):
    win_rows = min(WIN_ROWS, rows)
    cols = np.arange(GRID_W)
    win_start = np.clip(cols - WIN_COLS // 2, 0, GRID_W - WIN_COLS)
    kc = cols[None, :]
    valid = (kc >= win_start[:, None]) & (kc < win_start[:, None] + WIN_COLS)
    rel = np.clip(kc - cols[:, None] + WIN_COLS - 1, 0, 2 * WIN_COLS - 2)
    assert win_rows == WIN_ROWS
    pick_rel = (rel[None, :, :] == np.arange(2 * WIN_COLS - 1)[:, None, None]).astype(np.float32)
    case_i = np.arange(WIN_ROWS)[:, None] + np.arange(win_rows)[None, :]
    pick_dr = (case_i[:, :, None] == np.arange(2 * WIN_ROWS - 1)[None, None, :]).astype(np.float32)
    tab = jnp.einsum("hdr,rck,aid->ahcik", rpb * LOG2E, pick_rel, pick_dr, precision=HIGHEST)
    tab = jnp.where(valid[None, None, :, None, :], tab, -1e30)
    return tab.reshape(WIN_ROWS, NA_HEADS // 2, 2 * GRID_W, win_rows * GRID_W).astype(F32)


def _outffn_kernel(x_ref, dn_ref, na_ref, mod_ref, nw_ref, wo_ref, wi_ref, wf_ref, out_ref, *, d_ff, splits):
    half = dn_ref.shape[-1]
    attn = (jnp.dot(dn_ref[0], wo_ref[0:half, :], preferred_element_type=F32)
            + jnp.dot(na_ref[0], wo_ref[half:, :], preferred_element_type=F32))
    x1 = x_ref[0] + mod_ref[0, 2:3, :] * attn
    ms = jnp.mean(x1 * x1, axis=-1, keepdims=True)
    h = (x1 * lax.rsqrt(ms + EPS)) * nw_ref[...]
    hb = (h * (1.0 + mod_ref[0, 4:5, :]) + mod_ref[0, 3:4, :]).astype(BF16)
    acc = None
    off = 0
    for n in splits:
        gate = jnp.dot(hb, wi_ref[:, off:off + n], preferred_element_type=F32)
        up = jnp.dot(hb, wi_ref[:, d_ff + off:d_ff + off + n], preferred_element_type=F32)
        part = jnp.dot((_silu(gate) * up).astype(BF16), wf_ref[off:off + n, :], preferred_element_type=F32)
        acc = part if acc is None else acc + part
        off += n
    out_ref[0] = x1 + mod_ref[0, 5:6, :] * acc


def _out_ffn(x, dn, na, mods, norm_w, w_out, w_ffn_in, w_ffn_out, *, tm):
    b, l, d = x.shape
    half = dn.shape[-1]
    d_ff = w_ffn_out.shape[0]
    assert sum(FFN_SPLITS) == d_ff
    tm = min(tm, l)
    row_map = lambda bi, i: (bi, i, 0)
    const = lambda bi, i: (0, 0)
    single = pl.Buffered(1)
    return pl.pallas_call(
        functools.partial(_outffn_kernel, d_ff=d_ff, splits=FFN_SPLITS),
        grid=(b, l // tm),
        in_specs=[
            pl.BlockSpec((1, tm, d), row_map),
            pl.BlockSpec((1, tm, half), row_map),
            pl.BlockSpec((1, tm, half), row_map),
            pl.BlockSpec((1, 6, d), lambda bi, i: (bi, 0, 0)),
            pl.BlockSpec((1, d), const),
            pl.BlockSpec(w_out.shape, const, pipeline_mode=single),
            pl.BlockSpec(w_ffn_in.shape, const, pipeline_mode=single),
            pl.BlockSpec(w_ffn_out.shape, const, pipeline_mode=single),
        ],
        out_specs=pl.BlockSpec((1, tm, d), row_map),
        out_shape=jax.ShapeDtypeStruct((b, l, d), F32),
        compiler_params=pltpu.CompilerParams(
            dimension_semantics=("arbitrary", "arbitrary"), vmem_limit_bytes=VMEM_LIMIT),
        name="out_ffn",
    )(x, dn, na, mods, norm_w, w_out, w_ffn_in, w_ffn_out)


def kernel(x, c, ctx, c_ctx, norm1_w, norm2_w, w_ada, b_ada, w_in, dn_conv_w, dn_A_log, dn_dt_bias,
           dn_out_norm_w, na_q_norm_w, na_k_norm_w, na_rpb, w_out, w_ffn_in, w_ffn_out):
    assert w_in.shape[0] == 1, "single-layer problem"
    b, seq, d = x.shape
    hb = DN_HB
    ng = DN_HEADS // hb
    dn_w = DN_HEADS * DN_HEAD_DIM
    na_w = NA_HEADS * NA_HEAD_DIM
    rows = seq // GRID_W

    n_mod_rows = -(-(b + 1) // 8) * 8
    c_all = jnp.zeros((n_mod_rows, d), F32).at[:b].set(c).at[b].set(c_ctx)
    mod = _adaln(c_all, w_ada[0], b_ada[0][None, :])
    mod_x = mod[:b].reshape(b, 6, d)
    mod_c = mod[b:b + 1].reshape(1, 6, d)

    w0 = w_in[0]
    gate0 = 4 * dn_w
    na0 = gate0 + 4 * DN_HEADS
    gate_cols = w0[:, gate0:na0].reshape(d, 4, ng, hb)
    gate_cols = jnp.transpose(gate_cols, (0, 2, 1, 3)).reshape(d, ng, 4 * hb)
    gate_cols = jnp.pad(gate_cols, ((0, 0), (0, 0), (0, LANES - 4 * hb))).reshape(d, ng * LANES)
    w_lat = jnp.concatenate([w0[:, :gate0], w0[:, na0:], gate_cols], axis=1).astype(BF16)
    w_ctx = jnp.concatenate([w0[:, :3 * dn_w], w0[:, na0 + na_w:], gate_cols], axis=1).astype(BF16)
    n_gate = ng * LANES
    nw1 = norm1_w[0][None, :]
    qkv_x, z_x, na_x, ab_x = _in_proj(
        x, mod_x, nw1, w_lat, per_batch_mod=True, tm=IN_ROW_TILE,
        groups=((3 * dn_w, BF16), (dn_w, BF16), (3 * na_w, BF16), (n_gate, F32)))
    qkv_c, na_c, ab_c = _in_proj(
        ctx, mod_c, nw1, w_ctx, per_batch_mod=False, tm=IN_ROW_TILE,
        groups=((3 * dn_w, BF16), (2 * na_w, BF16), (n_gate, F32)))

    def group_params(p):
        return jnp.transpose(p[0].reshape(2, ng, hb), (1, 0, 2)).reshape(ng, 2 * hb)

    pg = jnp.stack([group_params(dn_A_log), group_params(dn_dt_bias)], axis=1)
    pcol = jnp.pad(pg, ((0, 0), (0, 0), (0, LANES - 2 * hb)))
    gcol_x = ab_x.reshape(b, seq // CHUNK, CHUNK, ng * LANES)
    gcol_c = ab_c.reshape(b, ctx.shape[1] // CHUNK, CHUNK, ng * LANES)
    dn_x = _deltanet(qkv_x, qkv_c, z_x, gcol_x, gcol_c, dn_conv_w[0], pcol,
                     dn_out_norm_w[0][None, :], hb=hb)

    bias = _na_bias_table(na_rpb[0], rows)
    qw = jnp.tile(na_q_norm_w[0], NA_HEADS)[None, :]
    kw = jnp.tile(na_k_norm_w[0], NA_HEADS)[None, :]
    na_o = _natten(na_x, na_c, qw, kw, bias)

    return _out_ffn(x, dn_x, na_o, mod_x, norm2_w[0][None, :], w_out[0].astype(BF16),
                    w_ffn_in[0].astype(BF16), w_ffn_out[0].astype(BF16), tm=ROW_TILE)
```

```python
import functools

import numpy as np
import jax
import jax.numpy as jnp
from jax import lax
from jax.experimental import pallas as pl
from jax.experimental.pallas import tpu as pltpu

F32 = jnp.float32
BF16 = jnp.bfloat16
HIGHEST = lax.Precision.HIGHEST

EPS = 1e-6
LOG2E = 1.4426950408889634
CHUNK = 64
CONV_K = 5
DN_HEAD_DIM = 128
DN_HEADS = 4
NA_HEAD_DIM = 64
NA_HEADS = 8
GRID_W = 64
WIN_ROWS = 8
WIN_COLS = 16
LANES = 128
VMEM_LIMIT = 56 * 1024 * 1024

DN_HB = 4
DN_CHUNKS_PER_STEP = 4
CONV_TILE = 128
CONV_HALO = 16
GATE_UNROLL = 8
NA_ROW_UNROLL = 8
IN_ROW_TILE = 1024
IN_SUB_ROWS = 512
ROW_TILE = 512
FFN_SUB_ROWS = 256
FFN_SPLITS = (1024, 1024, 768)


def _silu(x):
    return x * (1.0 / (1.0 + jnp.exp(-x)))


def _softplus(x):
    return jnp.maximum(x, 0.0) + jnp.log(1.0 + jnp.exp(-jnp.abs(x)))


def _mm(a, b):
    return jnp.dot(a.astype(BF16), b.astype(BF16), preferred_element_type=F32)


def _mm_nt(a, b):
    return lax.dot_general(a.astype(BF16), b.astype(BF16), (((1,), (1,)), ((), ())),
                           preferred_element_type=F32)


def _mm_tn(a, b):
    return lax.dot_general(a.astype(BF16), b.astype(BF16), (((0,), (0,)), ((), ())),
                           preferred_element_type=F32)


def _mm_f32(a, b):
    return lax.dot_general(a, b, (((1,), (0,)), ((), ())), precision=HIGHEST,
                           preferred_element_type=F32)


def _adaln_kernel(c_ref, w_ref, b_ref, o_ref):
    o_ref[...] = _mm_f32(_silu(c_ref[...]), w_ref[...]) + b_ref[...]


def _adaln(c_all, w_ada, b_ada):
    rows, d = c_all.shape
    n = w_ada.shape[1]
    tn = 1024
    return pl.pallas_call(
        _adaln_kernel,
        grid=(n // tn,),
        in_specs=[
            pl.BlockSpec((rows, d), lambda j: (0, 0)),
            pl.BlockSpec((d, tn), lambda j: (0, j)),
            pl.BlockSpec((1, tn), lambda j: (0, j)),
        ],
        out_specs=pl.BlockSpec((rows, tn), lambda j: (0, j)),
        out_shape=jax.ShapeDtypeStruct((rows, n), F32),
        compiler_params=pltpu.CompilerParams(
            dimension_semantics=("arbitrary",), vmem_limit_bytes=VMEM_LIMIT),
        name="adaln",
    )(c_all, w_ada, b_ada)


def _inproj_kernel(x_ref, mod_ref, nw_ref, w_ref, *out_refs, widths, n_sub):
    rows = x_ref.shape[1] // n_sub

    def prep(s):
        x = x_ref[0, s * rows:(s + 1) * rows, :]
        ms = jnp.mean(x * x, axis=-1, keepdims=True)
        h = (x * lax.rsqrt(ms + EPS)) * nw_ref[...]
        return (h * (1.0 + mod_ref[0, 1:2, :]) + mod_ref[0, 0:1, :]).astype(BF16)

    hb = prep(0)
    for s in range(n_sub):
        nxt = prep(s + 1) if s + 1 < n_sub else None
        off = 0
        for o_ref, n in zip(out_refs, widths):
            o_ref[0, s * rows:(s + 1) * rows, :] = jnp.dot(
                hb, w_ref[:, off:off + n], preferred_element_type=F32).astype(o_ref.dtype)
            off += n
        hb = nxt


def _in_proj(x, mods, norm_w, w_cols, *, groups, per_batch_mod, tm):
    b, l, d = x.shape
    widths = tuple(n for n, _ in groups)
    assert sum(widths) == w_cols.shape[1]
    tm = min(tm, l)
    mod_map = (lambda bi, i: (bi, 0, 0)) if per_batch_mod else (lambda bi, i: (0, 0, 0))
    row_map = lambda bi, i: (bi, i, 0)
    return pl.pallas_call(
        functools.partial(_inproj_kernel, widths=widths, n_sub=max(1, tm // IN_SUB_ROWS)),
        grid=(b, l // tm),
        in_specs=[
            pl.BlockSpec((1, tm, d), row_map),
            pl.BlockSpec((1, 6, d), mod_map),
            pl.BlockSpec((1, d), lambda bi, i: (0, 0)),
            pl.BlockSpec(w_cols.shape, lambda bi, i: (0, 0)),
        ],
        out_specs=[pl.BlockSpec((1, tm, n), row_map) for n in widths],
        out_shape=[jax.ShapeDtypeStruct((b, l, n), dt) for n, dt in groups],
        compiler_params=pltpu.CompilerParams(
            dimension_semantics=("arbitrary", "arbitrary"), vmem_limit_bytes=VMEM_LIMIT),
        name="in_proj",
    )(x, mods, norm_w, w_cols)


def _dn_kernel(qx_ref, kx_ref, vx_ref, qc_ref, kc_ref, vc_ref, z_ref,
               gcx_ref, gcc_ref,
               cwq_ref, cwk_ref, cwv_ref, pc_ref, onw_ref, sh_ref,
               out_ref,
               q_s, k_s, v_s, gc_s, gr_s, s_s, o_s, *, hb, seq, ctx_len):
    n_ctx = ctx_len // CHUNK
    n_lat = seq // CHUNK
    tile = CONV_TILE
    win = tile + 2 * CONV_HALO
    side_taps = [j for j in range(CONV_K) if j != CONV_K // 2]

    step = pl.program_id(0)
    slot_p = step % 2
    slot_c = 1 - slot_p

    streams = ((cwq_ref, q_s, "q"), (cwk_ref, k_s, "k"), (cwv_ref, v_s, "v"))
    lat_srcs = (qx_ref, kx_ref, vx_ref)
    ctx_srcs = (qc_ref, kc_ref, vc_ref)

    def conv_stages(srcs, n_rows, dst_off, t):
        n_tiles = n_rows // tile
        if isinstance(t, int):
            r0 = t * tile
            a0 = min(max(r0 - CONV_HALO, 0), n_rows - win)
            variant = 0 if t == 0 else (2 if t == n_tiles - 1 else 1)
        else:
            r0 = pl.multiple_of(t * tile, tile)
            a0 = pl.multiple_of(jnp.clip(r0 - CONV_HALO, 0, n_rows - win), CONV_HALO)
            variant = jnp.where(t == 0, 0, jnp.where(t == n_tiles - 1, 2, 1))
        sh = sh_ref[variant]
        zs = [jnp.dot(sh, src[0, pl.ds(a0, win), :], preferred_element_type=F32) for src in srcs]
        yield
        ys = []
        for src, z, (cw_ref, _, _) in zip(srcs, zs, streams):
            cw = cw_ref[...]
            acc = cw[CONV_K // 2:CONV_K // 2 + 1, :] * src[0, pl.ds(r0, tile), :].astype(F32)
            for n, j in enumerate(side_taps):
                acc = acc + cw[j:j + 1, :] * z[n * tile:(n + 1) * tile, :]
            ys.append(_silu(acc))
            yield
        for y, (_, dst_ref, mode) in zip(ys, streams):
            if mode != "v":
                parts = []
                for hh in range(hb):
                    seg = y[:, hh * DN_HEAD_DIM:(hh + 1) * DN_HEAD_DIM]
                    inv = lax.rsqrt(jnp.sum(seg * seg, axis=-1, keepdims=True) + EPS)
                    if mode == "q":
                        inv = inv * (DN_HEAD_DIM ** -0.5)
                    parts.append(seg * inv)
                y = parts[0] if hb == 1 else jnp.concatenate(parts, axis=1)
            dst_ref[slot_p, pl.ds(dst_off + r0, tile), :] = y.astype(dst_ref.dtype)
            yield

    def conv_tile(srcs, n_rows, dst_off, t):
        for _ in conv_stages(srcs, n_rows, dst_off, t):
            pass

    @pl.when(step == 0)
    def _():
        for t in range(ctx_len // tile):
            conv_tile(ctx_srcs, ctx_len, 0, t)

        def body(t, carry):
            conv_tile(lat_srcs, seq, ctx_len, t)
            return carry

        lax.fori_loop(0, seq // tile, body, 0)

    n_hp = hb // 2
    ri = lax.broadcasted_iota(jnp.int32, (CHUNK, CHUNK), 0)
    ci = lax.broadcasted_iota(jnp.int32, (CHUNK, CHUNK), 1)
    lower = jnp.where(ci <= ri, 1.0, 0.0).astype(BF16)
    upper = jnp.where(ci >= ri, 1.0, 0.0).astype(BF16)
    tri_c = jnp.concatenate([lower, upper], axis=1)
    a_log_c, dt_c = pc_ref[0, 0:1, :], pc_ref[0, 1:2, :]
    lane = lax.broadcasted_iota(jnp.int32, (CHUNK, LANES), 1)
    pr = lax.broadcasted_iota(jnp.int32, (8, LANES), 0)
    pl_ = lax.broadcasted_iota(jnp.int32, (8, LANES), 1)
    pick = jnp.where((pr < 2 * n_hp) & (pl_ == (pr // n_hp) * hb + 2 * (pr % n_hp)), 1.0, 0.0).astype(BF16)

    def split3(x):
        x1 = x.astype(BF16)
        r1 = x - x1.astype(F32)
        x2 = r1.astype(BF16)
        x3 = (r1 - x2.astype(F32)).astype(BF16)
        return x1, x2, x3

    def gate_stages(gc_ref, ts, off):
        xc = [gc_ref[0, t] for t in ts]
        val = [jnp.where(lane < 2 * hb, -jnp.exp(a_log_c) * _softplus(x + dt_c), 1.0 / (1.0 + jnp.exp(-x)))
               for x in xc]
        yield
        gcs = [split3(jnp.concatenate([jnp.where(lane < hb, v, 0.0),
                                       jnp.where((lane >= hb) & (lane < 2 * hb), v, 0.0)], axis=0))
               for v in val]
        gam_c = [sum(jnp.dot(tri_c, p, preferred_element_type=F32) for p in ps) for ps in gcs]
        yield
        both = [split3(jnp.concatenate([g, pltpu.roll(g, LANES - 1, 1)], axis=0)) for g in gam_c]
        gam_r = [sum(lax.dot_general(pick, p, (((1,), (1,)), ((), ())), preferred_element_type=F32)
                     for p in ps) for ps in both]
        yield
        for u, t in enumerate(ts):
            gc_s[slot_p, off + t] = jnp.where(lane < 2 * hb, gam_c[u], val[u])
            gr_s[slot_p, off + t] = gam_r[u]
        yield

    def gates_into(gc_ref, n_chunks, off):
        unroll = min(GATE_UNROLL, n_chunks)

        def body(it, carry):
            for _ in gate_stages(gc_ref, [it * unroll + u for u in range(unroll)], off):
                pass
            return carry

        lax.fori_loop(0, n_chunks // unroll, body, 0)

    @pl.when(step == 0)
    def _():
        gates_into(gcc_ref, n_ctx, 0)
        gates_into(gcx_ref, n_lat, n_ctx)

    onw = onw_ref[...]
    prow = lax.broadcasted_iota(jnp.int32, (CHUNK, LANES), 0)
    pcol = lane % CHUNK
    lo = lane < CHUNK
    eye_p = jnp.where(pcol == prow, 1.0, 0.0)
    before_eq = (pcol <= prow, pcol >= prow)
    strict = (pcol < prow, pcol > prow)
    zeros_w = jnp.zeros((CHUNK, 2 * DN_HEAD_DIM), BF16)

    def block_diag(x):
        xb = x.astype(BF16)
        zero = jnp.zeros_like(xb)
        return jnp.concatenate([jnp.where(lo, xb, zero), jnp.where(lo, zero, xb)], axis=0)

    def chunk_steps(groups, out_mode, side=None):
        def tick():
            if side is not None:
                next(side, None)

        pairs = [p for grp in groups for p in grp]
        rng = range(len(pairs))
        r0s = [pl.multiple_of(chunk * CHUNK, CHUNK) for _, _, chunk in pairs]
        wide = [slice(hp * 2 * DN_HEAD_DIM, (hp + 1) * 2 * DN_HEAD_DIM) for _, hp, _ in pairs]
        q = [q_s[slot_c, pl.ds(r0s[i], CHUNK), wide[i]].astype(F32) for i in rng]
        k = [k_s[slot_c, pl.ds(r0s[i], CHUNK), wide[i]].astype(F32) for i in rng]
        v = [v_s[slot_c, pl.ds(r0s[i], CHUNK), wide[i]].astype(F32) for i in rng]
        gcol = [gc_s[slot_c, pairs[i][2]] for i in rng]
        grow = [gr_s[slot_c, pairs[i][2]] for i in rng]
        cs = [[d * hb + 2 * hp + e for e in range(2)] for d, hp, _ in pairs]
        dirs = [d for d, _, _ in pairs]
        gam_c = [[gcol[i][:, c:c + 1] for c in cs[i]] for i in rng]
        beta_c = [[gcol[i][:, 2 * hb + c:2 * hb + c + 1] for c in cs[i]] for i in rng]
        g_tot = [[g[CHUNK - 1:CHUNK, :] if dirs[i] == 0 else g[0:1, :] for g in gam_c[i]] for i in rng]
        gam_cp = [jnp.where(lo, gam_c[i][0], gam_c[i][1]) for i in rng]
        beta_cp = [jnp.where(lo, beta_c[i][0], beta_c[i][1]) for i in rng]
        gam_rp = [grow[i][dirs[i] * n_hp + pairs[i][1]:dirs[i] * n_hp + pairs[i][1] + 1, :] for i in rng]
        decay = [jnp.exp(jnp.where(before_eq[dirs[i]], gam_cp[i] - gam_rp[i], -jnp.inf)) for i in rng]
        kb = [k[i].astype(BF16) for i in rng]
        k_bd = [jnp.concatenate([jnp.concatenate([kb[i][:, :DN_HEAD_DIM], zeros_w[:, :DN_HEAD_DIM]], axis=1),
                                 jnp.concatenate([zeros_w[:, :DN_HEAD_DIM], kb[i][:, DN_HEAD_DIM:]], axis=1)],
                                axis=0) for i in rng]
        qk_kk = [_mm_nt(jnp.concatenate([q[i], k[i]], axis=0), k_bd[i]) for i in rng]
        tick()
        qkd = [qk_kk[i][:CHUNK] * decay[i] for i in rng]
        m = [jnp.where(strict[dirs[i]], qk_kk[i][CHUNK:] * beta_cp[i] * decay[i], 0.0) for i in rng]
        t_inv = [eye_p - m[i] for i in rng]
        pw = [_mm(m[i], block_diag(m[i])) for i in rng]
        tick()
        for _ in range(4):
            res = [_mm(jnp.concatenate([pw[i], t_inv[i]], axis=0), block_diag(pw[i])) for i in rng]
            tick()
            pw = [res[i][:CHUNK] for i in rng]
            t_inv = [t_inv[i] + res[i][CHUNK:] for i in rng]
        t_inv = [t_inv[i] + _mm(t_inv[i], block_diag(pw[i])) for i in rng]
        e_gam = [[jnp.exp(g) for g in gam_c[i]] for i in rng]
        hd = [slice(0, DN_HEAD_DIM), slice(DN_HEAD_DIM, 2 * DN_HEAD_DIM)]
        rhs = [[jnp.concatenate([v[i][:, hd[e]] * beta_c[i][e], k[i][:, hd[e]] * (beta_c[i][e] * e_gam[i][e])],
                                axis=1).astype(BF16) for e in range(2)] for i in rng]
        rhs_bd = [jnp.concatenate([jnp.concatenate([rhs[i][0], zeros_w], axis=1),
                                   jnp.concatenate([zeros_w, rhs[i][1]], axis=1)], axis=0) for i in rng]
        uw = [_mm(t_inv[i], rhs_bd[i]) for i in rng]
        tick()
        u_h = [[uw[i][:, (2 * e) * DN_HEAD_DIM:(2 * e + 1) * DN_HEAD_DIM] for e in range(2)] for i in rng]
        w_h = [[uw[i][:, (2 * e + 1) * DN_HEAD_DIM:(2 * e + 2) * DN_HEAD_DIM] for e in range(2)] for i in rng]
        kdec = [[(k[i][:, hd[e]] * jnp.exp(g_tot[i][e] - gam_c[i][e])).astype(BF16) for e in range(2)] for i in rng]
        s_scale = [[jnp.exp(g) for g in g_tot[i]] for i in rng]
        if out_mode is None:
            lhs = [[w_h[i][e].astype(BF16) for e in range(2)] for i in rng]
        else:
            lhs = [[jnp.concatenate([q[i][:, hd[e]] * e_gam[i][e], w_h[i][e]], axis=0).astype(BF16)
                    for e in range(2)] for i in rng]

        state = {}
        pos = 0
        for grp in groups:
            idx = range(pos, pos + len(grp))
            pos += len(grp)
            ch = [(i, e) for i in idx for e in range(2)]
            s_old = {(i, e): (state[cs[i][e]] if cs[i][e] in state else s_s[cs[i][e]]) for i, e in ch}
            prod = {(i, e): _mm(lhs[i][e], s_old[(i, e)]) for i, e in ch}
            tick()
            v_new = {(i, e): u_h[i][e] - prod[(i, e)][-CHUNK:] for i, e in ch}
            for i, e in ch:
                state[cs[i][e]] = s_scale[i][e] * s_old[(i, e)] + _mm_tn(kdec[i][e], v_new[(i, e)])
            tick()
            if out_mode is None:
                continue
            zeros_h = zeros_w[:, :DN_HEAD_DIM]
            v_bd = {i: jnp.concatenate(
                [jnp.concatenate([v_new[(i, 0)].astype(BF16), zeros_h], axis=1),
                 jnp.concatenate([zeros_h, v_new[(i, 1)].astype(BF16)], axis=1)], axis=0) for i in idx}
            intra = {i: _mm(qkd[i], v_bd[i]) for i in idx}
            for i, e in ch:
                o = prod[(i, e)][:CHUNK] + intra[i][:, hd[e]]
                rl = pl.multiple_of(r0s[i] - ctx_len, CHUNK)
                hh = 2 * pairs[i][1] + e
                hs = slice(hh * DN_HEAD_DIM, (hh + 1) * DN_HEAD_DIM)
                if out_mode == "store":
                    o_s[pl.ds(rl, CHUNK), hs] = o
                else:
                    tot = o + o_s[pl.ds(rl, CHUNK), hs]
                    y = tot * lax.rsqrt(jnp.mean(tot * tot, axis=-1, keepdims=True) + EPS) * onw
                    zz = z_ref[0, pl.ds(rl, CHUNK), hs].astype(F32)
                    out_ref[0, pl.ds(rl, CHUNK), hs] = (y * _silu(zz)).astype(out_ref.dtype)
        for c, val in state.items():
            s_s[c] = val
        if side is not None:
            for _ in side:
                pass

    head_pairs = range(n_hp)

    def groups_at(first, last, i):
        return [(0, hp, first + i) for hp in head_pairs] + [(1, hp, last - i) for hp in head_pairs]

    @pl.when(step > 0)
    def _():
        s_s[...] = jnp.zeros(s_s.shape, F32)

        def side_work(srcs, n_rows, dst_off, gc_ref, chunk_off, j, cps):
            for u in range(cps // 2):
                yield from conv_stages(srcs, n_rows, dst_off, (cps // 2) * j + u)
            yield from gate_stages(gc_ref, [cps * j + u for u in range(cps)], chunk_off)

        cps_c = min(DN_CHUNKS_PER_STEP, n_ctx)
        for j in range(n_ctx // cps_c):
            chunk_steps([groups_at(0, n_ctx - 1, cps_c * j + u) for u in range(cps_c)], None,
                        side=side_work(ctx_srcs, ctx_len, 0, gcc_ref, 0, j, cps_c))

        cps = DN_CHUNKS_PER_STEP

        def lat_body(mode, j, carry):
            first, last = n_ctx, n_ctx + n_lat - 1
            chunk_steps([groups_at(first, last, cps * j + u) for u in range(cps)], mode,
                        side=side_work(lat_srcs, seq, ctx_len, gcx_ref, n_ctx, j, cps))
            return carry

        half = n_lat // (2 * cps)
        lax.fori_loop(0, half, functools.partial(lat_body, "store"), 0)
        lax.fori_loop(half, 2 * half, functools.partial(lat_body, "final"), 0)


def _conv_shift_matrices():
    side = [j - CONV_K // 2 for j in range(CONV_K) if j != CONV_K // 2]
    win = CONV_TILE + 2 * CONV_HALO
    sh = np.zeros((3, len(side) * CONV_TILE, win), np.float32)
    for d in range(3):
        for n, off in enumerate(side):
            for i in range(CONV_TILE):
                m = i + off + d * CONV_HALO
                if 0 <= m < win:
                    sh[d, n * CONV_TILE + i, m] = 1.0
    return jnp.asarray(sh, BF16)


def _deltanet(qkv_x, qkv_c, z_x, gcol_x, gcol_c, conv_w, pcol, onw, *, hb):
    b, seq, _ = qkv_x.shape
    ctx_len = qkv_c.shape[1]
    ng = DN_HEADS // hb
    w = hb * DN_HEAD_DIM
    n_lat, n_ctx = seq // CHUNK, ctx_len // CHUNK
    assert n_lat % (2 * DN_CHUNKS_PER_STEP) == 0 and n_ctx % 2 == 0 and DN_CHUNKS_PER_STEP % 2 == 0
    assert DN_CHUNKS_PER_STEP % n_ctx == 0 or n_ctx % DN_CHUNKS_PER_STEP == 0
    assert seq >= 2 * CONV_TILE and ctx_len >= 2 * CONV_TILE
    assert n_lat % GATE_UNROLL == 0 and GATE_UNROLL % n_ctx == 0 and hb % 2 == 0 and hb <= 8
    sh = _conv_shift_matrices()

    assert CONV_TILE == 2 * CHUNK
    assert ng == 1
    last = b - 1

    def col(part):
        return lambda s, g: (jnp.minimum(s, last), 0, part * ng + g)

    def cur(*rest):
        return lambda s, g: (jnp.maximum(s - 1, 0),) + tuple(g if r == "g" else r for r in rest)

    def cwcol(part):
        return lambda bi, g: (0, part * ng + g)

    return pl.pallas_call(
        functools.partial(_dn_kernel, hb=hb, seq=seq, ctx_len=ctx_len),
        grid=(b + 1, ng),
        in_specs=[
            pl.BlockSpec((1, seq, w), col(0)),
            pl.BlockSpec((1, seq, w), col(1)),
            pl.BlockSpec((1, seq, w), col(2)),
            pl.BlockSpec((1, ctx_len, w), col(0)),
            pl.BlockSpec((1, ctx_len, w), col(1)),
            pl.BlockSpec((1, ctx_len, w), col(2)),
            pl.BlockSpec((1, seq, w), cur(0, "g")),
            pl.BlockSpec((1, n_lat, CHUNK, LANES), lambda s, g: (jnp.minimum(s, last), 0, 0, g)),
            pl.BlockSpec((1, n_ctx, CHUNK, LANES), lambda s, g: (jnp.minimum(s, last), 0, 0, g)),
            pl.BlockSpec((CONV_K, w), cwcol(0)),
            pl.BlockSpec((CONV_K, w), cwcol(1)),
            pl.BlockSpec((CONV_K, w), cwcol(2)),
            pl.BlockSpec((1, 2, LANES), lambda bi, g: (g, 0, 0)),
            pl.BlockSpec((1, DN_HEAD_DIM), lambda bi, g: (0, 0)),
            pl.BlockSpec(sh.shape, lambda bi, g: (0, 0, 0)),
        ],
        out_specs=pl.BlockSpec((1, seq, w), cur(0, "g")),
        out_shape=jax.ShapeDtypeStruct((b, seq, DN_HEADS * DN_HEAD_DIM), BF16),
        scratch_shapes=[
            pltpu.VMEM((2, ctx_len + seq, w), BF16),
            pltpu.VMEM((2, ctx_len + seq, w), BF16),
            pltpu.VMEM((2, ctx_len + seq, w), BF16),
            pltpu.VMEM((2, n_ctx + n_lat, CHUNK, LANES), F32),
            pltpu.VMEM((2, n_ctx + n_lat, 8, LANES), F32),
            pltpu.VMEM((2 * hb, DN_HEAD_DIM, DN_HEAD_DIM), F32),
            pltpu.VMEM((seq, w), F32),
        ],
        compiler_params=pltpu.CompilerParams(
            dimension_semantics=("arbitrary", "arbitrary"), vmem_limit_bytes=VMEM_LIMIT),
        name="deltanet",
    )(qkv_x, qkv_x, qkv_x, qkv_c, qkv_c, qkv_c, z_x, gcol_x, gcol_c,
      conv_w, conv_w, conv_w, pcol, onw, sh)


def _na_kernel(q_ref, k_ref, v_ref, kc_ref, vc_ref, qw_ref, kw_ref, bias_ref, seg_ref, out_ref,
               qn_s, kn_s, knc_s, *, seq, ctx_len):
    rows = seq // GRID_W
    win_rows = min(WIN_ROWS, rows)
    n_win = win_rows * GRID_W
    n_pairs = NA_HEADS // 2

    def head_rms(x, wgt):
        ss = jnp.dot((x * x).astype(BF16), seg_ref[...], preferred_element_type=F32)
        return x * lax.rsqrt(ss * (1.0 / NA_HEAD_DIM) + EPS) * wgt

    kw = kw_ref[...]
    qw = qw_ref[...] * ((NA_HEAD_DIM ** -0.5) * LOG2E)
    tile = 256

    def norm_body(t, carry):
        r0 = pl.multiple_of(t * tile, tile)
        kn_s[pl.ds(r0, tile), :] = head_rms(k_ref[0, pl.ds(r0, tile), :].astype(F32), kw).astype(BF16)
        qn_s[pl.ds(r0, tile), :] = head_rms(q_ref[0, pl.ds(r0, tile), :].astype(F32), qw).astype(BF16)
        return carry

    lax.fori_loop(0, seq // tile, norm_body, 0)
    knc_s[...] = head_rms(kc_ref[0].astype(F32), kw).astype(BF16)

    lane2 = lax.broadcasted_iota(jnp.int32, (GRID_W, LANES), 1)
    lo2 = lane2 < NA_HEAD_DIM
    ones_w = jnp.ones((n_win, LANES), BF16)
    ones_c = jnp.ones((ctx_len, LANES), BF16)
    zero = jnp.zeros((GRID_W, LANES), BF16)

    def row_body(step, carry):
        streams = [(u, j) for u in range(NA_ROW_UNROLL) for j in range(n_pairs)]
        ns = range(len(streams))
        row = [step * NA_ROW_UNROLL + u for u, _ in streams]
        r_start = [jnp.clip(r - win_rows // 2, 0, rows - win_rows) for r in row]
        case = [r_start[n] - row[n] + (WIN_ROWS - 1) for n in ns]
        q0 = [pl.multiple_of(r * GRID_W, GRID_W) for r in row]
        k0 = [pl.multiple_of(r_start[n] * GRID_W, GRID_W) for n in ns]
        lss = [slice(j * LANES, (j + 1) * LANES) for _, j in streams]
        q2 = [qn_s[pl.ds(q0[n], GRID_W), lss[n]] for n in ns]
        qs = [jnp.concatenate([jnp.where(lo2, q2[n], zero), jnp.where(lo2, zero, q2[n])], axis=0)
              for n in ns]
        s_w = [_mm_nt(qs[n], kn_s[pl.ds(k0[n], n_win), lss[n]]) + bias_ref[case[n], streams[n][1]]
               for n in ns]
        s_c = [_mm_nt(qs[n], knc_s[:, lss[n]]) for n in ns]
        mx = [s_c[n][:, 0:LANES] for n in ns]
        for n in ns:
            for t in range(1, ctx_len // LANES):
                mx[n] = jnp.maximum(mx[n], s_c[n][:, t * LANES:(t + 1) * LANES])
            for t in range(n_win // LANES):
                mx[n] = jnp.maximum(mx[n], s_w[n][:, t * LANES:(t + 1) * LANES])
        mx = [jnp.max(mx[n], axis=-1, keepdims=True) for n in ns]
        p_w = [jnp.exp2(s_w[n] - mx[n]).astype(BF16) for n in ns]
        p_c = [jnp.exp2(s_c[n] - mx[n]).astype(BF16) for n in ns]
        o = [_mm(p_w[n], jnp.concatenate([v_ref[0, pl.ds(k0[n], n_win), lss[n]], ones_w], axis=1))
             + _mm(p_c[n], jnp.concatenate([vc_ref[0, :, lss[n]], ones_c], axis=1)) for n in ns]
        for n in ns:
            on = o[n][:, :LANES] * (1.0 / o[n][:, LANES:])
            out_ref[0, pl.ds(q0[n], GRID_W), lss[n]] = jnp.where(lo2, on[:GRID_W], on[GRID_W:]).astype(out_ref.dtype)
        return carry

    lax.fori_loop(0, rows // NA_ROW_UNROLL, row_body, 0)


def _natten(na_x, na_c, qw, kw, bias):
    b, seq, _ = na_x.shape
    ctx_len = na_c.shape[1]
    wq = NA_HEADS * NA_HEAD_DIM
    assert seq % 256 == 0 and ctx_len % LANES == 0 and (seq // GRID_W) % NA_ROW_UNROLL == 0
    head_of = np.arange(wq) // NA_HEAD_DIM
    seg = jnp.asarray(head_of[:, None] == head_of[None, :], BF16)
    return pl.pallas_call(
        functools.partial(_na_kernel, seq=seq, ctx_len=ctx_len),
        grid=(b,),
        in_specs=[
            pl.BlockSpec((1, seq, wq), lambda bi: (bi, 0, 0)),
            pl.BlockSpec((1, seq, wq), lambda bi: (bi, 0, 1)),
            pl.BlockSpec((1, seq, wq), lambda bi: (bi, 0, 2)),
            pl.BlockSpec((1, ctx_len, wq), lambda bi: (bi, 0, 0)),
            pl.BlockSpec((1, ctx_len, wq), lambda bi: (bi, 0, 1)),
            pl.BlockSpec((1, wq), lambda bi: (0, 0)),
            pl.BlockSpec((1, wq), lambda bi: (0, 0)),
            pl.BlockSpec(bias.shape, lambda bi: (0, 0, 0, 0)),
            pl.BlockSpec((wq, wq), lambda bi: (0, 0)),
        ],
        out_specs=pl.BlockSpec((1, seq, wq), lambda bi: (bi, 0, 0)),
        out_shape=jax.ShapeDtypeStruct((b, seq, wq), BF16),
        scratch_shapes=[
            pltpu.VMEM((seq, wq), BF16),
            pltpu.VMEM((seq, wq), BF16),
            pltpu.VMEM((ctx_len, wq), BF16),
        ],
        compiler_params=pltpu.CompilerParams(
            dimension_semantics=("arbitrary",), vmem_limit_bytes=VMEM_LIMIT),
        name="natten",
    )(na_x, na_x, na_x, na_c, na_c, qw, kw, bias, seg)


def _na_bias_table(rpb, rows):
    win_rows = min(WIN_ROWS, rows)
    cols = np.arange(GRID_W)
    win_start = np.clip(cols - WIN_COLS // 2, 0, GRID_W - WIN_COLS)
    kc = cols[None, :]
    valid = (kc >= win_start[:, None]) & (kc < win_start[:, None] + WIN_COLS)
    rel = np.clip(kc - cols[:, None] + WIN_COLS - 1, 0, 2 * WIN_COLS - 2)
    assert win_rows == WIN_ROWS
    pick_rel = (rel[None, :, :] == np.arange(2 * WIN_COLS - 1)[:, None, None]).astype(np.float32)
    case_i = np.arange(WIN_ROWS)[:, None] + np.arange(win_rows)[None, :]
    pick_dr = (case_i[:, :, None] == np.arange(2 * WIN_ROWS - 1)[None, None, :]).astype(np.float32)
    tab = jnp.einsum("hdr,rck,aid->ahcik", rpb * LOG2E, pick_rel, pick_dr, precision=HIGHEST)
    tab = jnp.where(valid[None, None, :, None, :], tab, -1e30)
    return tab.reshape(WIN_ROWS, NA_HEADS // 2, 2 * GRID_W, win_rows * GRID_W).astype(F32)


def _outffn_kernel(x_ref, dn_ref, na_ref, mod_ref, nw_ref, wo_ref, wi_ref, wf_ref, out_ref, *, d_ff, splits, n_sub):
    half = dn_ref.shape[-1]
    rows = x_ref.shape[1] // n_sub

    def prep(s):
        rs = slice(s * rows, (s + 1) * rows)
        attn = (jnp.dot(dn_ref[0, rs, :], wo_ref[0:half, :], preferred_element_type=F32)
                + jnp.dot(na_ref[0, rs, :], wo_ref[half:, :], preferred_element_type=F32))
        x1 = x_ref[0, rs, :] + mod_ref[0, 2:3, :] * attn
        ms = jnp.mean(x1 * x1, axis=-1, keepdims=True)
        h = (x1 * lax.rsqrt(ms + EPS)) * nw_ref[...]
        return x1, (h * (1.0 + mod_ref[0, 4:5, :]) + mod_ref[0, 3:4, :]).astype(BF16)

    cur = prep(0)
    for s in range(n_sub):
        nxt = prep(s + 1) if s + 1 < n_sub else None
        x1, hb = cur
        acc = None
        off = 0
        for n in splits:
            gate = jnp.dot(hb, wi_ref[:, off:off + n], preferred_element_type=F32)
            up = jnp.dot(hb, wi_ref[:, d_ff + off:d_ff + off + n], preferred_element_type=F32)
            part = jnp.dot((_silu(gate) * up).astype(BF16), wf_ref[off:off + n, :], preferred_element_type=F32)
            acc = part if acc is None else acc + part
            off += n
        out_ref[0, s * rows:(s + 1) * rows, :] = x1 + mod_ref[0, 5:6, :] * acc
        cur = nxt


def _out_ffn(x, dn, na, mods, norm_w, w_out, w_ffn_in, w_ffn_out, *, tm):
    b, l, d = x.shape
    half = dn.shape[-1]
    d_ff = w_ffn_out.shape[0]
    assert sum(FFN_SPLITS) == d_ff
    tm = min(tm, l)
    row_map = lambda bi, i: (bi, i, 0)
    const = lambda bi, i: (0, 0)
    single = pl.Buffered(1)
    return pl.pallas_call(
        functools.partial(_outffn_kernel, d_ff=d_ff, splits=FFN_SPLITS, n_sub=max(1, tm // FFN_SUB_ROWS)),
        grid=(b, l // tm),
        in_specs=[
            pl.BlockSpec((1, tm, d), row_map),
            pl.BlockSpec((1, tm, half), row_map),
            pl.BlockSpec((1, tm, half), row_map),
            pl.BlockSpec((1, 6, d), lambda bi, i: (bi, 0, 0)),
            pl.BlockSpec((1, d), const),
            pl.BlockSpec(w_out.shape, const, pipeline_mode=single),
            pl.BlockSpec(w_ffn_in.shape, const, pipeline_mode=single),
            pl.BlockSpec(w_ffn_out.shape, const, pipeline_mode=single),
        ],
        out_specs=pl.BlockSpec((1, tm, d), row_map),
        out_shape=jax.ShapeDtypeStruct((b, l, d), F32),
        compiler_params=pltpu.CompilerParams(
            dimension_semantics=("arbitrary", "arbitrary"), vmem_limit_bytes=VMEM_LIMIT),
        name="out_ffn",
    )(x, dn, na, mods, norm_w, w_out, w_ffn_in, w_ffn_out)


def kernel(x, c, ctx, c_ctx, norm1_w, norm2_w, w_ada, b_ada, w_in, dn_conv_w, dn_A_log, dn_dt_bias,
           dn_out_norm_w, na_q_norm_w, na_k_norm_w, na_rpb, w_out, w_ffn_in, w_ffn_out):
    assert w_in.shape[0] == 1, "single-layer problem"
    b, seq, d = x.shape
    hb = DN_HB
    ng = DN_HEADS // hb
    dn_w = DN_HEADS * DN_HEAD_DIM
    na_w = NA_HEADS * NA_HEAD_DIM
    rows = seq // GRID_W

    n_mod_rows = -(-(b + 1) // 8) * 8
    c_all = jnp.zeros((n_mod_rows, d), F32).at[:b].set(c).at[b].set(c_ctx)
    mod = _adaln(c_all, w_ada[0], b_ada[0][None, :])
    mod_x = mod[:b].reshape(b, 6, d)
    mod_c = mod[b:b + 1].reshape(1, 6, d)

    w0 = w_in[0]
    gate0 = 4 * dn_w
    na0 = gate0 + 4 * DN_HEADS
    gate_cols = w0[:, gate0:na0].reshape(d, 4, ng, hb)
    gate_cols = jnp.transpose(gate_cols, (0, 2, 1, 3)).reshape(d, ng, 4 * hb)
    gate_cols = jnp.pad(gate_cols, ((0, 0), (0, 0), (0, LANES - 4 * hb))).reshape(d, ng * LANES)
    w_lat = jnp.concatenate([w0[:, :gate0], w0[:, na0:], gate_cols], axis=1).astype(BF16)
    w_ctx = jnp.concatenate([w0[:, :3 * dn_w], w0[:, na0 + na_w:], gate_cols], axis=1).astype(BF16)
    n_gate = ng * LANES
    nw1 = norm1_w[0][None, :]
    qkv_x, z_x, na_x, ab_x = _in_proj(
        x, mod_x, nw1, w_lat, per_batch_mod=True, tm=IN_ROW_TILE,
        groups=((3 * dn_w, BF16), (dn_w, BF16), (3 * na_w, BF16), (n_gate, F32)))
    qkv_c, na_c, ab_c = _in_proj(
        ctx, mod_c, nw1, w_ctx, per_batch_mod=False, tm=IN_ROW_TILE,
        groups=((3 * dn_w, BF16), (2 * na_w, BF16), (n_gate, F32)))

    def group_params(p):
        return jnp.transpose(p[0].reshape(2, ng, hb), (1, 0, 2)).reshape(ng, 2 * hb)

    pg = jnp.stack([group_params(dn_A_log), group_params(dn_dt_bias)], axis=1)
    pcol = jnp.pad(pg, ((0, 0), (0, 0), (0, LANES - 2 * hb)))
    gcol_x = ab_x.reshape(b, seq // CHUNK, CHUNK, ng * LANES)
    gcol_c = ab_c.reshape(b, ctx.shape[1] // CHUNK, CHUNK, ng * LANES)
    dn_x = _deltanet(qkv_x, qkv_c, z_x, gcol_x, gcol_c, dn_conv_w[0], pcol,
                     dn_out_norm_w[0][None, :], hb=hb)

    bias = _na_bias_table(na_rpb[0], rows)
    qw = jnp.tile(na_q_norm_w[0], NA_HEADS)[None, :]
    kw = jnp.tile(na_k_norm_w[0], NA_HEADS)[None, :]
    na_o = _natten(na_x, na_c, qw, kw, bias)

    return _out_ffn(x, dn_x, na_o, mod_x, norm2_w[0][None, :], w_out[0].astype(BF16),
                    w_ffn_in[0].astype(BF16), w_ffn_out[0].astype(BF16), tm=ROW_TILE)
```

```python
import functools

import numpy as np
import jax
import jax.numpy as jnp
from jax import lax
from jax.experimental import pallas as pl
from jax.experimental.pallas import tpu as pltpu

F32 = jnp.float32
BF16 = jnp.bfloat16
HIGHEST = lax.Precision.HIGHEST

EPS = 1e-6
LOG2E = 1.4426950408889634
CHUNK = 64
CONV_K = 5
DN_HEAD_DIM = 128
DN_HEADS = 4
NA_HEAD_DIM = 64
NA_HEADS = 8
GRID_W = 64
WIN_ROWS = 8
WIN_COLS = 16
LANES = 128
VMEM_LIMIT = 56 * 1024 * 1024

DN_HB = 4
DN_CHUNKS_PER_STEP = 4
CONV_TILE = 128
CONV_HALO = 16
GATE_UNROLL = 8
NA_ROW_UNROLL = 8
IN_ROW_TILE = 1024
IN_SUB_ROWS = 512
ROW_TILE = 512
FFN_SUB_ROWS = 256
FFN_SPLITS = (1024, 1024, 768)


def _silu(x):
    return x * (1.0 / (1.0 + jnp.exp(-x)))


def _softplus(x):
    return jnp.maximum(x, 0.0) + jnp.log(1.0 + jnp.exp(-jnp.abs(x)))


def _mm(a, b):
    return jnp.dot(a.astype(BF16), b.astype(BF16), preferred_element_type=F32)


def _mm_nt(a, b):
    return lax.dot_general(a.astype(BF16), b.astype(BF16), (((1,), (1,)), ((), ())),
                           preferred_element_type=F32)


def _mm_tn(a, b):
    return lax.dot_general(a.astype(BF16), b.astype(BF16), (((0,), (0,)), ((), ())),
                           preferred_element_type=F32)


def _mm_f32(a, b):
    return lax.dot_general(a, b, (((1,), (0,)), ((), ())), precision=HIGHEST,
                           preferred_element_type=F32)


def _adaln_kernel(c_ref, w_ref, b_ref, o_ref):
    o_ref[...] = _mm_f32(_silu(c_ref[...]), w_ref[...]) + b_ref[...]


def _adaln(c_all, w_ada, b_ada):
    rows, d = c_all.shape
    n = w_ada.shape[1]
    tn = 1024
    return pl.pallas_call(
        _adaln_kernel,
        grid=(n // tn,),
        in_specs=[
            pl.BlockSpec((rows, d), lambda j: (0, 0)),
            pl.BlockSpec((d, tn), lambda j: (0, j)),
            pl.BlockSpec((1, tn), lambda j: (0, j)),
        ],
        out_specs=pl.BlockSpec((rows, tn), lambda j: (0, j)),
        out_shape=jax.ShapeDtypeStruct((rows, n), F32),
        compiler_params=pltpu.CompilerParams(
            dimension_semantics=("arbitrary",), vmem_limit_bytes=VMEM_LIMIT),
        name="adaln",
    )(c_all, w_ada, b_ada)


def _inproj_kernel(x_ref, mod_ref, nw_ref, w_ref, *out_refs, widths, n_sub):
    rows = x_ref.shape[1] // n_sub

    def prep(s):
        x = x_ref[0, s * rows:(s + 1) * rows, :]
        ms = jnp.mean(x * x, axis=-1, keepdims=True)
        h = (x * lax.rsqrt(ms + EPS)) * nw_ref[...]
        return (h * (1.0 + mod_ref[0, 1:2, :]) + mod_ref[0, 0:1, :]).astype(BF16)

    hb = prep(0)
    for s in range(n_sub):
        nxt = prep(s + 1) if s + 1 < n_sub else None
        off = 0
        for o_ref, n in zip(out_refs, widths):
            o_ref[0, s * rows:(s + 1) * rows, :] = jnp.dot(
                hb, w_ref[:, off:off + n], preferred_element_type=F32).astype(o_ref.dtype)
            off += n
        hb = nxt


def _in_proj(x, mods, norm_w, w_cols, *, groups, per_batch_mod, tm):
    b, l, d = x.shape
    widths = tuple(n for n, _ in groups)
    assert sum(widths) == w_cols.shape[1]
    tm = min(tm, l)
    mod_map = (lambda bi, i: (bi, 0, 0)) if per_batch_mod else (lambda bi, i: (0, 0, 0))
    row_map = lambda bi, i: (bi, i, 0)
    return pl.pallas_call(
        functools.partial(_inproj_kernel, widths=widths, n_sub=max(1, tm // IN_SUB_ROWS)),
        grid=(b, l // tm),
        in_specs=[
            pl.BlockSpec((1, tm, d), row_map),
            pl.BlockSpec((1, 6, d), mod_map),
            pl.BlockSpec((1, d), lambda bi, i: (0, 0)),
            pl.BlockSpec(w_cols.shape, lambda bi, i: (0, 0)),
        ],
        out_specs=[pl.BlockSpec((1, tm, n), row_map) for n in widths],
        out_shape=[jax.ShapeDtypeStruct((b, l, n), dt) for n, dt in groups],
        compiler_params=pltpu.CompilerParams(
            dimension_semantics=("arbitrary", "arbitrary"), vmem_limit_bytes=VMEM_LIMIT),
        name="in_proj",
    )(x, mods, norm_w, w_cols)


def _dn_kernel(qx_ref, kx_ref, vx_ref, qc_ref, kc_ref, vc_ref, z_ref,
               gcx_ref, gcc_ref,
               cwq_ref, cwk_ref, cwv_ref, pc_ref, onw_ref, sh_ref,
               out_ref,
               q_s, k_s, v_s, gc_s, gr_s, s_s, o_s, *, hb, seq, ctx_len):
    n_ctx = ctx_len // CHUNK
    n_lat = seq // CHUNK
    tile = CONV_TILE
    win = tile + 2 * CONV_HALO
    side_taps = [j for j in range(CONV_K) if j != CONV_K // 2]

    step = pl.program_id(0)
    slot_p = step % 2
    slot_c = 1 - slot_p

    streams = ((cwq_ref, q_s, "q"), (cwk_ref, k_s, "k"), (cwv_ref, v_s, "v"))
    lat_srcs = (qx_ref, kx_ref, vx_ref)
    ctx_srcs = (qc_ref, kc_ref, vc_ref)

    def conv_stages(srcs, n_rows, dst_off, t):
        n_tiles = n_rows // tile
        if isinstance(t, int):
            r0 = t * tile
            a0 = min(max(r0 - CONV_HALO, 0), n_rows - win)
            variant = 0 if t == 0 else (2 if t == n_tiles - 1 else 1)
        else:
            r0 = pl.multiple_of(t * tile, tile)
            a0 = pl.multiple_of(jnp.clip(r0 - CONV_HALO, 0, n_rows - win), CONV_HALO)
            variant = jnp.where(t == 0, 0, jnp.where(t == n_tiles - 1, 2, 1))
        sh = sh_ref[variant]
        zs = [jnp.dot(sh, src[0, pl.ds(a0, win), :], preferred_element_type=F32) for src in srcs]
        yield
        ys = []
        for src, z, (cw_ref, _, _) in zip(srcs, zs, streams):
            cw = cw_ref[...]
            acc = cw[CONV_K // 2:CONV_K // 2 + 1, :] * src[0, pl.ds(r0, tile), :].astype(F32)
            for n, j in enumerate(side_taps):
                acc = acc + cw[j:j + 1, :] * z[n * tile:(n + 1) * tile, :]
            ys.append(_silu(acc))
            yield
        for y, (_, dst_ref, mode) in zip(ys, streams):
            if mode != "v":
                parts = []
                for hh in range(hb):
                    seg = y[:, hh * DN_HEAD_DIM:(hh + 1) * DN_HEAD_DIM]
                    inv = lax.rsqrt(jnp.sum(seg * seg, axis=-1, keepdims=True) + EPS)
                    if mode == "q":
                        inv = inv * (DN_HEAD_DIM ** -0.5)
                    parts.append(seg * inv)
                y = parts[0] if hb == 1 else jnp.concatenate(parts, axis=1)
            dst_ref[slot_p, pl.ds(dst_off + r0, tile), :] = y.astype(dst_ref.dtype)
            yield

    def conv_tile(srcs, n_rows, dst_off, t):
        for _ in conv_stages(srcs, n_rows, dst_off, t):
            pass

    @pl.when(step == 0)
    def _():
        for t in range(ctx_len // tile):
            conv_tile(ctx_srcs, ctx_len, 0, t)

        def body(t, carry):
            conv_tile(lat_srcs, seq, ctx_len, t)
            return carry

        lax.fori_loop(0, seq // tile, body, 0)

    n_hp = hb // 2
    ri = lax.broadcasted_iota(jnp.int32, (CHUNK, CHUNK), 0)
    ci = lax.broadcasted_iota(jnp.int32, (CHUNK, CHUNK), 1)
    lower = jnp.where(ci <= ri, 1.0, 0.0).astype(BF16)
    upper = jnp.where(ci >= ri, 1.0, 0.0).astype(BF16)
    tri_c = jnp.concatenate([lower, upper], axis=1)
    a_log_c, dt_c = pc_ref[0, 0:1, :], pc_ref[0, 1:2, :]
    lane = lax.broadcasted_iota(jnp.int32, (CHUNK, LANES), 1)
    pr = lax.broadcasted_iota(jnp.int32, (8, LANES), 0)
    pl_ = lax.broadcasted_iota(jnp.int32, (8, LANES), 1)
    pick = jnp.where((pr < 2 * n_hp) & (pl_ == (pr // n_hp) * hb + 2 * (pr % n_hp)), 1.0, 0.0).astype(BF16)

    def split3(x):
        x1 = x.astype(BF16)
        r1 = x - x1.astype(F32)
        x2 = r1.astype(BF16)
        x3 = (r1 - x2.astype(F32)).astype(BF16)
        return x1, x2, x3

    def gate_stages(gc_ref, ts, off):
        xc = [gc_ref[0, t] for t in ts]
        val = [jnp.where(lane < 2 * hb, -jnp.exp(a_log_c) * _softplus(x + dt_c), 1.0 / (1.0 + jnp.exp(-x)))
               for x in xc]
        yield
        gcs = [split3(jnp.concatenate([jnp.where(lane < hb, v, 0.0),
                                       jnp.where((lane >= hb) & (lane < 2 * hb), v, 0.0)], axis=0))
               for v in val]
        gam_c = [sum(jnp.dot(tri_c, p, preferred_element_type=F32) for p in ps) for ps in gcs]
        yield
        both = [split3(jnp.concatenate([g, pltpu.roll(g, LANES - 1, 1)], axis=0)) for g in gam_c]
        gam_r = [sum(lax.dot_general(pick, p, (((1,), (1,)), ((), ())), preferred_element_type=F32)
                     for p in ps) for ps in both]
        yield
        for u, t in enumerate(ts):
            gc_s[slot_p, off + t] = jnp.where(lane < 2 * hb, gam_c[u], val[u])
            gr_s[slot_p, off + t] = gam_r[u]
        yield

    def gates_into(gc_ref, n_chunks, off):
        unroll = min(GATE_UNROLL, n_chunks)

        def body(it, carry):
            for _ in gate_stages(gc_ref, [it * unroll + u for u in range(unroll)], off):
                pass
            return carry

        lax.fori_loop(0, n_chunks // unroll, body, 0)

    @pl.when(step == 0)
    def _():
        gates_into(gcc_ref, n_ctx, 0)
        gates_into(gcx_ref, n_lat, n_ctx)

    onw = onw_ref[...]
    prow = lax.broadcasted_iota(jnp.int32, (CHUNK, LANES), 0)
    pcol = lane % CHUNK
    lo = lane < CHUNK
    eye_p = jnp.where(pcol == prow, 1.0, 0.0)
    before_eq = (pcol <= prow, pcol >= prow)
    strict = (pcol < prow, pcol > prow)
    zeros_w = jnp.zeros((CHUNK, 2 * DN_HEAD_DIM), BF16)

    def block_diag(x):
        xb = x.astype(BF16)
        zero = jnp.zeros_like(xb)
        return jnp.concatenate([jnp.where(lo, xb, zero), jnp.where(lo, zero, xb)], axis=0)

    def chunk_steps(groups, out_mode, side=None):
        def tick():
            if side is not None:
                next(side, None)

        pairs = [p for grp in groups for p in grp]
        rng = range(len(pairs))
        r0s = [pl.multiple_of(chunk * CHUNK, CHUNK) for _, _, chunk in pairs]
        wide = [slice(hp * 2 * DN_HEAD_DIM, (hp + 1) * 2 * DN_HEAD_DIM) for _, hp, _ in pairs]
        q = [q_s[slot_c, pl.ds(r0s[i], CHUNK), wide[i]].astype(F32) for i in rng]
        k = [k_s[slot_c, pl.ds(r0s[i], CHUNK), wide[i]].astype(F32) for i in rng]
        v = [v_s[slot_c, pl.ds(r0s[i], CHUNK), wide[i]].astype(F32) for i in rng]
        gcol = [gc_s[slot_c, pairs[i][2]] for i in rng]
        grow = [gr_s[slot_c, pairs[i][2]] for i in rng]
        cs = [[d * hb + 2 * hp + e for e in range(2)] for d, hp, _ in pairs]
        dirs = [d for d, _, _ in pairs]
        gam_c = [[gcol[i][:, c:c + 1] for c in cs[i]] for i in rng]
        beta_c = [[gcol[i][:, 2 * hb + c:2 * hb + c + 1] for c in cs[i]] for i in rng]
        g_tot = [[g[CHUNK - 1:CHUNK, :] if dirs[i] == 0 else g[0:1, :] for g in gam_c[i]] for i in rng]
        gam_cp = [jnp.where(lo, gam_c[i][0], gam_c[i][1]) for i in rng]
        beta_cp = [jnp.where(lo, beta_c[i][0], beta_c[i][1]) for i in rng]
        gam_rp = [grow[i][dirs[i] * n_hp + pairs[i][1]:dirs[i] * n_hp + pairs[i][1] + 1, :] for i in rng]
        decay = [jnp.exp(jnp.where(before_eq[dirs[i]], gam_cp[i] - gam_rp[i], -jnp.inf)) for i in rng]
        kb = [k[i].astype(BF16) for i in rng]
        k_bd = [jnp.concatenate([jnp.concatenate([kb[i][:, :DN_HEAD_DIM], zeros_w[:, :DN_HEAD_DIM]], axis=1),
                                 jnp.concatenate([zeros_w[:, :DN_HEAD_DIM], kb[i][:, DN_HEAD_DIM:]], axis=1)],
                                axis=0) for i in rng]
        qk_kk = [_mm_nt(jnp.concatenate([q[i], k[i]], axis=0), k_bd[i]) for i in rng]
        tick()
        qkd = [qk_kk[i][:CHUNK] * decay[i] for i in rng]
        m = [jnp.where(strict[dirs[i]], qk_kk[i][CHUNK:] * beta_cp[i] * decay[i], 0.0) for i in rng]
        t_inv = [eye_p - m[i] for i in rng]
        pw = [_mm(m[i], block_diag(m[i])) for i in rng]
        tick()
        for _ in range(4):
            res = [_mm(jnp.concatenate([pw[i], t_inv[i]], axis=0), block_diag(pw[i])) for i in rng]
            tick()
            pw = [res[i][:CHUNK] for i in rng]
            t_inv = [t_inv[i] + res[i][CHUNK:] for i in rng]
        t_inv = [t_inv[i] + _mm(t_inv[i], block_diag(pw[i])) for i in rng]
        e_gam = [[jnp.exp(g) for g in gam_c[i]] for i in rng]
        hd = [slice(0, DN_HEAD_DIM), slice(DN_HEAD_DIM, 2 * DN_HEAD_DIM)]
        rhs = [[jnp.concatenate([v[i][:, hd[e]] * beta_c[i][e], k[i][:, hd[e]] * (beta_c[i][e] * e_gam[i][e])],
                                axis=1).astype(BF16) for e in range(2)] for i in rng]
        rhs_bd = [jnp.concatenate([jnp.concatenate([rhs[i][0], zeros_w], axis=1),
                                   jnp.concatenate([zeros_w, rhs[i][1]], axis=1)], axis=0) for i in rng]
        uw = [_mm(t_inv[i], rhs_bd[i]) for i in rng]
        tick()
        u_h = [[uw[i][:, (2 * e) * DN_HEAD_DIM:(2 * e + 1) * DN_HEAD_DIM] for e in range(2)] for i in rng]
        w_h = [[uw[i][:, (2 * e + 1) * DN_HEAD_DIM:(2 * e + 2) * DN_HEAD_DIM] for e in range(2)] for i in rng]
        kdec = [[(k[i][:, hd[e]] * jnp.exp(g_tot[i][e] - gam_c[i][e])).astype(BF16) for e in range(2)] for i in rng]
        s_scale = [[jnp.exp(g) for g in g_tot[i]] for i in rng]
        if out_mode is None:
            lhs = [[w_h[i][e].astype(BF16) for e in range(2)] for i in rng]
        else:
            lhs = [[jnp.concatenate([q[i][:, hd[e]] * e_gam[i][e], w_h[i][e]], axis=0).astype(BF16)
                    for e in range(2)] for i in rng]

        state = {}
        pos = 0
        for grp in groups:
            idx = range(pos, pos + len(grp))
            pos += len(grp)
            ch = [(i, e) for i in idx for e in range(2)]
            s_old = {(i, e): (state[cs[i][e]] if cs[i][e] in state else s_s[cs[i][e]]) for i, e in ch}
            prod = {(i, e): _mm(lhs[i][e], s_old[(i, e)]) for i, e in ch}
            tick()
            v_new = {(i, e): u_h[i][e] - prod[(i, e)][-CHUNK:] for i, e in ch}
            for i, e in ch:
                state[cs[i][e]] = s_scale[i][e] * s_old[(i, e)] + _mm_tn(kdec[i][e], v_new[(i, e)])
            tick()
            if out_mode is None:
                continue
            zeros_h = zeros_w[:, :DN_HEAD_DIM]
            v_bd = {i: jnp.concatenate(
                [jnp.concatenate([v_new[(i, 0)].astype(BF16), zeros_h], axis=1),
                 jnp.concatenate([zeros_h, v_new[(i, 1)].astype(BF16)], axis=1)], axis=0) for i in idx}
            intra = {i: _mm(qkd[i], v_bd[i]) for i in idx}
            for i, e in ch:
                o = prod[(i, e)][:CHUNK] + intra[i][:, hd[e]]
                rl = pl.multiple_of(r0s[i] - ctx_len, CHUNK)
                hh = 2 * pairs[i][1] + e
                hs = slice(hh * DN_HEAD_DIM, (hh + 1) * DN_HEAD_DIM)
                if out_mode == "store":
                    o_s[pl.ds(rl, CHUNK), hs] = o
                else:
                    tot = o + o_s[pl.ds(rl, CHUNK), hs]
                    y = tot * lax.rsqrt(jnp.mean(tot * tot, axis=-1, keepdims=True) + EPS) * onw
                    zz = z_ref[0, pl.ds(rl, CHUNK), hs].astype(F32)
                    out_ref[0, pl.ds(rl, CHUNK), hs] = (y * _silu(zz)).astype(out_ref.dtype)
        for c, val in state.items():
            s_s[c] = val
        if side is not None:
            for _ in side:
                pass

    head_pairs = range(n_hp)

    def groups_at(first, last, i):
        return [(0, hp, first + i) for hp in head_pairs] + [(1, hp, last - i) for hp in head_pairs]

    @pl.when(step > 0)
    def _():
        s_s[...] = jnp.zeros(s_s.shape, F32)

        def side_work(srcs, n_rows, dst_off, gc_ref, chunk_off, j, cps):
            for u in range(cps // 2):
                yield from conv_stages(srcs, n_rows, dst_off, (cps // 2) * j + u)
            yield from gate_stages(gc_ref, [cps * j + u for u in range(cps)], chunk_off)

        cps_c = min(DN_CHUNKS_PER_STEP, n_ctx)
        for j in range(n_ctx // cps_c):
            chunk_steps([groups_at(0, n_ctx - 1, cps_c * j + u) for u in range(cps_c)], None,
                        side=side_work(ctx_srcs, ctx_len, 0, gcc_ref, 0, j, cps_c))

        cps = DN_CHUNKS_PER_STEP

        def lat_body(mode, j, carry):
            first, last = n_ctx, n_ctx + n_lat - 1
            chunk_steps([groups_at(first, last, cps * j + u) for u in range(cps)], mode,
                        side=side_work(lat_srcs, seq, ctx_len, gcx_ref, n_ctx, j, cps))
            return carry

        half = n_lat // (2 * cps)
        lax.fori_loop(0, half, functools.partial(lat_body, "store"), 0)
        lax.fori_loop(half, 2 * half, functools.partial(lat_body, "final"), 0)


def _conv_shift_matrices():
    side = [j - CONV_K // 2 for j in range(CONV_K) if j != CONV_K // 2]
    win = CONV_TILE + 2 * CONV_HALO
    sh = np.zeros((3, len(side) * CONV_TILE, win), np.float32)
    for d in range(3):
        for n, off in enumerate(side):
            for i in range(CONV_TILE):
                m = i + off + d * CONV_HALO
                if 0 <= m < win:
                    sh[d, n * CONV_TILE + i, m] = 1.0
    return jnp.asarray(sh, BF16)


def _deltanet(qkv_x, qkv_c, z_x, gcol_x, gcol_c, conv_w, pcol, onw, *, hb):
    b, seq, _ = qkv_x.shape
    ctx_len = qkv_c.shape[1]
    ng = DN_HEADS // hb
    w = hb * DN_HEAD_DIM
    n_lat, n_ctx = seq // CHUNK, ctx_len // CHUNK
    assert n_lat % (2 * DN_CHUNKS_PER_STEP) == 0 and n_ctx % 2 == 0 and DN_CHUNKS_PER_STEP % 2 == 0
    assert DN_CHUNKS_PER_STEP % n_ctx == 0 or n_ctx % DN_CHUNKS_PER_STEP == 0
    assert seq >= 2 * CONV_TILE and ctx_len >= 2 * CONV_TILE
    assert n_lat % GATE_UNROLL == 0 and GATE_UNROLL % n_ctx == 0 and hb % 2 == 0 and hb <= 8
    sh = _conv_shift_matrices()

    assert CONV_TILE == 2 * CHUNK
    assert ng == 1
    last = b - 1

    def col(part):
        return lambda s, g: (jnp.minimum(s, last), 0, part * ng + g)

    def cur(*rest):
        return lambda s, g: (jnp.maximum(s - 1, 0),) + tuple(g if r == "g" else r for r in rest)

    def cwcol(part):
        return lambda bi, g: (0, part * ng + g)

    return pl.pallas_call(
        functools.partial(_dn_kernel, hb=hb, seq=seq, ctx_len=ctx_len),
        grid=(b + 1, ng),
        in_specs=[
            pl.BlockSpec((1, seq, w), col(0)),
            pl.BlockSpec((1, seq, w), col(1)),
            pl.BlockSpec((1, seq, w), col(2)),
            pl.BlockSpec((1, ctx_len, w), col(0)),
            pl.BlockSpec((1, ctx_len, w), col(1)),
            pl.BlockSpec((1, ctx_len, w), col(2)),
            pl.BlockSpec((1, seq, w), cur(0, "g")),
            pl.BlockSpec((1, n_lat, CHUNK, LANES), lambda s, g: (jnp.minimum(s, last), 0, 0, g)),
            pl.BlockSpec((1, n_ctx, CHUNK, LANES), lambda s, g: (jnp.minimum(s, last), 0, 0, g)),
            pl.BlockSpec((CONV_K, w), cwcol(0)),
            pl.BlockSpec((CONV_K, w), cwcol(1)),
            pl.BlockSpec((CONV_K, w), cwcol(2)),
            pl.BlockSpec((1, 2, LANES), lambda bi, g: (g, 0, 0)),
            pl.BlockSpec((1, DN_HEAD_DIM), lambda bi, g: (0, 0)),
            pl.BlockSpec(sh.shape, lambda bi, g: (0, 0, 0)),
        ],
        out_specs=pl.BlockSpec((1, seq, w), cur(0, "g")),
        out_shape=jax.ShapeDtypeStruct((b, seq, DN_HEADS * DN_HEAD_DIM), BF16),
        scratch_shapes=[
            pltpu.VMEM((2, ctx_len + seq, w), BF16),
            pltpu.VMEM((2, ctx_len + seq, w), BF16),
            pltpu.VMEM((2, ctx_len + seq, w), BF16),
            pltpu.VMEM((2, n_ctx + n_lat, CHUNK, LANES), F32),
            pltpu.VMEM((2, n_ctx + n_lat, 8, LANES), F32),
            pltpu.VMEM((2 * hb, DN_HEAD_DIM, DN_HEAD_DIM), F32),
            pltpu.VMEM((seq, w), F32),
        ],
        compiler_params=pltpu.CompilerParams(
            dimension_semantics=("arbitrary", "arbitrary"), vmem_limit_bytes=VMEM_LIMIT),
        name="deltanet",
    )(qkv_x, qkv_x, qkv_x, qkv_c, qkv_c, qkv_c, z_x, gcol_x, gcol_c,
      conv_w, conv_w, conv_w, pcol, onw, sh)


def _na_kernel(q_ref, k_ref, v_ref, kc_ref, vc_ref, qw_ref, kw_ref, bias_ref, seg_ref, out_ref,
               qn_s, kn_s, knc_s, *, seq, ctx_len):
    rows = seq // GRID_W
    win_rows = min(WIN_ROWS, rows)
    n_win = win_rows * GRID_W
    n_pairs = NA_HEADS // 2

    def head_rms(x, wgt):
        ss = jnp.dot((x * x).astype(BF16), seg_ref[...], preferred_element_type=F32)
        return x * lax.rsqrt(ss * (1.0 / NA_HEAD_DIM) + EPS) * wgt

    kw = kw_ref[...]
    qw = qw_ref[...] * ((NA_HEAD_DIM ** -0.5) * LOG2E)
    tile = 256

    def norm_body(t, carry):
        r0 = pl.multiple_of(t * tile, tile)
        kn_s[pl.ds(r0, tile), :] = head_rms(k_ref[0, pl.ds(r0, tile), :].astype(F32), kw).astype(BF16)
        qn_s[pl.ds(r0, tile), :] = head_rms(q_ref[0, pl.ds(r0, tile), :].astype(F32), qw).astype(BF16)
        return carry

    lax.fori_loop(0, seq // tile, norm_body, 0)
    knc_s[...] = head_rms(kc_ref[0].astype(F32), kw).astype(BF16)

    lane2 = lax.broadcasted_iota(jnp.int32, (GRID_W, LANES), 1)
    lo2 = lane2 < NA_HEAD_DIM
    ones_w = jnp.ones((n_win, LANES), BF16)
    ones_c = jnp.ones((ctx_len, LANES), BF16)
    zero = jnp.zeros((GRID_W, LANES), BF16)

    def row_body(step, carry):
        streams = [(u, j) for u in range(NA_ROW_UNROLL) for j in range(n_pairs)]
        ns = range(len(streams))
        row = [step * NA_ROW_UNROLL + u for u, _ in streams]
        r_start = [jnp.clip(r - win_rows // 2, 0, rows - win_rows) for r in row]
        case = [r_start[n] - row[n] + (WIN_ROWS - 1) for n in ns]
        b_copy = [case[n] % 2 for n in ns]
        b_lane = [pl.multiple_of((case[n] // 2) * LANES, LANES) for n in ns]
        q0 = [pl.multiple_of(r * GRID_W, GRID_W) for r in row]
        k0 = [pl.multiple_of(r_start[n] * GRID_W, GRID_W) for n in ns]
        lss = [slice(j * LANES, (j + 1) * LANES) for _, j in streams]
        q2 = [qn_s[pl.ds(q0[n], GRID_W), lss[n]] for n in ns]
        qs = [jnp.concatenate([jnp.where(lo2, q2[n], zero), jnp.where(lo2, zero, q2[n])], axis=0)
              for n in ns]
        s_w = [_mm_nt(qs[n], kn_s[pl.ds(k0[n], n_win), lss[n]])
               + bias_ref[b_copy[n], streams[n][1], :, pl.ds(b_lane[n], n_win)] for n in ns]
        s_c = [_mm_nt(qs[n], knc_s[:, lss[n]]) for n in ns]
        mx = [s_c[n][:, 0:LANES] for n in ns]
        for n in ns:
            for t in range(1, ctx_len // LANES):
                mx[n] = jnp.maximum(mx[n], s_c[n][:, t * LANES:(t + 1) * LANES])
            for t in range(n_win // LANES):
                mx[n] = jnp.maximum(mx[n], s_w[n][:, t * LANES:(t + 1) * LANES])
        mx = [jnp.max(mx[n], axis=-1, keepdims=True) for n in ns]
        p_w = [jnp.exp2(s_w[n] - mx[n]).astype(BF16) for n in ns]
        p_c = [jnp.exp2(s_c[n] - mx[n]).astype(BF16) for n in ns]
        o = [_mm(p_w[n], jnp.concatenate([v_ref[0, pl.ds(k0[n], n_win), lss[n]], ones_w], axis=1))
             + _mm(p_c[n], jnp.concatenate([vc_ref[0, :, lss[n]], ones_c], axis=1)) for n in ns]
        for n in ns:
            on = o[n][:, :LANES] * (1.0 / o[n][:, LANES:])
            out_ref[0, pl.ds(q0[n], GRID_W), lss[n]] = jnp.where(lo2, on[:GRID_W], on[GRID_W:]).astype(out_ref.dtype)
        return carry

    lax.fori_loop(0, rows // NA_ROW_UNROLL, row_body, 0)


def _natten(na_x, na_c, qw, kw, bias):
    b, seq, _ = na_x.shape
    ctx_len = na_c.shape[1]
    wq = NA_HEADS * NA_HEAD_DIM
    assert seq % 256 == 0 and ctx_len % LANES == 0 and (seq // GRID_W) % NA_ROW_UNROLL == 0
    head_of = np.arange(wq) // NA_HEAD_DIM
    seg = jnp.asarray(head_of[:, None] == head_of[None, :], BF16)
    return pl.pallas_call(
        functools.partial(_na_kernel, seq=seq, ctx_len=ctx_len),
        grid=(b,),
        in_specs=[
            pl.BlockSpec((1, seq, wq), lambda bi: (bi, 0, 0)),
            pl.BlockSpec((1, seq, wq), lambda bi: (bi, 0, 1)),
            pl.BlockSpec((1, seq, wq), lambda bi: (bi, 0, 2)),
            pl.BlockSpec((1, ctx_len, wq), lambda bi: (bi, 0, 0)),
            pl.BlockSpec((1, ctx_len, wq), lambda bi: (bi, 0, 1)),
            pl.BlockSpec((1, wq), lambda bi: (0, 0)),
            pl.BlockSpec((1, wq), lambda bi: (0, 0)),
            pl.BlockSpec(bias.shape, lambda bi: (0, 0, 0, 0)),
            pl.BlockSpec((wq, wq), lambda bi: (0, 0)),
        ],
        out_specs=pl.BlockSpec((1, seq, wq), lambda bi: (bi, 0, 0)),
        out_shape=jax.ShapeDtypeStruct((b, seq, wq), BF16),
        scratch_shapes=[
            pltpu.VMEM((seq, wq), BF16),
            pltpu.VMEM((seq, wq), BF16),
            pltpu.VMEM((ctx_len, wq), BF16),
        ],
        compiler_params=pltpu.CompilerParams(
            dimension_semantics=("arbitrary",), vmem_limit_bytes=VMEM_LIMIT),
        name="natten",
    )(na_x, na_x, na_x, na_c, na_c, qw, kw, bias, seg)


def _na_bias_table(rpb, rows):
    assert min(WIN_ROWS, rows) == WIN_ROWS
    n_dr = 2 * WIN_ROWS - 1
    cols = np.arange(GRID_W)
    win_start = np.clip(cols - WIN_COLS // 2, 0, GRID_W - WIN_COLS)
    kc = cols[None, :]
    valid = (kc >= win_start[:, None]) & (kc < win_start[:, None] + WIN_COLS)
    rel = np.clip(kc - cols[:, None] + WIN_COLS - 1, 0, 2 * WIN_COLS - 2)
    pick_rel = (rel[None, :, :] == np.arange(2 * WIN_COLS - 1)[:, None, None]).astype(np.float32)
    tab = jnp.einsum("hdr,rck->hcdk", rpb * LOG2E, pick_rel, precision=HIGHEST)
    tab = jnp.where(valid[None, :, None, :], tab, -1e30).reshape(NA_HEADS // 2, 2 * GRID_W, n_dr * GRID_W)
    width = 8 * LANES
    assert (WIN_ROWS - 2) * GRID_W + WIN_ROWS * GRID_W <= width - GRID_W
    tab = jnp.pad(tab, ((0, 0), (0, 0), (0, width + GRID_W - n_dr * GRID_W)))
    return jnp.stack([tab[..., :width], tab[..., GRID_W:]], axis=0).astype(F32)


def _outffn_kernel(x_ref, dn_ref, na_ref, mod_ref, nw_ref, wo_ref, wi_ref, wf_ref, out_ref, *, d_ff, splits, n_sub):
    half = dn_ref.shape[-1]
    rows = x_ref.shape[1] // n_sub

    def prep(s):
        rs = slice(s * rows, (s + 1) * rows)
        attn = (jnp.dot(dn_ref[0, rs, :], wo_ref[0:half, :], preferred_element_type=F32)
                + jnp.dot(na_ref[0, rs, :], wo_ref[half:, :], preferred_element_type=F32))
        x1 = x_ref[0, rs, :] + mod_ref[0, 2:3, :] * attn
        ms = jnp.mean(x1 * x1, axis=-1, keepdims=True)
        h = (x1 * lax.rsqrt(ms + EPS)) * nw_ref[...]
        return x1, (h * (1.0 + mod_ref[0, 4:5, :]) + mod_ref[0, 3:4, :]).astype(BF16)

    cur = prep(0)
    for s in range(n_sub):
        nxt = prep(s + 1) if s + 1 < n_sub else None
        x1, hb = cur
        acc = None
        off = 0
        for n in splits:
            gate = jnp.dot(hb, wi_ref[:, off:off + n], preferred_element_type=F32)
            up = jnp.dot(hb, wi_ref[:, d_ff + off:d_ff + off + n], preferred_element_type=F32)
            part = jnp.dot((_silu(gate) * up).astype(BF16), wf_ref[off:off + n, :], preferred_element_type=F32)
            acc = part if acc is None else acc + part
            off += n
        out_ref[0, s * rows:(s + 1) * rows, :] = x1 + mod_ref[0, 5:6, :] * acc
        cur = nxt


def _out_ffn(x, dn, na, mods, norm_w, w_out, w_ffn_in, w_ffn_out, *, tm):
    b, l, d = x.shape
    half = dn.shape[-1]
    d_ff = w_ffn_out.shape[0]
    assert sum(FFN_SPLITS) == d_ff
    tm = min(tm, l)
    row_map = lambda bi, i: (bi, i, 0)
    const = lambda bi, i: (0, 0)
    single = pl.Buffered(1)
    return pl.pallas_call(
        functools.partial(_outffn_kernel, d_ff=d_ff, splits=FFN_SPLITS, n_sub=max(1, tm // FFN_SUB_ROWS)),
        grid=(b, l // tm),
        in_specs=[
            pl.BlockSpec((1, tm, d), row_map),
            pl.BlockSpec((1, tm, half), row_map),
            pl.BlockSpec((1, tm, half), row_map),
            pl.BlockSpec((1, 6, d), lambda bi, i: (bi, 0, 0)),
            pl.BlockSpec((1, d), const),
            pl.BlockSpec(w_out.shape, const, pipeline_mode=single),
            pl.BlockSpec(w_ffn_in.shape, const, pipeline_mode=single),
            pl.BlockSpec(w_ffn_out.shape, const, pipeline_mode=single),
        ],
        out_specs=pl.BlockSpec((1, tm, d), row_map),
        out_shape=jax.ShapeDtypeStruct((b, l, d), F32),
        compiler_params=pltpu.CompilerParams(
            dimension_semantics=("arbitrary", "arbitrary"), vmem_limit_bytes=VMEM_LIMIT),
        name="out_ffn",
    )(x, dn, na, mods, norm_w, w_out, w_ffn_in, w_ffn_out)


def kernel(x, c, ctx, c_ctx, norm1_w, norm2_w, w_ada, b_ada, w_in, dn_conv_w, dn_A_log, dn_dt_bias,
           dn_out_norm_w, na_q_norm_w, na_k_norm_w, na_rpb, w_out, w_ffn_in, w_ffn_out):
    assert w_in.shape[0] == 1, "single-layer problem"
    b, seq, d = x.shape
    hb = DN_HB
    ng = DN_HEADS // hb
    dn_w = DN_HEADS * DN_HEAD_DIM
    na_w = NA_HEADS * NA_HEAD_DIM
    rows = seq // GRID_W

    n_mod_rows = -(-(b + 1) // 8) * 8
    c_all = jnp.zeros((n_mod_rows, d), F32).at[:b].set(c).at[b].set(c_ctx)
    mod = _adaln(c_all, w_ada[0], b_ada[0][None, :])
    mod_x = mod[:b].reshape(b, 6, d)
    mod_c = mod[b:b + 1].reshape(1, 6, d)

    w0 = w_in[0]
    gate0 = 4 * dn_w
    na0 = gate0 + 4 * DN_HEADS
    gate_cols = w0[:, gate0:na0].reshape(d, 4, ng, hb)
    gate_cols = jnp.transpose(gate_cols, (0, 2, 1, 3)).reshape(d, ng, 4 * hb)
    gate_cols = jnp.pad(gate_cols, ((0, 0), (0, 0), (0, LANES - 4 * hb))).reshape(d, ng * LANES)
    w_lat = jnp.concatenate([w0[:, :gate0], w0[:, na0:], gate_cols], axis=1).astype(BF16)
    w_ctx = jnp.concatenate([w0[:, :3 * dn_w], w0[:, na0 + na_w:], gate_cols], axis=1).astype(BF16)
    n_gate = ng * LANES
    nw1 = norm1_w[0][None, :]
    qkv_x, z_x, na_x, ab_x = _in_proj(
        x, mod_x, nw1, w_lat, per_batch_mod=True, tm=IN_ROW_TILE,
        groups=((3 * dn_w, BF16), (dn_w, BF16), (3 * na_w, BF16), (n_gate, F32)))
    qkv_c, na_c, ab_c = _in_proj(
        ctx, mod_c, nw1, w_ctx, per_batch_mod=False, tm=IN_ROW_TILE,
        groups=((3 * dn_w, BF16), (2 * na_w, BF16), (n_gate, F32)))

    def group_params(p):
        return jnp.transpose(p[0].reshape(2, ng, hb), (1, 0, 2)).reshape(ng, 2 * hb)

    pg = jnp.stack([group_params(dn_A_log), group_params(dn_dt_bias)], axis=1)
    pcol = jnp.pad(pg, ((0, 0), (0, 0), (0, LANES - 2 * hb)))
    gcol_x = ab_x.reshape(b, seq // CHUNK, CHUNK, ng * LANES)
    gcol_c = ab_c.reshape(b, ctx.shape[1] // CHUNK, CHUNK, ng * LANES)
    dn_x = _deltanet(qkv_x, qkv_c, z_x, gcol_x, gcol_c, dn_conv_w[0], pcol,
                     dn_out_norm_w[0][None, :], hb=hb)

    bias = _na_bias_table(na_rpb[0], rows)
    qw = jnp.tile(na_q_norm_w[0], NA_HEADS)[None, :]
    kw = jnp.tile(na_k_norm_w[0], NA_HEADS)[None, :]
    na_o = _natten(na_x, na_c, qw, kw, bias)

    return _out_ffn(x, dn_x, na_o, mod_x, norm2_w[0][None, :], w_out[0].astype(BF16),
                    w_ffn_in[0].astype(BF16), w_ffn_out[0].astype(BF16), tm=ROW_TILE)
```

```python
import functools

import numpy as np
import jax
import jax.numpy as jnp
from jax import lax
from jax.experimental import pallas as pl
from jax.experimental.pallas import tpu as pltpu

F32 = jnp.float32
BF16 = jnp.bfloat16
HIGHEST = lax.Precision.HIGHEST

EPS = 1e-6
LOG2E = 1.4426950408889634
CHUNK = 64
CONV_K = 5
DN_HEAD_DIM = 128
DN_HEADS = 4
NA_HEAD_DIM = 64
NA_HEADS = 8
GRID_W = 64
WIN_ROWS = 8
WIN_COLS = 16
LANES = 128
VMEM_LIMIT = 56 * 1024 * 1024

DN_HB = 4
DN_CHUNKS_PER_STEP = 4
CONV_TILE = 128
CONV_HALO = 16
GATE_UNROLL = 8
NA_ROW_UNROLL = 8
IN_ROW_TILE = 1024
IN_SUB_ROWS = 512
ROW_TILE = 512
FFN_SUB_ROWS = 256
FFN_SPLITS = (1024, 1024, 768)


def _silu(x):
    return x * (1.0 / (1.0 + jnp.exp(-x)))


def _softplus(x):
    return jnp.maximum(x, 0.0) + jnp.log(1.0 + jnp.exp(-jnp.abs(x)))


def _mm(a, b):
    return jnp.dot(a.astype(BF16), b.astype(BF16), preferred_element_type=F32)


def _mm_nt(a, b):
    return lax.dot_general(a.astype(BF16), b.astype(BF16), (((1,), (1,)), ((), ())),
                           preferred_element_type=F32)


def _mm_tn(a, b):
    return lax.dot_general(a.astype(BF16), b.astype(BF16), (((0,), (0,)), ((), ())),
                           preferred_element_type=F32)


def _mm_f32(a, b):
    return lax.dot_general(a, b, (((1,), (0,)), ((), ())), precision=HIGHEST,
                           preferred_element_type=F32)


def _adaln_kernel(c_ref, w_ref, b_ref, o_ref):
    o_ref[...] = _mm_f32(_silu(c_ref[...]), w_ref[...]) + b_ref[...]


def _adaln(c_all, w_ada, b_ada):
    rows, d = c_all.shape
    n = w_ada.shape[1]
    tn = 1024
    return pl.pallas_call(
        _adaln_kernel,
        grid=(n // tn,),
        in_specs=[
            pl.BlockSpec((rows, d), lambda j: (0, 0)),
            pl.BlockSpec((d, tn), lambda j: (0, j)),
            pl.BlockSpec((1, tn), lambda j: (0, j)),
        ],
        out_specs=pl.BlockSpec((rows, tn), lambda j: (0, j)),
        out_shape=jax.ShapeDtypeStruct((rows, n), F32),
        compiler_params=pltpu.CompilerParams(
            dimension_semantics=("arbitrary",), vmem_limit_bytes=VMEM_LIMIT),
        name="adaln",
    )(c_all, w_ada, b_ada)


def _inproj_kernel(x_ref, mod_ref, nw_ref, w_ref, *out_refs, widths, n_sub):
    rows = x_ref.shape[1] // n_sub

    def prep(s):
        x = x_ref[0, s * rows:(s + 1) * rows, :]
        ms = jnp.mean(x * x, axis=-1, keepdims=True)
        h = (x * lax.rsqrt(ms + EPS)) * nw_ref[...]
        return (h * (1.0 + mod_ref[0, 1:2, :]) + mod_ref[0, 0:1, :]).astype(BF16)

    hb = prep(0)
    for s in range(n_sub):
        nxt = prep(s + 1) if s + 1 < n_sub else None
        off = 0
        for o_ref, n in zip(out_refs, widths):
            o_ref[0, s * rows:(s + 1) * rows, :] = jnp.dot(
                hb, w_ref[:, off:off + n], preferred_element_type=F32).astype(o_ref.dtype)
            off += n
        hb = nxt


def _in_proj(x, mods, norm_w, w_cols, *, groups, per_batch_mod, tm):
    b, l, d = x.shape
    widths = tuple(n for n, _ in groups)
    assert sum(widths) == w_cols.shape[1]
    tm = min(tm, l)
    mod_map = (lambda bi, i: (bi, 0, 0)) if per_batch_mod else (lambda bi, i: (0, 0, 0))
    row_map = lambda bi, i: (bi, i, 0)
    return pl.pallas_call(
        functools.partial(_inproj_kernel, widths=widths, n_sub=max(1, tm // IN_SUB_ROWS)),
        grid=(b, l // tm),
        in_specs=[
            pl.BlockSpec((1, tm, d), row_map),
            pl.BlockSpec((1, 6, d), mod_map),
            pl.BlockSpec((1, d), lambda bi, i: (0, 0)),
            pl.BlockSpec(w_cols.shape, lambda bi, i: (0, 0)),
        ],
        out_specs=[pl.BlockSpec((1, tm, n), row_map) for n in widths],
        out_shape=[jax.ShapeDtypeStruct((b, l, n), dt) for n, dt in groups],
        compiler_params=pltpu.CompilerParams(
            dimension_semantics=("arbitrary", "arbitrary"), vmem_limit_bytes=VMEM_LIMIT),
        name="in_proj",
    )(x, mods, norm_w, w_cols)


def _dn_kernel(qx_ref, kx_ref, vx_ref, qc_ref, kc_ref, vc_ref, z_ref,
               gcx_ref, gcc_ref,
               cwq_ref, cwk_ref, cwv_ref, pc_ref, onw_ref, sh_ref,
               out_ref,
               q_s, k_s, v_s, gc_s, gr_s, s_s, o_s, *, hb, seq, ctx_len):
    n_ctx = ctx_len // CHUNK
    n_lat = seq // CHUNK
    tile = CONV_TILE
    win = tile + 2 * CONV_HALO
    side_taps = [j for j in range(CONV_K) if j != CONV_K // 2]

    step = pl.program_id(0)
    slot_p = step % 2
    slot_c = 1 - slot_p

    streams = ((cwq_ref, q_s, "q"), (cwk_ref, k_s, "k"), (cwv_ref, v_s, "v"))
    lat_srcs = (qx_ref, kx_ref, vx_ref)
    ctx_srcs = (qc_ref, kc_ref, vc_ref)

    def conv_stages(srcs, n_rows, dst_off, t):
        n_tiles = n_rows // tile
        if isinstance(t, int):
            r0 = t * tile
            a0 = min(max(r0 - CONV_HALO, 0), n_rows - win)
            variant = 0 if t == 0 else (2 if t == n_tiles - 1 else 1)
        else:
            r0 = pl.multiple_of(t * tile, tile)
            a0 = pl.multiple_of(jnp.clip(r0 - CONV_HALO, 0, n_rows - win), CONV_HALO)
            variant = jnp.where(t == 0, 0, jnp.where(t == n_tiles - 1, 2, 1))
        sh = sh_ref[variant]
        zs = [jnp.dot(sh, src[0, pl.ds(a0, win), :], preferred_element_type=F32) for src in srcs]
        yield
        ys = []
        for src, z, (cw_ref, _, _) in zip(srcs, zs, streams):
            cw = cw_ref[...]
            acc = cw[CONV_K // 2:CONV_K // 2 + 1, :] * src[0, pl.ds(r0, tile), :].astype(F32)
            for n, j in enumerate(side_taps):
                acc = acc + cw[j:j + 1, :] * z[n * tile:(n + 1) * tile, :]
            ys.append(_silu(acc))
            yield
        for y, (_, dst_ref, mode) in zip(ys, streams):
            if mode != "v":
                parts = []
                for hh in range(hb):
                    seg = y[:, hh * DN_HEAD_DIM:(hh + 1) * DN_HEAD_DIM]
                    inv = lax.rsqrt(jnp.sum(seg * seg, axis=-1, keepdims=True) + EPS)
                    if mode == "q":
                        inv = inv * (DN_HEAD_DIM ** -0.5)
                    parts.append(seg * inv)
                y = parts[0] if hb == 1 else jnp.concatenate(parts, axis=1)
            dst_ref[slot_p, pl.ds(dst_off + r0, tile), :] = y.astype(dst_ref.dtype)
            yield

    def conv_tile(srcs, n_rows, dst_off, t):
        for _ in conv_stages(srcs, n_rows, dst_off, t):
            pass

    @pl.when(step == 0)
    def _():
        for t in range(ctx_len // tile):
            conv_tile(ctx_srcs, ctx_len, 0, t)

        def body(t, carry):
            conv_tile(lat_srcs, seq, ctx_len, t)
            return carry

        lax.fori_loop(0, seq // tile, body, 0)

    n_hp = hb // 2
    ri = lax.broadcasted_iota(jnp.int32, (CHUNK, CHUNK), 0)
    ci = lax.broadcasted_iota(jnp.int32, (CHUNK, CHUNK), 1)
    lower = jnp.where(ci <= ri, 1.0, 0.0).astype(BF16)
    upper = jnp.where(ci >= ri, 1.0, 0.0).astype(BF16)
    tri_c = jnp.concatenate([lower, upper], axis=1)
    a_log_c, dt_c = pc_ref[0, 0:1, :], pc_ref[0, 1:2, :]
    lane = lax.broadcasted_iota(jnp.int32, (CHUNK, LANES), 1)
    pr = lax.broadcasted_iota(jnp.int32, (8, LANES), 0)
    pl_ = lax.broadcasted_iota(jnp.int32, (8, LANES), 1)
    pick = jnp.where((pr < 2 * n_hp) & (pl_ == (pr // n_hp) * hb + 2 * (pr % n_hp)), 1.0, 0.0).astype(BF16)

    def split3(x):
        x1 = x.astype(BF16)
        r1 = x - x1.astype(F32)
        x2 = r1.astype(BF16)
        x3 = (r1 - x2.astype(F32)).astype(BF16)
        return x1, x2, x3

    def gate_stages(gc_ref, ts, off):
        xc = [gc_ref[0, t] for t in ts]
        val = [jnp.where(lane < 2 * hb, -jnp.exp(a_log_c) * _softplus(x + dt_c), 1.0 / (1.0 + jnp.exp(-x)))
               for x in xc]
        yield
        gcs = [split3(jnp.concatenate([jnp.where(lane < hb, v, 0.0),
                                       jnp.where((lane >= hb) & (lane < 2 * hb), v, 0.0)], axis=0))
               for v in val]
        gam_c = [sum(jnp.dot(tri_c, p, preferred_element_type=F32) for p in ps) for ps in gcs]
        yield
        both = [split3(jnp.concatenate([g, pltpu.roll(g, LANES - 1, 1)], axis=0)) for g in gam_c]
        gam_r = [sum(lax.dot_general(pick, p, (((1,), (1,)), ((), ())), preferred_element_type=F32)
                     for p in ps) for ps in both]
        yield
        for u, t in enumerate(ts):
            gc_s[slot_p, off + t] = jnp.where(lane < 2 * hb, gam_c[u], val[u])
            gr_s[slot_p, off + t] = gam_r[u]
        yield

    def gates_into(gc_ref, n_chunks, off):
        unroll = min(GATE_UNROLL, n_chunks)

        def body(it, carry):
            for _ in gate_stages(gc_ref, [it * unroll + u for u in range(unroll)], off):
                pass
            return carry

        lax.fori_loop(0, n_chunks // unroll, body, 0)

    @pl.when(step == 0)
    def _():
        gates_into(gcc_ref, n_ctx, 0)
        gates_into(gcx_ref, n_lat, n_ctx)

    onw = onw_ref[...]
    prow = lax.broadcasted_iota(jnp.int32, (CHUNK, LANES), 0)
    pcol = lane % CHUNK
    lo = lane < CHUNK
    eye_p = jnp.where(pcol == prow, 1.0, 0.0)
    before_eq = (pcol <= prow, pcol >= prow)
    strict = (pcol < prow, pcol > prow)
    zeros_w = jnp.zeros((CHUNK, 2 * DN_HEAD_DIM), BF16)

    def block_diag(x):
        xb = x.astype(BF16)
        zero = jnp.zeros_like(xb)
        return jnp.concatenate([jnp.where(lo, xb, zero), jnp.where(lo, zero, xb)], axis=0)

    def chunk_steps(groups, out_mode, side=None):
        def tick():
            if side is not None:
                next(side, None)

        pairs = [p for grp in groups for p in grp]
        rng = range(len(pairs))
        r0s = [pl.multiple_of(chunk * CHUNK, CHUNK) for _, _, chunk in pairs]
        wide = [slice(hp * 2 * DN_HEAD_DIM, (hp + 1) * 2 * DN_HEAD_DIM) for _, hp, _ in pairs]
        q = [q_s[slot_c, pl.ds(r0s[i], CHUNK), wide[i]].astype(F32) for i in rng]
        k = [k_s[slot_c, pl.ds(r0s[i], CHUNK), wide[i]].astype(F32) for i in rng]
        v = [v_s[slot_c, pl.ds(r0s[i], CHUNK), wide[i]].astype(F32) for i in rng]
        gcol = [gc_s[slot_c, pairs[i][2]] for i in rng]
        grow = [gr_s[slot_c, pairs[i][2]] for i in rng]
        cs = [[d * hb + 2 * hp + e for e in range(2)] for d, hp, _ in pairs]
        dirs = [d for d, _, _ in pairs]
        gam_c = [[gcol[i][:, c:c + 1] for c in cs[i]] for i in rng]
        beta_c = [[gcol[i][:, 2 * hb + c:2 * hb + c + 1] for c in cs[i]] for i in rng]
        g_tot = [[g[CHUNK - 1:CHUNK, :] if dirs[i] == 0 else g[0:1, :] for g in gam_c[i]] for i in rng]
        gam_cp = [jnp.where(lo, gam_c[i][0], gam_c[i][1]) for i in rng]
        beta_cp = [jnp.where(lo, beta_c[i][0], beta_c[i][1]) for i in rng]
        gam_rp = [grow[i][dirs[i] * n_hp + pairs[i][1]:dirs[i] * n_hp + pairs[i][1] + 1, :] for i in rng]
        decay = [jnp.exp(jnp.where(before_eq[dirs[i]], gam_cp[i] - gam_rp[i], -jnp.inf)) for i in rng]
        kb = [k[i].astype(BF16) for i in rng]
        k_bd = [jnp.concatenate([jnp.concatenate([kb[i][:, :DN_HEAD_DIM], zeros_w[:, :DN_HEAD_DIM]], axis=1),
                                 jnp.concatenate([zeros_w[:, :DN_HEAD_DIM], kb[i][:, DN_HEAD_DIM:]], axis=1)],
                                axis=0) for i in rng]
        qk_kk = [_mm_nt(jnp.concatenate([q[i], k[i]], axis=0), k_bd[i]) for i in rng]
        tick()
        qkd = [qk_kk[i][:CHUNK] * decay[i] for i in rng]
        m = [jnp.where(strict[dirs[i]], qk_kk[i][CHUNK:] * beta_cp[i] * decay[i], 0.0) for i in rng]
        t_inv = [eye_p - m[i] for i in rng]
        pw = [_mm(m[i], block_diag(m[i])) for i in rng]
        tick()
        for _ in range(4):
            res = [_mm(jnp.concatenate([pw[i], t_inv[i]], axis=0), block_diag(pw[i])) for i in rng]
            tick()
            pw = [res[i][:CHUNK] for i in rng]
            t_inv = [t_inv[i] + res[i][CHUNK:] for i in rng]
        t_inv = [t_inv[i] + _mm(t_inv[i], block_diag(pw[i])) for i in rng]
        e_gam = [[jnp.exp(g) for g in gam_c[i]] for i in rng]
        hd = [slice(0, DN_HEAD_DIM), slice(DN_HEAD_DIM, 2 * DN_HEAD_DIM)]
        rhs = [[jnp.concatenate([v[i][:, hd[e]] * beta_c[i][e], k[i][:, hd[e]] * (beta_c[i][e] * e_gam[i][e])],
                                axis=1).astype(BF16) for e in range(2)] for i in rng]
        rhs_bd = [jnp.concatenate([jnp.concatenate([rhs[i][0], zeros_w], axis=1),
                                   jnp.concatenate([zeros_w, rhs[i][1]], axis=1)], axis=0) for i in rng]
        uw = [_mm(t_inv[i], rhs_bd[i]) for i in rng]
        tick()
        u_h = [[uw[i][:, (2 * e) * DN_HEAD_DIM:(2 * e + 1) * DN_HEAD_DIM] for e in range(2)] for i in rng]
        w_h = [[uw[i][:, (2 * e + 1) * DN_HEAD_DIM:(2 * e + 2) * DN_HEAD_DIM] for e in range(2)] for i in rng]
        kdec = [[(k[i][:, hd[e]] * jnp.exp(g_tot[i][e] - gam_c[i][e])).astype(BF16) for e in range(2)] for i in rng]
        s_scale = [[jnp.exp(g) for g in g_tot[i]] for i in rng]
        if out_mode is None:
            lhs = [[w_h[i][e].astype(BF16) for e in range(2)] for i in rng]
        else:
            lhs = [[jnp.concatenate([q[i][:, hd[e]] * e_gam[i][e], w_h[i][e]], axis=0).astype(BF16)
                    for e in range(2)] for i in rng]

        state = {}
        pos = 0
        for grp in groups:
            idx = range(pos, pos + len(grp))
            pos += len(grp)
            ch = [(i, e) for i in idx for e in range(2)]
            s_old = {(i, e): (state[cs[i][e]] if cs[i][e] in state else s_s[cs[i][e]]) for i, e in ch}
            prod = {(i, e): _mm(lhs[i][e], s_old[(i, e)]) for i, e in ch}
            tick()
            v_new = {(i, e): u_h[i][e] - prod[(i, e)][-CHUNK:] for i, e in ch}
            for i, e in ch:
                state[cs[i][e]] = s_scale[i][e] * s_old[(i, e)] + _mm_tn(kdec[i][e], v_new[(i, e)])
            tick()
            if out_mode is None:
                continue
            zeros_h = zeros_w[:, :DN_HEAD_DIM]
            v_bd = {i: jnp.concatenate(
                [jnp.concatenate([v_new[(i, 0)].astype(BF16), zeros_h], axis=1),
                 jnp.concatenate([zeros_h, v_new[(i, 1)].astype(BF16)], axis=1)], axis=0) for i in idx}
            intra = {i: _mm(qkd[i], v_bd[i]) for i in idx}
            for i, e in ch:
                o = prod[(i, e)][:CHUNK] + intra[i][:, hd[e]]
                rl = pl.multiple_of(r0s[i] - ctx_len, CHUNK)
                hh = 2 * pairs[i][1] + e
                hs = slice(hh * DN_HEAD_DIM, (hh + 1) * DN_HEAD_DIM)
                if out_mode == "store":
                    o_s[pl.ds(rl, CHUNK), hs] = o
                else:
                    tot = o + o_s[pl.ds(rl, CHUNK), hs]
                    y = tot * lax.rsqrt(jnp.mean(tot * tot, axis=-1, keepdims=True) + EPS) * onw
                    zz = z_ref[0, pl.ds(rl, CHUNK), hs].astype(F32)
                    out_ref[0, pl.ds(rl, CHUNK), hs] = (y * _silu(zz)).astype(out_ref.dtype)
        for c, val in state.items():
            s_s[c] = val
        if side is not None:
            for _ in side:
                pass

    head_pairs = range(n_hp)

    def groups_at(first, last, i):
        return [(0, hp, first + i) for hp in head_pairs] + [(1, hp, last - i) for hp in head_pairs]

    @pl.when(step > 0)
    def _():
        s_s[...] = jnp.zeros(s_s.shape, F32)

        def side_work(srcs, n_rows, dst_off, gc_ref, chunk_off, j, cps):
            for u in range(cps // 2):
                yield from conv_stages(srcs, n_rows, dst_off, (cps // 2) * j + u)
            yield from gate_stages(gc_ref, [cps * j + u for u in range(cps)], chunk_off)

        cps_c = min(DN_CHUNKS_PER_STEP, n_ctx)
        for j in range(n_ctx // cps_c):
            chunk_steps([groups_at(0, n_ctx - 1, cps_c * j + u) for u in range(cps_c)], None,
                        side=side_work(ctx_srcs, ctx_len, 0, gcc_ref, 0, j, cps_c))

        cps = DN_CHUNKS_PER_STEP

        def lat_body(mode, j, carry):
            first, last = n_ctx, n_ctx + n_lat - 1
            chunk_steps([groups_at(first, last, cps * j + u) for u in range(cps)], mode,
                        side=side_work(lat_srcs, seq, ctx_len, gcx_ref, n_ctx, j, cps))
            return carry

        half = n_lat // (2 * cps)
        lax.fori_loop(0, half, functools.partial(lat_body, "store"), 0)
        lax.fori_loop(half, 2 * half, functools.partial(lat_body, "final"), 0)


def _conv_shift_matrices():
    side = [j - CONV_K // 2 for j in range(CONV_K) if j != CONV_K // 2]
    win = CONV_TILE + 2 * CONV_HALO
    sh = np.zeros((3, len(side) * CONV_TILE, win), np.float32)
    for d in range(3):
        for n, off in enumerate(side):
            for i in range(CONV_TILE):
                m = i + off + d * CONV_HALO
                if 0 <= m < win:
                    sh[d, n * CONV_TILE + i, m] = 1.0
    return jnp.asarray(sh, BF16)


def _deltanet(qkv_x, qkv_c, z_x, gcol_x, gcol_c, conv_w, pcol, onw, *, hb):
    b, seq, _ = qkv_x.shape
    ctx_len = qkv_c.shape[1]
    ng = DN_HEADS // hb
    w = hb * DN_HEAD_DIM
    n_lat, n_ctx = seq // CHUNK, ctx_len // CHUNK
    assert n_lat % (2 * DN_CHUNKS_PER_STEP) == 0 and n_ctx % 2 == 0 and DN_CHUNKS_PER_STEP % 2 == 0
    assert DN_CHUNKS_PER_STEP % n_ctx == 0 or n_ctx % DN_CHUNKS_PER_STEP == 0
    assert seq >= 2 * CONV_TILE and ctx_len >= 2 * CONV_TILE
    assert n_lat % GATE_UNROLL == 0 and GATE_UNROLL % n_ctx == 0 and hb % 2 == 0 and hb <= 8
    sh = _conv_shift_matrices()

    assert CONV_TILE == 2 * CHUNK
    assert ng == 1
    last = b - 1

    def col(part):
        return lambda s, g: (jnp.minimum(s, last), 0, part * ng + g)

    def cur(*rest):
        return lambda s, g: (jnp.maximum(s - 1, 0),) + tuple(g if r == "g" else r for r in rest)

    def cwcol(part):
        return lambda bi, g: (0, part * ng + g)

    return pl.pallas_call(
        functools.partial(_dn_kernel, hb=hb, seq=seq, ctx_len=ctx_len),
        grid=(b + 1, ng),
        in_specs=[
            pl.BlockSpec((1, seq, w), col(0)),
            pl.BlockSpec((1, seq, w), col(1)),
            pl.BlockSpec((1, seq, w), col(2)),
            pl.BlockSpec((1, ctx_len, w), col(0)),
            pl.BlockSpec((1, ctx_len, w), col(1)),
            pl.BlockSpec((1, ctx_len, w), col(2)),
            pl.BlockSpec((1, seq, w), cur(0, "g")),
            pl.BlockSpec((1, n_lat, CHUNK, LANES), lambda s, g: (jnp.minimum(s, last), 0, 0, g)),
            pl.BlockSpec((1, n_ctx, CHUNK, LANES), lambda s, g: (jnp.minimum(s, last), 0, 0, g)),
            pl.BlockSpec((CONV_K, w), cwcol(0)),
            pl.BlockSpec((CONV_K, w), cwcol(1)),
            pl.BlockSpec((CONV_K, w), cwcol(2)),
            pl.BlockSpec((1, 2, LANES), lambda bi, g: (g, 0, 0)),
            pl.BlockSpec((1, DN_HEAD_DIM), lambda bi, g: (0, 0)),
            pl.BlockSpec(sh.shape, lambda bi, g: (0, 0, 0)),
        ],
        out_specs=pl.BlockSpec((1, seq, w), cur(0, "g")),
        out_shape=jax.ShapeDtypeStruct((b, seq, DN_HEADS * DN_HEAD_DIM), BF16),
        scratch_shapes=[
            pltpu.VMEM((2, ctx_len + seq, w), BF16),
            pltpu.VMEM((2, ctx_len + seq, w), BF16),
            pltpu.VMEM((2, ctx_len + seq, w), BF16),
            pltpu.VMEM((2, n_ctx + n_lat, CHUNK, LANES), F32),
            pltpu.VMEM((2, n_ctx + n_lat, 8, LANES), F32),
            pltpu.VMEM((2 * hb, DN_HEAD_DIM, DN_HEAD_DIM), F32),
            pltpu.VMEM((seq, w), F32),
        ],
        compiler_params=pltpu.CompilerParams(
            dimension_semantics=("arbitrary", "arbitrary"), vmem_limit_bytes=VMEM_LIMIT),
        name="deltanet",
    )(qkv_x, qkv_x, qkv_x, qkv_c, qkv_c, qkv_c, z_x, gcol_x, gcol_c,
      conv_w, conv_w, conv_w, pcol, onw, sh)


def _na_kernel(q_ref, k_ref, v_ref, kc_ref, vc_ref, qw_ref, kw_ref, bias_ref, seg_ref, out_ref,
               qn_s, kn_s, knc_s, *, seq, ctx_len):
    step = pl.program_id(0)
    slot_p = step % 2
    slot_c = 1 - slot_p
    rows = seq // GRID_W
    win_rows = min(WIN_ROWS, rows)
    n_win = win_rows * GRID_W
    n_pairs = NA_HEADS // 2

    def head_rms(x, wgt):
        ss = jnp.dot((x * x).astype(BF16), seg_ref[...], preferred_element_type=F32)
        return x * lax.rsqrt(ss * (1.0 / NA_HEAD_DIM) + EPS) * wgt

    kw = kw_ref[...]
    qw = qw_ref[...] * ((NA_HEAD_DIM ** -0.5) * LOG2E)
    tile = 256

    def norm_stages(t):
        r0 = pl.multiple_of(t * tile, tile)
        kn = head_rms(k_ref[0, pl.ds(r0, tile), :].astype(F32), kw)
        yield
        kn_s[slot_p, pl.ds(r0, tile), :] = kn.astype(BF16)
        qn = head_rms(q_ref[0, pl.ds(r0, tile), :].astype(F32), qw)
        yield
        qn_s[slot_p, pl.ds(r0, tile), :] = qn.astype(BF16)
        yield

    n_norm = seq // tile
    knc_s[slot_p] = head_rms(kc_ref[0].astype(F32), kw).astype(BF16)

    @pl.when(step == 0)
    def _():
        def norm_body(t, carry):
            for _ in norm_stages(t):
                pass
            return carry

        lax.fori_loop(0, n_norm, norm_body, 0)

    lane2 = lax.broadcasted_iota(jnp.int32, (GRID_W, LANES), 1)
    lo2 = lane2 < NA_HEAD_DIM
    ones_w = jnp.ones((n_win, LANES), BF16)
    ones_c = jnp.ones((ctx_len, LANES), BF16)
    zero = jnp.zeros((GRID_W, LANES), BF16)

    n_row_steps = rows // NA_ROW_UNROLL
    norm_per_step = n_norm // n_row_steps

    def row_body(it, carry):
        def side_gen():
            for u in range(norm_per_step):
                yield from norm_stages(it * norm_per_step + u)

        side = side_gen()

        def tick():
            next(side, None)

        streams = [(u, j) for u in range(NA_ROW_UNROLL) for j in range(n_pairs)]
        ns = range(len(streams))
        row = [it * NA_ROW_UNROLL + u for u, _ in streams]
        r_start = [jnp.clip(r - win_rows // 2, 0, rows - win_rows) for r in row]
        case = [r_start[n] - row[n] + (WIN_ROWS - 1) for n in ns]
        b_copy = [case[n] % 2 for n in ns]
        b_lane = [pl.multiple_of((case[n] // 2) * LANES, LANES) for n in ns]
        q0 = [pl.multiple_of(r * GRID_W, GRID_W) for r in row]
        k0 = [pl.multiple_of(r_start[n] * GRID_W, GRID_W) for n in ns]
        lss = [slice(j * LANES, (j + 1) * LANES) for _, j in streams]
        q2 = [qn_s[slot_c, pl.ds(q0[n], GRID_W), lss[n]] for n in ns]
        qs = [jnp.concatenate([jnp.where(lo2, q2[n], zero), jnp.where(lo2, zero, q2[n])], axis=0)
              for n in ns]
        tick()
        s_w = [_mm_nt(qs[n], kn_s[slot_c, pl.ds(k0[n], n_win), lss[n]])
               + bias_ref[b_copy[n], streams[n][1], :, pl.ds(b_lane[n], n_win)] for n in ns]
        tick()
        s_c = [_mm_nt(qs[n], knc_s[slot_c, :, lss[n]]) for n in ns]
        tick()
        mx = [s_c[n][:, 0:LANES] for n in ns]
        for n in ns:
            for t in range(1, ctx_len // LANES):
                mx[n] = jnp.maximum(mx[n], s_c[n][:, t * LANES:(t + 1) * LANES])
            for t in range(n_win // LANES):
                mx[n] = jnp.maximum(mx[n], s_w[n][:, t * LANES:(t + 1) * LANES])
        tick()
        mx = [jnp.max(mx[n], axis=-1, keepdims=True) for n in ns]
        p_w = [jnp.exp2(s_w[n] - mx[n]).astype(BF16) for n in ns]
        tick()
        p_c = [jnp.exp2(s_c[n] - mx[n]).astype(BF16) for n in ns]
        tick()
        o = [_mm(p_w[n], jnp.concatenate([v_ref[0, pl.ds(k0[n], n_win), lss[n]], ones_w], axis=1))
             + _mm(p_c[n], jnp.concatenate([vc_ref[0, :, lss[n]], ones_c], axis=1)) for n in ns]
        for n in ns:
            on = o[n][:, :LANES] * (1.0 / o[n][:, LANES:])
            out_ref[0, pl.ds(q0[n], GRID_W), lss[n]] = jnp.where(lo2, on[:GRID_W], on[GRID_W:]).astype(out_ref.dtype)
        for _ in side:
            pass
        return carry

    @pl.when(step > 0)
    def _():
        lax.fori_loop(0, n_row_steps, row_body, 0)


def _natten(na_x, na_c, qw, kw, bias):
    b, seq, _ = na_x.shape
    ctx_len = na_c.shape[1]
    wq = NA_HEADS * NA_HEAD_DIM
    assert seq % 256 == 0 and ctx_len % LANES == 0 and (seq // GRID_W) % NA_ROW_UNROLL == 0
    assert (seq // 256) % (seq // GRID_W // NA_ROW_UNROLL) == 0
    last = b - 1
    nxt = lambda col: (lambda s: (jnp.minimum(s, last), 0, col))
    cur = lambda col: (lambda s: (jnp.maximum(s - 1, 0), 0, col))
    head_of = np.arange(wq) // NA_HEAD_DIM
    seg = jnp.asarray(head_of[:, None] == head_of[None, :], BF16)
    return pl.pallas_call(
        functools.partial(_na_kernel, seq=seq, ctx_len=ctx_len),
        grid=(b + 1,),
        in_specs=[
            pl.BlockSpec((1, seq, wq), nxt(0)),
            pl.BlockSpec((1, seq, wq), nxt(1)),
            pl.BlockSpec((1, seq, wq), cur(2)),
            pl.BlockSpec((1, ctx_len, wq), nxt(0)),
            pl.BlockSpec((1, ctx_len, wq), cur(1)),
            pl.BlockSpec((1, wq), lambda bi: (0, 0)),
            pl.BlockSpec((1, wq), lambda bi: (0, 0)),
            pl.BlockSpec(bias.shape, lambda bi: (0, 0, 0, 0)),
            pl.BlockSpec((wq, wq), lambda bi: (0, 0)),
        ],
        out_specs=pl.BlockSpec((1, seq, wq), cur(0)),
        out_shape=jax.ShapeDtypeStruct((b, seq, wq), BF16),
        scratch_shapes=[
            pltpu.VMEM((2, seq, wq), BF16),
            pltpu.VMEM((2, seq, wq), BF16),
            pltpu.VMEM((2, ctx_len, wq), BF16),
        ],
        compiler_params=pltpu.CompilerParams(
            dimension_semantics=("arbitrary",), vmem_limit_bytes=VMEM_LIMIT),
        name="natten",
    )(na_x, na_x, na_x, na_c, na_c, qw, kw, bias, seg)


def _na_bias_table(rpb, rows):
    assert min(WIN_ROWS, rows) == WIN_ROWS
    n_dr = 2 * WIN_ROWS - 1
    cols = np.arange(GRID_W)
    win_start = np.clip(cols - WIN_COLS // 2, 0, GRID_W - WIN_COLS)
    kc = cols[None, :]
    valid = (kc >= win_start[:, None]) & (kc < win_start[:, None] + WIN_COLS)
    rel = np.clip(kc - cols[:, None] + WIN_COLS - 1, 0, 2 * WIN_COLS - 2)
    pick_rel = (rel[None, :, :] == np.arange(2 * WIN_COLS - 1)[:, None, None]).astype(np.float32)
    tab = jnp.einsum("hdr,rck->hcdk", rpb * LOG2E, pick_rel, precision=HIGHEST)
    tab = jnp.where(valid[None, :, None, :], tab, -1e30).reshape(NA_HEADS // 2, 2 * GRID_W, n_dr * GRID_W)
    width = 8 * LANES
    assert (WIN_ROWS - 2) * GRID_W + WIN_ROWS * GRID_W <= width - GRID_W
    tab = jnp.pad(tab, ((0, 0), (0, 0), (0, width + GRID_W - n_dr * GRID_W)))
    return jnp.stack([tab[..., :width], tab[..., GRID_W:]], axis=0).astype(F32)


def _outffn_kernel(x_ref, dn_ref, na_ref, mod_ref, nw_ref, wo_ref, wi_ref, wf_ref, out_ref, *, d_ff, splits, n_sub):
    half = dn_ref.shape[-1]
    rows = x_ref.shape[1] // n_sub

    def prep(s):
        rs = slice(s * rows, (s + 1) * rows)
        attn = (jnp.dot(dn_ref[0, rs, :], wo_ref[0:half, :], preferred_element_type=F32)
                + jnp.dot(na_ref[0, rs, :], wo_ref[half:, :], preferred_element_type=F32))
        x1 = x_ref[0, rs, :] + mod_ref[0, 2:3, :] * attn
        ms = jnp.mean(x1 * x1, axis=-1, keepdims=True)
        h = (x1 * lax.rsqrt(ms + EPS)) * nw_ref[...]
        return x1, (h * (1.0 + mod_ref[0, 4:5, :]) + mod_ref[0, 3:4, :]).astype(BF16)

    cur = prep(0)
    for s in range(n_sub):
        nxt = prep(s + 1) if s + 1 < n_sub else None
        x1, hb = cur
        acc = None
        off = 0
        for n in splits:
            gate = jnp.dot(hb, wi_ref[:, off:off + n], preferred_element_type=F32)
            up = jnp.dot(hb, wi_ref[:, d_ff + off:d_ff + off + n], preferred_element_type=F32)
            part = jnp.dot((_silu(gate) * up).astype(BF16), wf_ref[off:off + n, :], preferred_element_type=F32)
            acc = part if acc is None else acc + part
            off += n
        out_ref[0, s * rows:(s + 1) * rows, :] = x1 + mod_ref[0, 5:6, :] * acc
        cur = nxt


def _out_ffn(x, dn, na, mods, norm_w, w_out, w_ffn_in, w_ffn_out, *, tm):
    b, l, d = x.shape
    half = dn.shape[-1]
    d_ff = w_ffn_out.shape[0]
    assert sum(FFN_SPLITS) == d_ff
    tm = min(tm, l)
    row_map = lambda bi, i: (bi, i, 0)
    const = lambda bi, i: (0, 0)
    single = pl.Buffered(1)
    return pl.pallas_call(
        functools.partial(_outffn_kernel, d_ff=d_ff, splits=FFN_SPLITS, n_sub=max(1, tm // FFN_SUB_ROWS)),
        grid=(b, l // tm),
        in_specs=[
            pl.BlockSpec((1, tm, d), row_map),
            pl.BlockSpec((1, tm, half), row_map),
            pl.BlockSpec((1, tm, half), row_map),
            pl.BlockSpec((1, 6, d), lambda bi, i: (bi, 0, 0)),
            pl.BlockSpec((1, d), const),
            pl.BlockSpec(w_out.shape, const, pipeline_mode=single),
            pl.BlockSpec(w_ffn_in.shape, const, pipeline_mode=single),
            pl.BlockSpec(w_ffn_out.shape, const, pipeline_mode=single),
        ],
        out_specs=pl.BlockSpec((1, tm, d), row_map),
        out_shape=jax.ShapeDtypeStruct((b, l, d), F32),
        compiler_params=pltpu.CompilerParams(
            dimension_semantics=("arbitrary", "arbitrary"), vmem_limit_bytes=VMEM_LIMIT),
        name="out_ffn",
    )(x, dn, na, mods, norm_w, w_out, w_ffn_in, w_ffn_out)


def kernel(x, c, ctx, c_ctx, norm1_w, norm2_w, w_ada, b_ada, w_in, dn_conv_w, dn_A_log, dn_dt_bias,
           dn_out_norm_w, na_q_norm_w, na_k_norm_w, na_rpb, w_out, w_ffn_in, w_ffn_out):
    assert w_in.shape[0] == 1, "single-layer problem"
    b, seq, d = x.shape
    hb = DN_HB
    ng = DN_HEADS // hb
    dn_w = DN_HEADS * DN_HEAD_DIM
    na_w = NA_HEADS * NA_HEAD_DIM
    rows = seq // GRID_W

    n_mod_rows = -(-(b + 1) // 8) * 8
    c_all = jnp.zeros((n_mod_rows, d), F32).at[:b].set(c).at[b].set(c_ctx)
    mod = _adaln(c_all, w_ada[0], b_ada[0][None, :])
    mod_x = mod[:b].reshape(b, 6, d)
    mod_c = mod[b:b + 1].reshape(1, 6, d)

    w0 = w_in[0]
    gate0 = 4 * dn_w
    na0 = gate0 + 4 * DN_HEADS
    gate_cols = w0[:, gate0:na0].reshape(d, 4, ng, hb)
    gate_cols = jnp.transpose(gate_cols, (0, 2, 1, 3)).reshape(d, ng, 4 * hb)
    gate_cols = jnp.pad(gate_cols, ((0, 0), (0, 0), (0, LANES - 4 * hb))).reshape(d, ng * LANES)
    w_lat = jnp.concatenate([w0[:, :gate0], w0[:, na0:], gate_cols], axis=1).astype(BF16)
    w_ctx = jnp.concatenate([w0[:, :3 * dn_w], w0[:, na0 + na_w:], gate_cols], axis=1).astype(BF16)
    n_gate = ng * LANES
    nw1 = norm1_w[0][None, :]
    qkv_x, z_x, na_x, ab_x = _in_proj(
        x, mod_x, nw1, w_lat, per_batch_mod=True, tm=IN_ROW_TILE,
        groups=((3 * dn_w, BF16), (dn_w, BF16), (3 * na_w, BF16), (n_gate, F32)))
    qkv_c, na_c, ab_c = _in_proj(
        ctx, mod_c, nw1, w_ctx, per_batch_mod=False, tm=IN_ROW_TILE,
        groups=((3 * dn_w, BF16), (2 * na_w, BF16), (n_gate, F32)))

    def group_params(p):
        return jnp.transpose(p[0].reshape(2, ng, hb), (1, 0, 2)).reshape(ng, 2 * hb)

    pg = jnp.stack([group_params(dn_A_log), group_params(dn_dt_bias)], axis=1)
    pcol = jnp.pad(pg, ((0, 0), (0, 0), (0, LANES - 2 * hb)))
    gcol_x = ab_x.reshape(b, seq // CHUNK, CHUNK, ng * LANES)
    gcol_c = ab_c.reshape(b, ctx.shape[1] // CHUNK, CHUNK, ng * LANES)
    dn_x = _deltanet(qkv_x, qkv_c, z_x, gcol_x, gcol_c, dn_conv_w[0], pcol,
                     dn_out_norm_w[0][None, :], hb=hb)

    bias = _na_bias_table(na_rpb[0], rows)
    qw = jnp.tile(na_q_norm_w[0], NA_HEADS)[None, :]
    kw = jnp.tile(na_k_norm_w[0], NA_HEADS)[None, :]
    na_o = _natten(na_x, na_c, qw, kw, bias)

    return _out_ffn(x, dn_x, na_o, mod_x, norm2_w[0][None, :], w_out[0].astype(BF16),
                    w_ffn_in[0].astype(BF16), w_ffn_out[0].astype(BF16), tm=ROW_TILE)
```

```python
import functools

import numpy as np
import jax
import jax.numpy as jnp
from jax import lax
from jax.experimental import pallas as pl
from jax.experimental.pallas import tpu as pltpu

F32 = jnp.float32
BF16 = jnp.bfloat16
HIGHEST = lax.Precision.HIGHEST

EPS = 1e-6
LOG2E = 1.4426950408889634
CHUNK = 64
CONV_K = 5
DN_HEAD_DIM = 128
DN_HEADS = 4
NA_HEAD_DIM = 64
NA_HEADS = 8
GRID_W = 64
WIN_ROWS = 8
WIN_COLS = 16
LANES = 128
VMEM_LIMIT = 56 * 1024 * 1024

DN_HB = 4
DN_CHUNKS_PER_STEP = 4
CONV_TILE = 128
CONV_HALO = 16
GATE_UNROLL = 8
NA_ROW_UNROLL = 8
IN_ROW_TILE = 1024
IN_SUB_ROWS = 512
ROW_TILE = 1024
FFN_SUB_ROWS = 256
FFN_SPLITS = (1024, 1024, 768)


def _silu(x):
    return x * (1.0 / (1.0 + jnp.exp(-x)))


def _softplus(x):
    return jnp.maximum(x, 0.0) + jnp.log(1.0 + jnp.exp(-jnp.abs(x)))


def _mm(a, b):
    return jnp.dot(a.astype(BF16), b.astype(BF16), preferred_element_type=F32)


def _mm_nt(a, b):
    return lax.dot_general(a.astype(BF16), b.astype(BF16), (((1,), (1,)), ((), ())),
                           preferred_element_type=F32)


def _mm_tn(a, b):
    return lax.dot_general(a.astype(BF16), b.astype(BF16), (((0,), (0,)), ((), ())),
                           preferred_element_type=F32)


def _mm_f32(a, b):
    return lax.dot_general(a, b, (((1,), (0,)), ((), ())), precision=HIGHEST,
                           preferred_element_type=F32)


def _adaln_kernel(c_ref, w_ref, b_ref, o_ref):
    o_ref[...] = _mm_f32(_silu(c_ref[...]), w_ref[...]) + b_ref[...]


def _adaln(c_all, w_ada, b_ada):
    rows, d = c_all.shape
    n = w_ada.shape[1]
    tn = 1024
    return pl.pallas_call(
        _adaln_kernel,
        grid=(n // tn,),
        in_specs=[
            pl.BlockSpec((rows, d), lambda j: (0, 0)),
            pl.BlockSpec((d, tn), lambda j: (0, j)),
            pl.BlockSpec((1, tn), lambda j: (0, j)),
        ],
        out_specs=pl.BlockSpec((rows, tn), lambda j: (0, j)),
        out_shape=jax.ShapeDtypeStruct((rows, n), F32),
        compiler_params=pltpu.CompilerParams(
            dimension_semantics=("arbitrary",), vmem_limit_bytes=VMEM_LIMIT),
        name="adaln",
    )(c_all, w_ada, b_ada)


def _inproj_kernel(x_ref, mod_ref, nw_ref, w_ref, *out_refs, widths, n_sub):
    rows = x_ref.shape[1] // n_sub

    def prep(s):
        x = x_ref[0, s * rows:(s + 1) * rows, :]
        ms = jnp.mean(x * x, axis=-1, keepdims=True)
        h = (x * lax.rsqrt(ms + EPS)) * nw_ref[...]
        return (h * (1.0 + mod_ref[0, 1:2, :]) + mod_ref[0, 0:1, :]).astype(BF16)

    hb = prep(0)
    for s in range(n_sub):
        nxt = prep(s + 1) if s + 1 < n_sub else None
        off = 0
        for o_ref, n in zip(out_refs, widths):
            o_ref[0, s * rows:(s + 1) * rows, :] = jnp.dot(
                hb, w_ref[:, off:off + n], preferred_element_type=F32).astype(o_ref.dtype)
            off += n
        hb = nxt


def _in_proj(x, mods, norm_w, w_cols, *, groups, per_batch_mod, tm):
    b, l, d = x.shape
    widths = tuple(n for n, _ in groups)
    assert sum(widths) == w_cols.shape[1]
    tm = min(tm, l)
    mod_map = (lambda bi, i: (bi, 0, 0)) if per_batch_mod else (lambda bi, i: (0, 0, 0))
    row_map = lambda bi, i: (bi, i, 0)
    return pl.pallas_call(
        functools.partial(_inproj_kernel, widths=widths, n_sub=max(1, tm // IN_SUB_ROWS)),
        grid=(b, l // tm),
        in_specs=[
            pl.BlockSpec((1, tm, d), row_map),
            pl.BlockSpec((1, 6, d), mod_map),
            pl.BlockSpec((1, d), lambda bi, i: (0, 0)),
            pl.BlockSpec(w_cols.shape, lambda bi, i: (0, 0)),
        ],
        out_specs=[pl.BlockSpec((1, tm, n), row_map) for n in widths],
        out_shape=[jax.ShapeDtypeStruct((b, l, n), dt) for n, dt in groups],
        compiler_params=pltpu.CompilerParams(
            dimension_semantics=("arbitrary", "arbitrary"), vmem_limit_bytes=VMEM_LIMIT),
        name="in_proj",
    )(x, mods, norm_w, w_cols)


def _dn_kernel(qx_ref, kx_ref, vx_ref, qc_ref, kc_ref, vc_ref, z_ref,
               gcx_ref, gcc_ref,
               cwq_ref, cwk_ref, cwv_ref, pc_ref, onw_ref, sh_ref,
               out_ref,
               q_s, k_s, v_s, gc_s, gr_s, s_s, o_s, *, hb, seq, ctx_len):
    n_ctx = ctx_len // CHUNK
    n_lat = seq // CHUNK
    tile = CONV_TILE
    win = tile + 2 * CONV_HALO
    side_taps = [j for j in range(CONV_K) if j != CONV_K // 2]

    step = pl.program_id(0)
    slot_p = step % 2
    slot_c = 1 - slot_p

    streams = ((cwq_ref, q_s, "q"), (cwk_ref, k_s, "k"), (cwv_ref, v_s, "v"))
    lat_srcs = (qx_ref, kx_ref, vx_ref)
    ctx_srcs = (qc_ref, kc_ref, vc_ref)

    def conv_stages(srcs, n_rows, dst_off, t):
        n_tiles = n_rows // tile
        if isinstance(t, int):
            r0 = t * tile
            a0 = min(max(r0 - CONV_HALO, 0), n_rows - win)
            variant = 0 if t == 0 else (2 if t == n_tiles - 1 else 1)
        else:
            r0 = pl.multiple_of(t * tile, tile)
            a0 = pl.multiple_of(jnp.clip(r0 - CONV_HALO, 0, n_rows - win), CONV_HALO)
            variant = jnp.where(t == 0, 0, jnp.where(t == n_tiles - 1, 2, 1))
        sh = sh_ref[variant]
        zs = [jnp.dot(sh, src[0, pl.ds(a0, win), :], preferred_element_type=F32) for src in srcs]
        yield
        ys = []
        for src, z, (cw_ref, _, _) in zip(srcs, zs, streams):
            cw = cw_ref[...]
            acc = cw[CONV_K // 2:CONV_K // 2 + 1, :] * src[0, pl.ds(r0, tile), :].astype(F32)
            for n, j in enumerate(side_taps):
                acc = acc + cw[j:j + 1, :] * z[n * tile:(n + 1) * tile, :]
            ys.append(_silu(acc))
            yield
        for y, (_, dst_ref, mode) in zip(ys, streams):
            if mode != "v":
                parts = []
                for hh in range(hb):
                    seg = y[:, hh * DN_HEAD_DIM:(hh + 1) * DN_HEAD_DIM]
                    inv = lax.rsqrt(jnp.sum(seg * seg, axis=-1, keepdims=True) + EPS)
                    if mode == "q":
                        inv = inv * (DN_HEAD_DIM ** -0.5)
                    parts.append(seg * inv)
                y = parts[0] if hb == 1 else jnp.concatenate(parts, axis=1)
            dst_ref[slot_p, pl.ds(dst_off + r0, tile), :] = y.astype(dst_ref.dtype)
            yield

    def conv_tile(srcs, n_rows, dst_off, t):
        for _ in conv_stages(srcs, n_rows, dst_off, t):
            pass

    @pl.when(step == 0)
    def _():
        for t in range(ctx_len // tile):
            conv_tile(ctx_srcs, ctx_len, 0, t)

        def body(t, carry):
            conv_tile(lat_srcs, seq, ctx_len, t)
            return carry

        lax.fori_loop(0, seq // tile, body, 0)

    n_hp = hb // 2
    ri = lax.broadcasted_iota(jnp.int32, (CHUNK, CHUNK), 0)
    ci = lax.broadcasted_iota(jnp.int32, (CHUNK, CHUNK), 1)
    lower = jnp.where(ci <= ri, 1.0, 0.0).astype(BF16)
    upper = jnp.where(ci >= ri, 1.0, 0.0).astype(BF16)
    tri_c = jnp.concatenate([lower, upper], axis=1)
    a_log_c, dt_c = pc_ref[0, 0:1, :], pc_ref[0, 1:2, :]
    lane = lax.broadcasted_iota(jnp.int32, (CHUNK, LANES), 1)
    pr = lax.broadcasted_iota(jnp.int32, (8, LANES), 0)
    pl_ = lax.broadcasted_iota(jnp.int32, (8, LANES), 1)
    pick = jnp.where((pr < 2 * n_hp) & (pl_ == (pr // n_hp) * hb + 2 * (pr % n_hp)), 1.0, 0.0).astype(BF16)

    def split3(x):
        x1 = x.astype(BF16)
        r1 = x - x1.astype(F32)
        x2 = r1.astype(BF16)
        x3 = (r1 - x2.astype(F32)).astype(BF16)
        return x1, x2, x3

    def gate_stages(gc_ref, ts, off):
        xc = [gc_ref[0, t] for t in ts]
        val = [jnp.where(lane < 2 * hb, -jnp.exp(a_log_c) * _softplus(x + dt_c), 1.0 / (1.0 + jnp.exp(-x)))
               for x in xc]
        yield
        gcs = [split3(jnp.concatenate([jnp.where(lane < hb, v, 0.0),
                                       jnp.where((lane >= hb) & (lane < 2 * hb), v, 0.0)], axis=0))
               for v in val]
        gam_c = [sum(jnp.dot(tri_c, p, preferred_element_type=F32) for p in ps) for ps in gcs]
        yield
        both = [split3(jnp.concatenate([g, pltpu.roll(g, LANES - 1, 1)], axis=0)) for g in gam_c]
        gam_r = [sum(lax.dot_general(pick, p, (((1,), (1,)), ((), ())), preferred_element_type=F32)
                     for p in ps) for ps in both]
        yield
        for u, t in enumerate(ts):
            gc_s[slot_p, off + t] = jnp.where(lane < 2 * hb, gam_c[u], val[u])
            gr_s[slot_p, off + t] = gam_r[u]
        yield

    def gates_into(gc_ref, n_chunks, off):
        unroll = min(GATE_UNROLL, n_chunks)

        def body(it, carry):
            for _ in gate_stages(gc_ref, [it * unroll + u for u in range(unroll)], off):
                pass
            return carry

        lax.fori_loop(0, n_chunks // unroll, body, 0)

    @pl.when(step == 0)
    def _():
        gates_into(gcc_ref, n_ctx, 0)
        gates_into(gcx_ref, n_lat, n_ctx)

    onw = onw_ref[...]
    prow = lax.broadcasted_iota(jnp.int32, (CHUNK, LANES), 0)
    pcol = lane % CHUNK
    lo = lane < CHUNK
    eye_p = jnp.where(pcol == prow, 1.0, 0.0)
    before_eq = (pcol <= prow, pcol >= prow)
    strict = (pcol < prow, pcol > prow)
    zeros_w = jnp.zeros((CHUNK, 2 * DN_HEAD_DIM), BF16)

    def block_diag(x):
        xb = x.astype(BF16)
        zero = jnp.zeros_like(xb)
        return jnp.concatenate([jnp.where(lo, xb, zero), jnp.where(lo, zero, xb)], axis=0)

    def chunk_steps(groups, out_mode, side=None):
        def tick():
            if side is not None:
                next(side, None)

        pairs = [p for grp in groups for p in grp]
        rng = range(len(pairs))
        r0s = [pl.multiple_of(chunk * CHUNK, CHUNK) for _, _, chunk in pairs]
        wide = [slice(hp * 2 * DN_HEAD_DIM, (hp + 1) * 2 * DN_HEAD_DIM) for _, hp, _ in pairs]
        q = [q_s[slot_c, pl.ds(r0s[i], CHUNK), wide[i]].astype(F32) for i in rng]
        k = [k_s[slot_c, pl.ds(r0s[i], CHUNK), wide[i]].astype(F32) for i in rng]
        v = [v_s[slot_c, pl.ds(r0s[i], CHUNK), wide[i]].astype(F32) for i in rng]
        gcol = [gc_s[slot_c, pairs[i][2]] for i in rng]
        grow = [gr_s[slot_c, pairs[i][2]] for i in rng]
        cs = [[d * hb + 2 * hp + e for e in range(2)] for d, hp, _ in pairs]
        dirs = [d for d, _, _ in pairs]
        gam_c = [[gcol[i][:, c:c + 1] for c in cs[i]] for i in rng]
        beta_c = [[gcol[i][:, 2 * hb + c:2 * hb + c + 1] for c in cs[i]] for i in rng]
        g_tot = [[g[CHUNK - 1:CHUNK, :] if dirs[i] == 0 else g[0:1, :] for g in gam_c[i]] for i in rng]
        gam_cp = [jnp.where(lo, gam_c[i][0], gam_c[i][1]) for i in rng]
        beta_cp = [jnp.where(lo, beta_c[i][0], beta_c[i][1]) for i in rng]
        gam_rp = [grow[i][dirs[i] * n_hp + pairs[i][1]:dirs[i] * n_hp + pairs[i][1] + 1, :] for i in rng]
        decay = [jnp.exp(jnp.where(before_eq[dirs[i]], gam_cp[i] - gam_rp[i], -jnp.inf)) for i in rng]
        kb = [k[i].astype(BF16) for i in rng]
        k_bd = [jnp.concatenate([jnp.concatenate([kb[i][:, :DN_HEAD_DIM], zeros_w[:, :DN_HEAD_DIM]], axis=1),
                                 jnp.concatenate([zeros_w[:, :DN_HEAD_DIM], kb[i][:, DN_HEAD_DIM:]], axis=1)],
                                axis=0) for i in rng]
        qk_kk = [_mm_nt(jnp.concatenate([q[i], k[i]], axis=0), k_bd[i]) for i in rng]
        tick()
        qkd = [qk_kk[i][:CHUNK] * decay[i] for i in rng]
        m = [jnp.where(strict[dirs[i]], qk_kk[i][CHUNK:] * beta_cp[i] * decay[i], 0.0) for i in rng]
        t_inv = [eye_p - m[i] for i in rng]
        pw = [_mm(m[i], block_diag(m[i])) for i in rng]
        tick()
        for _ in range(4):
            res = [_mm(jnp.concatenate([pw[i], t_inv[i]], axis=0), block_diag(pw[i])) for i in rng]
            tick()
            pw = [res[i][:CHUNK] for i in rng]
            t_inv = [t_inv[i] + res[i][CHUNK:] for i in rng]
        t_inv = [t_inv[i] + _mm(t_inv[i], block_diag(pw[i])) for i in rng]
        e_gam = [[jnp.exp(g) for g in gam_c[i]] for i in rng]
        hd = [slice(0, DN_HEAD_DIM), slice(DN_HEAD_DIM, 2 * DN_HEAD_DIM)]
        rhs = [[jnp.concatenate([v[i][:, hd[e]] * beta_c[i][e], k[i][:, hd[e]] * (beta_c[i][e] * e_gam[i][e])],
                                axis=1).astype(BF16) for e in range(2)] for i in rng]
        rhs_bd = [jnp.concatenate([jnp.concatenate([rhs[i][0], zeros_w], axis=1),
                                   jnp.concatenate([zeros_w, rhs[i][1]], axis=1)], axis=0) for i in rng]
        uw = [_mm(t_inv[i], rhs_bd[i]) for i in rng]
        tick()
        u_h = [[uw[i][:, (2 * e) * DN_HEAD_DIM:(2 * e + 1) * DN_HEAD_DIM] for e in range(2)] for i in rng]
        w_h = [[uw[i][:, (2 * e + 1) * DN_HEAD_DIM:(2 * e + 2) * DN_HEAD_DIM] for e in range(2)] for i in rng]
        kdec = [[(k[i][:, hd[e]] * jnp.exp(g_tot[i][e] - gam_c[i][e])).astype(BF16) for e in range(2)] for i in rng]
        s_scale = [[jnp.exp(g) for g in g_tot[i]] for i in rng]
        if out_mode is None:
            lhs = [[w_h[i][e].astype(BF16) for e in range(2)] for i in rng]
        else:
            lhs = [[jnp.concatenate([q[i][:, hd[e]] * e_gam[i][e], w_h[i][e]], axis=0).astype(BF16)
                    for e in range(2)] for i in rng]

        state = {}
        pos = 0
        for grp in groups:
            idx = range(pos, pos + len(grp))
            pos += len(grp)
            ch = [(i, e) for i in idx for e in range(2)]
            s_old = {(i, e): (state[cs[i][e]] if cs[i][e] in state else s_s[cs[i][e]]) for i, e in ch}
            prod = {(i, e): _mm(lhs[i][e], s_old[(i, e)]) for i, e in ch}
            tick()
            v_new = {(i, e): u_h[i][e] - prod[(i, e)][-CHUNK:] for i, e in ch}
            for i, e in ch:
                state[cs[i][e]] = s_scale[i][e] * s_old[(i, e)] + _mm_tn(kdec[i][e], v_new[(i, e)])
            tick()
            if out_mode is None:
                continue
            zeros_h = zeros_w[:, :DN_HEAD_DIM]
            v_bd = {i: jnp.concatenate(
                [jnp.concatenate([v_new[(i, 0)].astype(BF16), zeros_h], axis=1),
                 jnp.concatenate([zeros_h, v_new[(i, 1)].astype(BF16)], axis=1)], axis=0) for i in idx}
            intra = {i: _mm(qkd[i], v_bd[i]) for i in idx}
            for i, e in ch:
                o = prod[(i, e)][:CHUNK] + intra[i][:, hd[e]]
                rl = pl.multiple_of(r0s[i] - ctx_len, CHUNK)
                hh = 2 * pairs[i][1] + e
                hs = slice(hh * DN_HEAD_DIM, (hh + 1) * DN_HEAD_DIM)
                if out_mode == "store":
                    o_s[pl.ds(rl, CHUNK), hs] = o
                else:
                    tot = o + o_s[pl.ds(rl, CHUNK), hs]
                    y = tot * lax.rsqrt(jnp.mean(tot * tot, axis=-1, keepdims=True) + EPS) * onw
                    zz = z_ref[0, pl.ds(rl, CHUNK), hs].astype(F32)
                    out_ref[0, pl.ds(rl, CHUNK), hs] = (y * _silu(zz)).astype(out_ref.dtype)
        for c, val in state.items():
            s_s[c] = val
        if side is not None:
            for _ in side:
                pass

    head_pairs = range(n_hp)

    def groups_at(first, last, i):
        return [(0, hp, first + i) for hp in head_pairs] + [(1, hp, last - i) for hp in head_pairs]

    @pl.when(step > 0)
    def _():
        s_s[...] = jnp.zeros(s_s.shape, F32)

        def side_work(srcs, n_rows, dst_off, gc_ref, chunk_off, j, cps):
            for u in range(cps // 2):
                yield from conv_stages(srcs, n_rows, dst_off, (cps // 2) * j + u)
            yield from gate_stages(gc_ref, [cps * j + u for u in range(cps)], chunk_off)

        cps_c = min(DN_CHUNKS_PER_STEP, n_ctx)
        for j in range(n_ctx // cps_c):
            chunk_steps([groups_at(0, n_ctx - 1, cps_c * j + u) for u in range(cps_c)], None,
                        side=side_work(ctx_srcs, ctx_len, 0, gcc_ref, 0, j, cps_c))

        cps = DN_CHUNKS_PER_STEP

        def lat_body(mode, j, carry):
            first, last = n_ctx, n_ctx + n_lat - 1
            chunk_steps([groups_at(first, last, cps * j + u) for u in range(cps)], mode,
                        side=side_work(lat_srcs, seq, ctx_len, gcx_ref, n_ctx, j, cps))
            return carry

        half = n_lat // (2 * cps)
        lax.fori_loop(0, half, functools.partial(lat_body, "store"), 0)
        lax.fori_loop(half, 2 * half, functools.partial(lat_body, "final"), 0)


def _conv_shift_matrices():
    side = [j - CONV_K // 2 for j in range(CONV_K) if j != CONV_K // 2]
    win = CONV_TILE + 2 * CONV_HALO
    sh = np.zeros((3, len(side) * CONV_TILE, win), np.float32)
    for d in range(3):
        for n, off in enumerate(side):
            for i in range(CONV_TILE):
                m = i + off + d * CONV_HALO
                if 0 <= m < win:
                    sh[d, n * CONV_TILE + i, m] = 1.0
    return jnp.asarray(sh, BF16)


def _deltanet(qkv_x, qkv_c, z_x, gcol_x, gcol_c, conv_w, pcol, onw, *, hb):
    b, seq, _ = qkv_x.shape
    ctx_len = qkv_c.shape[1]
    ng = DN_HEADS // hb
    w = hb * DN_HEAD_DIM
    n_lat, n_ctx = seq // CHUNK, ctx_len // CHUNK
    assert n_lat % (2 * DN_CHUNKS_PER_STEP) == 0 and n_ctx % 2 == 0 and DN_CHUNKS_PER_STEP % 2 == 0
    assert DN_CHUNKS_PER_STEP % n_ctx == 0 or n_ctx % DN_CHUNKS_PER_STEP == 0
    assert seq >= 2 * CONV_TILE and ctx_len >= 2 * CONV_TILE
    assert n_lat % GATE_UNROLL == 0 and GATE_UNROLL % n_ctx == 0 and hb % 2 == 0 and hb <= 8
    sh = _conv_shift_matrices()

    assert CONV_TILE == 2 * CHUNK
    assert ng == 1
    last = b - 1

    def col(part):
        return lambda s, g: (jnp.minimum(s, last), 0, part * ng + g)

    def cur(*rest):
        return lambda s, g: (jnp.maximum(s - 1, 0),) + tuple(g if r == "g" else r for r in rest)

    def cwcol(part):
        return lambda bi, g: (0, part * ng + g)

    return pl.pallas_call(
        functools.partial(_dn_kernel, hb=hb, seq=seq, ctx_len=ctx_len),
        grid=(b + 1, ng),
        in_specs=[
            pl.BlockSpec((1, seq, w), col(0)),
            pl.BlockSpec((1, seq, w), col(1)),
            pl.BlockSpec((1, seq, w), col(2)),
            pl.BlockSpec((1, ctx_len, w), col(0)),
            pl.BlockSpec((1, ctx_len, w), col(1)),
            pl.BlockSpec((1, ctx_len, w), col(2)),
            pl.BlockSpec((1, seq, w), cur(0, "g")),
            pl.BlockSpec((1, n_lat, CHUNK, LANES), lambda s, g: (jnp.minimum(s, last), 0, 0, g)),
            pl.BlockSpec((1, n_ctx, CHUNK, LANES), lambda s, g: (jnp.minimum(s, last), 0, 0, g)),
            pl.BlockSpec((CONV_K, w), cwcol(0)),
            pl.BlockSpec((CONV_K, w), cwcol(1)),
            pl.BlockSpec((CONV_K, w), cwcol(2)),
            pl.BlockSpec((1, 2, LANES), lambda bi, g: (g, 0, 0)),
            pl.BlockSpec((1, DN_HEAD_DIM), lambda bi, g: (0, 0)),
            pl.BlockSpec(sh.shape, lambda bi, g: (0, 0, 0)),
        ],
        out_specs=pl.BlockSpec((1, seq, w), cur(0, "g")),
        out_shape=jax.ShapeDtypeStruct((b, seq, DN_HEADS * DN_HEAD_DIM), BF16),
        scratch_shapes=[
            pltpu.VMEM((2, ctx_len + seq, w), BF16),
            pltpu.VMEM((2, ctx_len + seq, w), BF16),
            pltpu.VMEM((2, ctx_len + seq, w), BF16),
            pltpu.VMEM((2, n_ctx + n_lat, CHUNK, LANES), F32),
            pltpu.VMEM((2, n_ctx + n_lat, 8, LANES), F32),
            pltpu.VMEM((2 * hb, DN_HEAD_DIM, DN_HEAD_DIM), F32),
            pltpu.VMEM((seq, w), F32),
        ],
        compiler_params=pltpu.CompilerParams(
            dimension_semantics=("arbitrary", "arbitrary"), vmem_limit_bytes=VMEM_LIMIT),
        name="deltanet",
    )(qkv_x, qkv_x, qkv_x, qkv_c, qkv_c, qkv_c, z_x, gcol_x, gcol_c,
      conv_w, conv_w, conv_w, pcol, onw, sh)


def _na_kernel(q_ref, k_ref, v_ref, kc_ref, vc_ref, qw_ref, kw_ref, bias_ref, seg_ref, out_ref,
               qn_s, kn_s, knc_s, *, seq, ctx_len):
    step = pl.program_id(0)
    slot_p = step % 2
    slot_c = 1 - slot_p
    rows = seq // GRID_W
    win_rows = min(WIN_ROWS, rows)
    n_win = win_rows * GRID_W
    n_pairs = NA_HEADS // 2

    def head_rms(x, wgt):
        ss = jnp.dot((x * x).astype(BF16), seg_ref[...], preferred_element_type=F32)
        return x * lax.rsqrt(ss * (1.0 / NA_HEAD_DIM) + EPS) * wgt

    kw = kw_ref[...]
    qw = qw_ref[...] * ((NA_HEAD_DIM ** -0.5) * LOG2E)
    tile = 256

    def norm_stages(t):
        r0 = pl.multiple_of(t * tile, tile)
        kn = head_rms(k_ref[0, pl.ds(r0, tile), :].astype(F32), kw)
        yield
        kn_s[slot_p, pl.ds(r0, tile), :] = kn.astype(BF16)
        qn = head_rms(q_ref[0, pl.ds(r0, tile), :].astype(F32), qw)
        yield
        qn_s[slot_p, pl.ds(r0, tile), :] = qn.astype(BF16)
        yield

    n_norm = seq // tile
    knc_s[slot_p] = head_rms(kc_ref[0].astype(F32), kw).astype(BF16)

    @pl.when(step == 0)
    def _():
        def norm_body(t, carry):
            for _ in norm_stages(t):
                pass
            return carry

        lax.fori_loop(0, n_norm, norm_body, 0)

    lane2 = lax.broadcasted_iota(jnp.int32, (GRID_W, LANES), 1)
    lo2 = lane2 < NA_HEAD_DIM
    ones_w = jnp.ones((n_win, LANES), BF16)
    ones_c = jnp.ones((ctx_len, LANES), BF16)
    zero = jnp.zeros((GRID_W, LANES), BF16)

    n_row_steps = rows // NA_ROW_UNROLL
    norm_per_step = n_norm // n_row_steps

    def row_body(it, carry):
        def side_gen():
            for u in range(norm_per_step):
                yield from norm_stages(it * norm_per_step + u)

        side = side_gen()

        def tick():
            next(side, None)

        streams = [(u, j) for u in range(NA_ROW_UNROLL) for j in range(n_pairs)]
        ns = range(len(streams))
        row = [it * NA_ROW_UNROLL + u for u, _ in streams]
        r_start = [jnp.clip(r - win_rows // 2, 0, rows - win_rows) for r in row]
        case = [r_start[n] - row[n] + (WIN_ROWS - 1) for n in ns]
        b_copy = [case[n] % 2 for n in ns]
        b_lane = [pl.multiple_of((case[n] // 2) * LANES, LANES) for n in ns]
        q0 = [pl.multiple_of(r * GRID_W, GRID_W) for r in row]
        k0 = [pl.multiple_of(r_start[n] * GRID_W, GRID_W) for n in ns]
        lss = [slice(j * LANES, (j + 1) * LANES) for _, j in streams]
        q2 = [qn_s[slot_c, pl.ds(q0[n], GRID_W), lss[n]] for n in ns]
        qs = [jnp.concatenate([jnp.where(lo2, q2[n], zero), jnp.where(lo2, zero, q2[n])], axis=0)
              for n in ns]
        tick()
        s_w = [_mm_nt(qs[n], kn_s[slot_c, pl.ds(k0[n], n_win), lss[n]])
               + bias_ref[b_copy[n], streams[n][1], :, pl.ds(b_lane[n], n_win)] for n in ns]
        tick()
        s_c = [_mm_nt(qs[n], knc_s[slot_c, :, lss[n]]) for n in ns]
        tick()
        mx = [s_c[n][:, 0:LANES] for n in ns]
        for n in ns:
            for t in range(1, ctx_len // LANES):
                mx[n] = jnp.maximum(mx[n], s_c[n][:, t * LANES:(t + 1) * LANES])
            for t in range(n_win // LANES):
                mx[n] = jnp.maximum(mx[n], s_w[n][:, t * LANES:(t + 1) * LANES])
        tick()
        mx = [jnp.max(mx[n], axis=-1, keepdims=True) for n in ns]
        p_w = [jnp.exp2(s_w[n] - mx[n]).astype(BF16) for n in ns]
        tick()
        p_c = [jnp.exp2(s_c[n] - mx[n]).astype(BF16) for n in ns]
        tick()
        o = [_mm(p_w[n], jnp.concatenate([v_ref[0, pl.ds(k0[n], n_win), lss[n]], ones_w], axis=1))
             + _mm(p_c[n], jnp.concatenate([vc_ref[0, :, lss[n]], ones_c], axis=1)) for n in ns]
        for n in ns:
            on = o[n][:, :LANES] * (1.0 / o[n][:, LANES:])
            out_ref[0, pl.ds(q0[n], GRID_W), lss[n]] = jnp.where(lo2, on[:GRID_W], on[GRID_W:]).astype(out_ref.dtype)
        for _ in side:
            pass
        return carry

    @pl.when(step > 0)
    def _():
        lax.fori_loop(0, n_row_steps, row_body, 0)


def _natten(na_x, na_c, qw, kw, bias):
    b, seq, _ = na_x.shape
    ctx_len = na_c.shape[1]
    wq = NA_HEADS * NA_HEAD_DIM
    assert seq % 256 == 0 and ctx_len % LANES == 0 and (seq // GRID_W) % NA_ROW_UNROLL == 0
    assert (seq // 256) % (seq // GRID_W // NA_ROW_UNROLL) == 0
    last = b - 1
    nxt = lambda col: (lambda s: (jnp.minimum(s, last), 0, col))
    cur = lambda col: (lambda s: (jnp.maximum(s - 1, 0), 0, col))
    head_of = np.arange(wq) // NA_HEAD_DIM
    seg = jnp.asarray(head_of[:, None] == head_of[None, :], BF16)
    return pl.pallas_call(
        functools.partial(_na_kernel, seq=seq, ctx_len=ctx_len),
        grid=(b + 1,),
        in_specs=[
            pl.BlockSpec((1, seq, wq), nxt(0)),
            pl.BlockSpec((1, seq, wq), nxt(1)),
            pl.BlockSpec((1, seq, wq), cur(2)),
            pl.BlockSpec((1, ctx_len, wq), nxt(0)),
            pl.BlockSpec((1, ctx_len, wq), cur(1)),
            pl.BlockSpec((1, wq), lambda bi: (0, 0)),
            pl.BlockSpec((1, wq), lambda bi: (0, 0)),
            pl.BlockSpec(bias.shape, lambda bi: (0, 0, 0, 0)),
            pl.BlockSpec((wq, wq), lambda bi: (0, 0)),
        ],
        out_specs=pl.BlockSpec((1, seq, wq), cur(0)),
        out_shape=jax.ShapeDtypeStruct((b, seq, wq), BF16),
        scratch_shapes=[
            pltpu.VMEM((2, seq, wq), BF16),
            pltpu.VMEM((2, seq, wq), BF16),
            pltpu.VMEM((2, ctx_len, wq), BF16),
        ],
        compiler_params=pltpu.CompilerParams(
            dimension_semantics=("arbitrary",), vmem_limit_bytes=VMEM_LIMIT),
        name="natten",
    )(na_x, na_x, na_x, na_c, na_c, qw, kw, bias, seg)


def _na_bias_table(rpb, rows):
    assert min(WIN_ROWS, rows) == WIN_ROWS
    n_dr = 2 * WIN_ROWS - 1
    cols = np.arange(GRID_W)
    win_start = np.clip(cols - WIN_COLS // 2, 0, GRID_W - WIN_COLS)
    kc = cols[None, :]
    valid = (kc >= win_start[:, None]) & (kc < win_start[:, None] + WIN_COLS)
    rel = np.clip(kc - cols[:, None] + WIN_COLS - 1, 0, 2 * WIN_COLS - 2)
    pick_rel = (rel[None, :, :] == np.arange(2 * WIN_COLS - 1)[:, None, None]).astype(np.float32)
    tab = jnp.einsum("hdr,rck->hcdk", rpb * LOG2E, pick_rel, precision=HIGHEST)
    tab = jnp.where(valid[None, :, None, :], tab, -1e30).reshape(NA_HEADS // 2, 2 * GRID_W, n_dr * GRID_W)
    width = 8 * LANES
    assert (WIN_ROWS - 2) * GRID_W + WIN_ROWS * GRID_W <= width - GRID_W
    tab = jnp.pad(tab, ((0, 0), (0, 0), (0, width + GRID_W - n_dr * GRID_W)))
    return jnp.stack([tab[..., :width], tab[..., GRID_W:]], axis=0).astype(F32)


def _outffn_kernel(x_ref, dn_ref, na_ref, mod_ref, nw_ref, wo_ref, wi_ref, wf_ref, out_ref, *, d_ff, splits, n_sub):
    half = dn_ref.shape[-1]
    rows = x_ref.shape[1] // n_sub

    def prep(s):
        rs = slice(s * rows, (s + 1) * rows)
        attn = (jnp.dot(dn_ref[0, rs, :], wo_ref[0:half, :], preferred_element_type=F32)
                + jnp.dot(na_ref[0, rs, :], wo_ref[half:, :], preferred_element_type=F32))
        x1 = x_ref[0, rs, :] + mod_ref[0, 2:3, :] * attn
        ms = jnp.mean(x1 * x1, axis=-1, keepdims=True)
        h = (x1 * lax.rsqrt(ms + EPS)) * nw_ref[...]
        return x1, (h * (1.0 + mod_ref[0, 4:5, :]) + mod_ref[0, 3:4, :]).astype(BF16)

    cur = prep(0)
    for s in range(n_sub):
        nxt = prep(s + 1) if s + 1 < n_sub else None
        x1, hb = cur
        acc = None
        off = 0
        for n in splits:
            gate = jnp.dot(hb, wi_ref[:, off:off + n], preferred_element_type=F32)
            up = jnp.dot(hb, wi_ref[:, d_ff + off:d_ff + off + n], preferred_element_type=F32)
            part = jnp.dot((_silu(gate) * up).astype(BF16), wf_ref[off:off + n, :], preferred_element_type=F32)
            acc = part if acc is None else acc + part
            off += n
        out_ref[0, s * rows:(s + 1) * rows, :] = x1 + mod_ref[0, 5:6, :] * acc
        cur = nxt


def _out_ffn(x, dn, na, mods, norm_w, w_out, w_ffn_in, w_ffn_out, *, tm):
    b, l, d = x.shape
    half = dn.shape[-1]
    d_ff = w_ffn_out.shape[0]
    assert sum(FFN_SPLITS) == d_ff
    tm = min(tm, l)
    row_map = lambda bi, i: (bi, i, 0)
    const = lambda bi, i: (0, 0)
    single = pl.Buffered(1)
    return pl.pallas_call(
        functools.partial(_outffn_kernel, d_ff=d_ff, splits=FFN_SPLITS, n_sub=max(1, tm // FFN_SUB_ROWS)),
        grid=(b, l // tm),
        in_specs=[
            pl.BlockSpec((1, tm, d), row_map),
            pl.BlockSpec((1, tm, half), row_map),
            pl.BlockSpec((1, tm, half), row_map),
            pl.BlockSpec((1, 6, d), lambda bi, i: (bi, 0, 0)),
            pl.BlockSpec((1, d), const),
            pl.BlockSpec(w_out.shape, const, pipeline_mode=single),
            pl.BlockSpec(w_ffn_in.shape, const, pipeline_mode=single),
            pl.BlockSpec(w_ffn_out.shape, const, pipeline_mode=single),
        ],
        out_specs=pl.BlockSpec((1, tm, d), row_map),
        out_shape=jax.ShapeDtypeStruct((b, l, d), F32),
        compiler_params=pltpu.CompilerParams(
            dimension_semantics=("arbitrary", "arbitrary"), vmem_limit_bytes=VMEM_LIMIT),
        name="out_ffn",
    )(x, dn, na, mods, norm_w, w_out, w_ffn_in, w_ffn_out)


def kernel(x, c, ctx, c_ctx, norm1_w, norm2_w, w_ada, b_ada, w_in, dn_conv_w, dn_A_log, dn_dt_bias,
           dn_out_norm_w, na_q_norm_w, na_k_norm_w, na_rpb, w_out, w_ffn_in, w_ffn_out):
    assert w_in.shape[0] == 1, "single-layer problem"
    b, seq, d = x.shape
    hb = DN_HB
    ng = DN_HEADS // hb
    dn_w = DN_HEADS * DN_HEAD_DIM
    na_w = NA_HEADS * NA_HEAD_DIM
    rows = seq // GRID_W

    n_mod_rows = -(-(b + 1) // 8) * 8
    c_all = jnp.zeros((n_mod_rows, d), F32).at[:b].set(c).at[b].set(c_ctx)
    mod = _adaln(c_all, w_ada[0], b_ada[0][None, :])
    mod_x = mod[:b].reshape(b, 6, d)
    mod_c = mod[b:b + 1].reshape(1, 6, d)

    w0 = w_in[0]
    gate0 = 4 * dn_w
    na0 = gate0 + 4 * DN_HEADS
    gate_cols = w0[:, gate0:na0].reshape(d, 4, ng, hb)
    gate_cols = jnp.transpose(gate_cols, (0, 2, 1, 3)).reshape(d, ng, 4 * hb)
    gate_cols = jnp.pad(gate_cols, ((0, 0), (0, 0), (0, LANES - 4 * hb))).reshape(d, ng * LANES)
    w_lat = jnp.concatenate([w0[:, :gate0], w0[:, na0:], gate_cols], axis=1).astype(BF16)
    w_ctx = jnp.concatenate([w0[:, :3 * dn_w], w0[:, na0 + na_w:], gate_cols], axis=1).astype(BF16)
    n_gate = ng * LANES
    nw1 = norm1_w[0][None, :]
    qkv_x, z_x, na_x, ab_x = _in_proj(
        x, mod_x, nw1, w_lat, per_batch_mod=True, tm=IN_ROW_TILE,
        groups=((3 * dn_w, BF16), (dn_w, BF16), (3 * na_w, BF16), (n_gate, F32)))
    qkv_c, na_c, ab_c = _in_proj(
        ctx, mod_c, nw1, w_ctx, per_batch_mod=False, tm=IN_ROW_TILE,
        groups=((3 * dn_w, BF16), (2 * na_w, BF16), (n_gate, F32)))

    def group_params(p):
        return jnp.transpose(p[0].reshape(2, ng, hb), (1, 0, 2)).reshape(ng, 2 * hb)

    pg = jnp.stack([group_params(dn_A_log), group_params(dn_dt_bias)], axis=1)
    pcol = jnp.pad(pg, ((0, 0), (0, 0), (0, LANES - 2 * hb)))
    gcol_x = ab_x.reshape(b, seq // CHUNK, CHUNK, ng * LANES)
    gcol_c = ab_c.reshape(b, ctx.shape[1] // CHUNK, CHUNK, ng * LANES)
    dn_x = _deltanet(qkv_x, qkv_c, z_x, gcol_x, gcol_c, dn_conv_w[0], pcol,
                     dn_out_norm_w[0][None, :], hb=hb)

    bias = _na_bias_table(na_rpb[0], rows)
    qw = jnp.tile(na_q_norm_w[0], NA_HEADS)[None, :]
    kw = jnp.tile(na_k_norm_w[0], NA_HEADS)[None, :]
    na_o = _natten(na_x, na_c, qw, kw, bias)

    return _out_ffn(x, dn_x, na_o, mod_x, norm2_w[0][None, :], w_out[0].astype(BF16),
                    w_ffn_in[0].astype(BF16), w_ffn_out[0].astype(BF16), tm=ROW_TILE)
```
